```python
import jax
import jax.numpy as jnp
from jax import lax
import numpy as np


D_MODEL = 1024
BATCH = 8
SEQ = 8192
DEPTH = 2

HEAD_DIM = 64
ROPE_THETA = 10000.0
EPS = 1e-6
Q_BLOCK = 128

A_HEADS = 4
IDX_HEADS = 8
IDX_DIM = 32
TOPK_MAX = 256

B_HEADS = 4
B_NOPE = 64
B_ROPE = 32
B_V = 64
B_Q_RANK = 256
B_KV_RANK = 128

C_HEADS = 8
C_KV_HEADS = 2
WINDOW = 128

D_FF = 2816
N_EXPERTS = 8
TOP_K = 2
EXPERT_FF = 3584
MOE_BLOCK = 256

N_DENSE = (DEPTH + 1) // 2
N_MOE = DEPTH // 2

IN_SIZES = (A_HEADS * HEAD_DIM, HEAD_DIM, HEAD_DIM,
            IDX_HEADS * IDX_DIM, IDX_DIM, IDX_HEADS,
            B_Q_RANK, B_KV_RANK, B_ROPE,
            C_HEADS * HEAD_DIM, C_KV_HEADS * HEAD_DIM, C_KV_HEADS * HEAD_DIM,
            3 * D_MODEL)
D_IN = sum(IN_SIZES)

kernel_name = 'hybrid_dsa_mla_swa_gated_moe'


def rms_norm(x, g):
    xf = x.astype(jnp.float32)
    y = xf * lax.rsqrt(jnp.mean(xf * xf, axis=-1, keepdims=True) + EPS)
    return (y * g.astype(jnp.float32)).astype(x.dtype)


def rope(x, pos):
    half = x.shape[-1] // 2
    inv_freq = ROPE_THETA ** (-jnp.arange(half, dtype=jnp.float32) / half)
    ang = pos.astype(jnp.float32)[:, :, None] * inv_freq
    cos = jnp.cos(ang)[:, :, None, :]
    sin = jnp.sin(ang)[:, :, None, :]
    xf = x.astype(jnp.float32)
    x1, x2 = xf[..., :half], xf[..., half:]
    return jnp.concatenate([x1 * cos - x2 * sin, x2 * cos + x1 * sin], axis=-1).astype(x.dtype)


def to_blocks(x, blk=Q_BLOCK):
    b, s = x.shape[0], x.shape[1]
    return jnp.moveaxis(x.reshape((b, s // blk, blk) + x.shape[2:]), 1, 0)


def from_blocks(y):
    n, b, blk = y.shape[0], y.shape[1], y.shape[2]
    return jnp.moveaxis(y, 0, 1).reshape((b, n * blk) + y.shape[3:])


def split_columns(p):
    cuts = np.cumsum(IN_SIZES)[:-1].tolist()
    return jnp.split(p, cuts, axis=-1)


def dsa_attention(q, k, v, iq, ik, iw):
    s = q.shape[1]
    n_sel = min(TOPK_MAX, s // 4)
    key_pos = jnp.arange(s)

    def block(args):
        qb, iqb, iwb, t = args
        sc = jnp.einsum('bqhd,bsd->bqhs', iqb, ik, preferred_element_type=jnp.float32) * IDX_DIM ** -0.5
        score = jnp.einsum('bqhs,bqh->bqs', jax.nn.relu(sc), iwb.astype(jnp.float32))
        score = jnp.where((key_pos[None, :] <= t[:, None])[None], score, -jnp.inf)
        _, sel = lax.top_k(score, n_sel)
        valid = sel <= t[None, :, None]
        k_g = jax.vmap(lambda kk, ii: kk[ii])(k, sel)
        v_g = jax.vmap(lambda vv, ii: vv[ii])(v, sel)
        logits = jnp.einsum('bqhd,bqkd->bqhk', qb, k_g, preferred_element_type=jnp.float32) * HEAD_DIM ** -0.5
        logits = jnp.where(valid[:, :, None, :], logits, -jnp.inf)
        p = jax.nn.softmax(logits, axis=-1)
        return jnp.einsum('bqhk,bqkd->bqhd', p.astype(v.dtype), v_g)

    t_blocks = jnp.arange(s).reshape(s // Q_BLOCK, Q_BLOCK)
    out = lax.map(block, (to_blocks(q), to_blocks(iq), to_blocks(iw), t_blocks))
    return from_blocks(out)


def mla_attention(q_nope, q_rope, k_nope, k_rope, v):
    s = q_nope.shape[1]
    key_pos = jnp.arange(s)
    scale = (B_NOPE + B_ROPE) ** -0.5

    def block(args):
        qn, qr, t = args
        logits = (jnp.einsum('bqhd,bshd->bhqs', qn, k_nope, preferred_element_type=jnp.float32)
                  + jnp.einsum('bqhr,bsr->bhqs', qr, k_rope, preferred_element_type=jnp.float32)) * scale
        logits = jnp.where(key_pos[None, :] <= t[:, None], logits, -jnp.inf)
        p = jax.nn.softmax(logits, axis=-1)
        return jnp.einsum('bhqs,bshd->bqhd', p.astype(v.dtype), v)

    t_blocks = jnp.arange(s).reshape(s // Q_BLOCK, Q_BLOCK)
    out = lax.map(block, (to_blocks(q_nope), to_blocks(q_rope), t_blocks))
    return from_blocks(out)


def swa_attention(q, k, v, sinks):
    b, s, _, d = q.shape
    g = C_HEADS // C_KV_HEADS
    n = s // WINDOW
    qb = q.reshape(b, n, WINDOW, C_KV_HEADS, g, d)

    def band(x):
        xb = x.reshape(b, n, WINDOW, C_KV_HEADS, d)
        prev = jnp.concatenate([jnp.zeros_like(xb[:, :1]), xb[:, :-1]], axis=1)
        return jnp.concatenate([prev, xb], axis=2)

    kb, vb = band(k), band(v)
    i = jnp.arange(WINDOW)[:, None]
    j = jnp.arange(2 * WINDOW)[None, :]
    in_window = (j > i) & (j <= i + WINDOW)
    sink = sinks.reshape(C_KV_HEADS, g).astype(jnp.float32)[None, :, :, None, None]

    def block(args):
        qn, kn, vn, blk = args
        mask = in_window & (blk * WINDOW - WINDOW + j >= 0)
        logits = jnp.einsum('bqkgd,bskd->bkgqs', qn, kn, preferred_element_type=jnp.float32) * d ** -0.5
        logits = jnp.where(mask, logits, -jnp.inf)
        m = jnp.maximum(jnp.max(logits, axis=-1, keepdims=True), sink)
        p = jnp.exp(logits - m)
        denom = jnp.sum(p, axis=-1, keepdims=True) + jnp.exp(sink - m)
        return jnp.einsum('bkgqs,bskd->bqkgd', (p / denom).astype(v.dtype), vn)

    out = lax.map(block, (jnp.moveaxis(qb, 1, 0), jnp.moveaxis(kb, 1, 0), jnp.moveaxis(vb, 1, 0), jnp.arange(n)))
    return jnp.moveaxis(out, 0, 1).reshape(b, s, C_HEADS, d)


def swiglu(h, w1, w3, w2):
    return (jax.nn.silu(h @ w1) * (h @ w3)) @ w2


def moe_ffn(h, w_router, w1, w3, w2):
    b, s, d = h.shape
    n_tok = b * s
    hf = h.reshape(n_tok, d)
    logits = jnp.matmul(hf, w_router, preferred_element_type=jnp.float32)
    top_logits, top_e = lax.top_k(logits, TOP_K)
    gates = jax.nn.softmax(top_logits, axis=-1)
    n_asg = n_tok * TOP_K
    e_flat = top_e.reshape(n_asg)
    tok_flat = jnp.repeat(jnp.arange(n_tok), TOP_K)
    order = jnp.argsort(e_flat)
    e_s, tok_s = e_flat[order], tok_flat[order]
    gate_s = gates.reshape(n_asg)[order]
    counts = jnp.bincount(e_flat, length=N_EXPERTS)
    padded = (counts + MOE_BLOCK - 1) // MOE_BLOCK * MOE_BLOCK
    start = jnp.cumsum(counts) - counts
    pend = jnp.cumsum(padded)
    pstart = pend - padded
    dest = pstart[e_s] + (jnp.arange(n_asg) - start[e_s])
    n_blocks = -(-n_asg // MOE_BLOCK) + N_EXPERTS
    rows = n_blocks * MOE_BLOCK
    xbuf = jnp.zeros((rows, d), h.dtype).at[dest].set(hf[tok_s])
    block_e = jnp.minimum(jnp.searchsorted(pend, jnp.arange(n_blocks) * MOE_BLOCK, side='right'), N_EXPERTS - 1)

    def expert_block(args):
        xb, e = args
        return swiglu(xb, w1[e], w3[e], w2[e])

    ybuf = lax.map(expert_block, (xbuf.reshape(n_blocks, MOE_BLOCK, d), block_e)).reshape(rows, d)
    y_s = ybuf[dest] * gate_s[:, None].astype(h.dtype)
    out = jax.ops.segment_sum(y_s, tok_s, num_segments=n_tok)
    return out.reshape(b, s, d)


def setup_inputs(seed: int = 0) -> dict:
    key = jax.random.key(seed)
    ks = jax.random.split(key, 29)

    def w(k, shape, fan_in):
        return jax.random.normal(k, shape, jnp.float32) * fan_in ** -0.5

    def gain(k, shape):
        return 1.0 + 0.1 * jax.random.normal(k, shape, jnp.float32)

    b_qk = B_HEADS * (B_NOPE + B_ROPE)
    b_kv = B_HEADS * (B_NOPE + B_V)
    return {
        'x': jax.random.normal(ks[0], (BATCH, SEQ, D_MODEL), jnp.float32),
        'positions': (jnp.arange(SEQ, dtype=jnp.int32)[None, :]
                      + jax.random.randint(ks[1], (BATCH, 1), 0, 4096, dtype=jnp.int32)),
        'attn_norm_g': gain(ks[2], (DEPTH, D_MODEL)),
        'w_in': w(ks[3], (DEPTH, D_MODEL, D_IN), D_MODEL),
        'a_q_norm_g': gain(ks[4], (DEPTH, HEAD_DIM)),
        'a_k_norm_g': gain(ks[5], (DEPTH, HEAD_DIM)),
        'b_cq_norm_g': gain(ks[6], (DEPTH, B_Q_RANK)),
        'b_ckv_norm_g': gain(ks[7], (DEPTH, B_KV_RANK)),
        'b_w_uq': w(ks[8], (DEPTH, B_Q_RANK, b_qk), B_Q_RANK),
        'b_w_ukv': w(ks[9], (DEPTH, B_KV_RANK, b_kv), B_KV_RANK),
        'b_qn_g': gain(ks[10], (DEPTH, B_NOPE)),
        'b_qr_g': gain(ks[11], (DEPTH, B_ROPE)),
        'b_kn_g': gain(ks[12], (DEPTH, B_NOPE)),
        'b_kr_g': gain(ks[13], (DEPTH, B_ROPE)),
        'c_q_norm_g': gain(ks[14], (DEPTH, HEAD_DIM)),
        'c_k_norm_g': gain(ks[15], (DEPTH, HEAD_DIM)),
        'c_sinks': 0.5 * jax.random.normal(ks[16], (DEPTH, C_HEADS), jnp.float32),
        'w_a_out': w(ks[17], (DEPTH, A_HEADS * HEAD_DIM, D_MODEL), A_HEADS * HEAD_DIM),
        'w_b_out': w(ks[18], (DEPTH, B_HEADS * B_V, D_MODEL), B_HEADS * B_V),
        'w_c_out': w(ks[19], (DEPTH, C_HEADS * HEAD_DIM, D_MODEL), C_HEADS * HEAD_DIM),
        'w_o': w(ks[20], (DEPTH, D_MODEL, D_MODEL), D_MODEL),
        'ffn_norm_g': gain(ks[21], (DEPTH, D_MODEL)),
        'ffn_w1': w(ks[22], (N_DENSE, D_MODEL, D_FF), D_MODEL),
        'ffn_w3': w(ks[23], (N_DENSE, D_MODEL, D_FF), D_MODEL),
        'ffn_w2': w(ks[24], (N_DENSE, D_FF, D_MODEL), D_FF),
        'router_w': w(ks[25], (N_MOE, D_MODEL, N_EXPERTS), D_MODEL),
        'moe_w1': w(ks[26], (N_MOE, N_EXPERTS, D_MODEL, EXPERT_FF), D_MODEL),
        'moe_w3': w(ks[27], (N_MOE, N_EXPERTS, D_MODEL, EXPERT_FF), D_MODEL),
        'moe_w2': w(ks[28], (N_MOE, N_EXPERTS, EXPERT_FF, D_MODEL), EXPERT_FF),
    }


def reference(x, positions, attn_norm_g, w_in, a_q_norm_g, a_k_norm_g, b_cq_norm_g, b_ckv_norm_g,
              b_w_uq, b_w_ukv, b_qn_g, b_qr_g, b_kn_g, b_kr_g, c_q_norm_g, c_k_norm_g, c_sinks,
              w_a_out, w_b_out, w_c_out, w_o, ffn_norm_g, ffn_w1, ffn_w3, ffn_w2,
              router_w, moe_w1, moe_w3, moe_w2):
    b, s, _ = x.shape
    for l in range(DEPTH):
        h = rms_norm(x, attn_norm_g[l])
        proj = h @ w_in[l]
        qa, ka, va, iq, ik, iw, cq, ckv, kr, qc, kc, vc, gates = split_columns(proj)

        qa = rope(rms_norm(qa.reshape(b, s, A_HEADS, HEAD_DIM), a_q_norm_g[l]), positions)
        ka = rope(rms_norm(ka[:, :, None, :], a_k_norm_g[l]), positions)[:, :, 0]
        iq = rope(iq.reshape(b, s, IDX_HEADS, IDX_DIM), positions)
        ik = rope(ik[:, :, None, :], positions)[:, :, 0]
        ya = dsa_attention(qa, ka, va, iq, ik, iw * IDX_HEADS ** -0.5)

        q_b = (rms_norm(cq, b_cq_norm_g[l]) @ b_w_uq[l]).reshape(b, s, B_HEADS, B_NOPE + B_ROPE)
        kv_b = (rms_norm(ckv, b_ckv_norm_g[l]) @ b_w_ukv[l]).reshape(b, s, B_HEADS, B_NOPE + B_V)
        q_nope = rms_norm(q_b[..., :B_NOPE], b_qn_g[l])
        q_rope = rope(rms_norm(q_b[..., B_NOPE:], b_qr_g[l]), positions)
        k_nope = rms_norm(kv_b[..., :B_NOPE], b_kn_g[l])
        v_b = kv_b[..., B_NOPE:]
        k_rope = rope(rms_norm(kr[:, :, None, :], b_kr_g[l]), positions)[:, :, 0]
        yb = mla_attention(q_nope, q_rope, k_nope, k_rope, v_b)

        qc = rope(rms_norm(qc.reshape(b, s, C_HEADS, HEAD_DIM), c_q_norm_g[l]), positions)
        kc = rope(rms_norm(kc.reshape(b, s, C_KV_HEADS, HEAD_DIM), c_k_norm_g[l]), positions)
        vc = vc.reshape(b, s, C_KV_HEADS, HEAD_DIM)
        yc = swa_attention(qc, kc, vc, c_sinks[l])

        ga, gb, gc = jnp.split(jax.nn.sigmoid(gates), 3, axis=-1)
        merged = (ga * (ya.reshape(b, s, -1) @ w_a_out[l])
                  + gb * (yb.reshape(b, s, -1) @ w_b_out[l])
                  + gc * (yc.reshape(b, s, -1) @ w_c_out[l]))
        x = x + merged @ w_o[l]

        h2 = rms_norm(x, ffn_norm_g[l])
        if l % 2 == 0:
            x = x + swiglu(h2, ffn_w1[l // 2], ffn_w3[l // 2], ffn_w2[l // 2])
        else:
            x = x + moe_ffn(h2, router_w[l // 2], moe_w1[l // 2], moe_w3[l // 2], moe_w2[l // 2])
    return x
```

```python
import functools

import numpy as np
import jax
import jax.numpy as jnp
from jax import lax
from jax.experimental import pallas as pl
from jax.experimental.pallas import tpu as pltpu

F32, BF16, I32 = jnp.float32, jnp.bfloat16, jnp.int32

EPS = 1e-6
ROPE_THETA = 10000.0
HEAD_DIM = 64
A_HEADS = 4
IDX_HEADS = 8
IDX_DIM = 32
TOPK_MAX = 256
B_HEADS = 4
B_NOPE = 64
B_ROPE = 32
B_V = 64
C_HEADS = 8
C_KV_HEADS = 2
WINDOW = 128
N_EXPERTS = 8
MOE_BLOCK = 256

LANES = 128
VMEM_LIMIT = 56 * 1024 * 1024
INT_MIN = -2 ** 31
NEG_BIG = -1e30

C_QA, C_KVA, C_IQ, C_IKW, C_CQ, C_CKV, C_KR, C_QC, C_KC, C_VC, C_G = (
    0, 256, 384, 640, 768, 1024, 1152, 1280, 1792, 1920, 2048)


def _params(sem):
    return pltpu.CompilerParams(dimension_semantics=sem, vmem_limit_bytes=VMEM_LIMIT)


def _mm(a, b):
    return jnp.dot(a, b, preferred_element_type=F32)


def _mm_nt(a, b):
    return lax.dot_general(a, b, (((1,), (1,)), ((), ())), preferred_element_type=F32)


def _sigmoid(v):
    return 1.0 / (1.0 + jnp.exp(-v))


def _rope_table_kernel(pos_ref, f_ref, sg_ref, c64_ref, s64_ref, c32_ref, s32_ref):
    pos = pos_ref[...]
    a = pos * f_ref[0:1, :]
    c64_ref[...] = jnp.cos(a)
    s64_ref[...] = jnp.sin(a) * sg_ref[0:1, :]
    a = pos * f_ref[1:2, :]
    c32_ref[...] = jnp.cos(a)
    s32_ref[...] = jnp.sin(a) * sg_ref[1:2, :]


def _rope_tables(pos_f, tm=1024):
    n = pos_f.shape[0]
    lane = np.arange(LANES)
    f32 = ROPE_THETA ** (-jnp.arange(32, dtype=F32) / 32)
    f16 = ROPE_THETA ** (-jnp.arange(16, dtype=F32) / 16)
    freqs = jnp.stack([f32[lane % 32], f16[lane % 16]])
    signs = jnp.asarray(np.stack([np.where(lane % 64 < 32, -1.0, 1.0),
                                  np.where(lane % 32 < 16, -1.0, 1.0)]), F32)
    tab = jax.ShapeDtypeStruct((n, LANES), F32)
    row = pl.BlockSpec((tm, LANES), lambda i: (i, 0))
    par = pl.BlockSpec((2, LANES), lambda i: (0, 0))
    return pl.pallas_call(
        _rope_table_kernel, out_shape=(tab,) * 4, grid=(n // tm,),
        in_specs=[pl.BlockSpec((tm, 1), lambda i: (i, 0)), par, par],
        out_specs=(row,) * 4, compiler_params=_params(("arbitrary",)), name="rope_tables",
    )(pos_f, freqs, signs)


def _swap_half(y, half):
    lane = lax.broadcasted_iota(I32, y.shape, 1)
    return jnp.where((lane & half) == 0, pltpu.roll(y, LANES - half, 1), pltpu.roll(y, half, 1))


def _rope(y, cos, sin_signed, half):
    return y * cos + _swap_half(y, half) * sin_signed


def _seg_mean_sq(y, mseg):
    sq = y * y
    hi = sq.astype(BF16)
    lo = (sq - hi.astype(F32)).astype(BF16)
    return _mm(hi, mseg) + _mm(lo, mseg)


def _seg_norm(y, mseg, gain):
    return y * lax.rsqrt(_seg_mean_sq(y, mseg) + EPS) * gain


def _slab(s):
    return slice(s * LANES, (s + 1) * LANES)


def _proj_kernel(x_ref, g_ref, w_ref, wuq_ref, wukv_ref, gcq_ref, grows_ref, mseg_ref,
                 c64_ref, s64_ref, c32_ref, s32_ref,
                 qa_ref, kva_ref, iq_ref, ik_ref, iw_ref, qb_ref, kb_ref, vb_ref,
                 qc_ref, kc_ref, vc_ref, gt_ref):
    x = x_ref[...]
    h = (x * lax.rsqrt(jnp.mean(x * x, axis=-1, keepdims=True) + EPS) * g_ref[...]).astype(BF16)
    c64, s64, c32, s32 = c64_ref[...], s64_ref[...], c32_ref[...], s32_ref[...]
    m64, mqb = mseg_ref[0], mseg_ref[1]
    lane = lax.broadcasted_iota(I32, c64.shape, 1)
    in_rope = (lane >= B_NOPE) & (lane < B_NOPE + B_ROPE)
    cb = jnp.where(in_rope, c32, 1.0)
    sb = jnp.where(in_rope, s32, 0.0)

    p = _mm(h, w_ref[:, C_QA:C_CQ])
    for s in range(2):
        y = _rope(_seg_norm(p[:, _slab(s)], m64, grows_ref[0:1, :]), c64, s64, 32)
        qa_ref[:, _slab(s)] = (y * HEAD_DIM ** -0.5).astype(BF16)
    y = p[:, _slab(2)]
    yr = _rope(_seg_norm(y, m64, grows_ref[1:2, :]), c64, s64, 32)
    kva_ref[...] = jnp.where(lane < HEAD_DIM, yr, y).astype(BF16)
    for s in range(2):
        iq_ref[:, _slab(s)] = _rope(p[:, _slab(3 + s)], c32, s32, 16).astype(BF16)
    y = p[:, _slab(5)]
    ik_ref[...] = jnp.where(lane < IDX_DIM, _rope(y, c32, s32, 16), 0.0).astype(BF16)
    iw_ref[...] = y * (IDX_HEADS * IDX_DIM) ** -0.5

    p = _mm(h, w_ref[:, C_CQ:C_QC])
    cq = p[:, 0:256]
    cqn = cq * lax.rsqrt(jnp.mean(cq * cq, axis=-1, keepdims=True) + EPS) * gcq_ref[...]
    qb = _mm(cqn.astype(BF16), wuq_ref[...])
    for s in range(B_HEADS):
        y = _rope(_seg_norm(qb[:, _slab(s)], mqb, grows_ref[2:3, :]), cb, sb, 16)
        qb_ref[:, _slab(s)] = (y * (B_NOPE + B_ROPE) ** -0.5).astype(BF16)
    ckv = p[:, 256:384]
    ckvn = ckv * lax.rsqrt(jnp.mean(ckv * ckv, axis=-1, keepdims=True) + EPS) * grows_ref[7:8, :]
    kvb = _mm(ckvn.astype(BF16), wukv_ref[...])
    krs = p[:, 384:512]
    kr = krs * lax.rsqrt(jnp.sum(krs * krs, axis=-1, keepdims=True) * (1.0 / B_ROPE) + EPS) * grows_ref[4:5, :]
    kr = _rope(kr, cb, sb, 16)
    for s in range(B_HEADS):
        kb_ref[:, _slab(s)] = (_seg_norm(kvb[:, _slab(s)], m64, grows_ref[3:4, :]) + kr).astype(BF16)
    vb_ref[...] = kvb[:, 512:768].astype(BF16)

    p = _mm(h, w_ref[:, C_QC:C_G])
    for s in range(4):
        y = _rope(_seg_norm(p[:, _slab(s)], m64, grows_ref[5:6, :]), c64, s64, 32)
        qc_ref[:, _slab(s)] = (y * HEAD_DIM ** -0.5).astype(BF16)
    y = _rope(_seg_norm(p[:, _slab(4)], m64, grows_ref[6:7, :]), c64, s64, 32)
    kc_ref[...] = y.astype(BF16)
    vc_ref[...] = p[:, _slab(5)].astype(BF16)

    for c in range(3):
        lo = C_G + c * 1024
        gt_ref[:, c * 1024:(c + 1) * 1024] = _sigmoid(_mm(h, w_ref[:, lo:lo + 1024]))


def _proj(xf, g, w_r, wuq, wukv, gcq, grows, mseg, tabs, tm=256):
    n, d = xf.shape
    row = lambda w: pl.BlockSpec((tm, w), lambda i: (i, 0))
    full = lambda a: pl.BlockSpec(a.shape, lambda i: (0,) * a.ndim)
    widths = [(256, BF16), (128, BF16), (256, BF16), (128, BF16), (128, F32), (512, BF16),
              (512, BF16), (256, BF16), (512, BF16), (128, BF16), (128, BF16), (3072, F32)]
    return pl.pallas_call(
        _proj_kernel,
        out_shape=tuple(jax.ShapeDtypeStruct((n, w), dt) for w, dt in widths),
        grid=(n // tm,),
        in_specs=[row(d), full(g), full(w_r), full(wuq), full(wukv), full(gcq), full(grows), full(mseg)]
        + [row(LANES)] * 4,
        out_specs=tuple(row(w) for w, _ in widths),
        compiler_params=_params(("arbitrary",)), name="in_proj",
    )(xf, g, w_r, wuq, wukv, gcq, grows, mseg, *tabs)


def _dsa_kernel(iq_ref, iw_ref, qa_ref, ik_ref, kv_ref, o_ref, keys_ref, *, tq, ck, nsel, seq):
    i = pl.program_id(1)
    nk = (i * tq + tq + ck - 1) // ck
    iq = iq_ref[...]
    iw = iw_ref[...]
    iq_h = [iq[:, h * IDX_DIM:(h + 1) * IDX_DIM] for h in range(IDX_HEADS)]
    iw_b = [jnp.broadcast_to(iw[:, IDX_DIM + h:IDX_DIM + h + 1], (tq, ck)) for h in range(IDX_HEADS)]
    qpos = lax.broadcasted_iota(I32, (tq, ck), 0) + i * tq
    col = lax.broadcasted_iota(I32, (tq, ck), 1)

    def score_body(c, carry):
        ikc = ik_ref[pl.ds(pl.multiple_of(c * ck, ck), ck), :][:, 0:IDX_DIM]
        acc = jnp.zeros((tq, ck), F32)
        for h in range(IDX_HEADS):
            acc = acc + jnp.maximum(_mm_nt(iq_h[h], ikc), 0.0) * iw_b[h]
        bits = lax.bitcast_convert_type(acc, I32)
        key = bits ^ ((bits >> 31) & 0x7FFFFFFF)
        keys_ref[c] = jnp.where(col + c * ck <= qpos, key, INT_MIN)
        return carry

    lax.fori_loop(0, nk, score_body, 0)

    def count(pred):
        def body(c, acc):
            return acc + jnp.where(pred(keys_ref[c], col + c * ck), 1.0, 0.0)
        acc = lax.fori_loop(0, nk, body, jnp.zeros((tq, ck), F32))
        return jnp.sum(acc, axis=1, keepdims=True)

    def bis_body(it, thr):
        cand = thr + jnp.left_shift(jnp.int32(1), 31 - it)
        return jnp.where(count(lambda k, kp: k >= cand) >= nsel, cand, thr)

    thr = lax.fori_loop(0, 32, bis_body, jnp.full((tq, 1), INT_MIN, I32))

    need = nsel - count(lambda k, kp: k > thr)
    excess = (count(lambda k, kp: k == thr) > need) & (thr != INT_MIN)
    last0 = jnp.where(thr == INT_MIN, -1, seq).astype(I32)
    nbits = seq.bit_length() - 1

    def tie_fix(_):
        def body(it, p):
            cand = p + jnp.left_shift(jnp.int32(1), nbits - 1 - it)
            return jnp.where(count(lambda k, kp: (k == thr) & (kp < cand)) < need, cand, p)
        p = lax.fori_loop(0, nbits, body, jnp.zeros((tq, 1), I32))
        return jnp.where(excess, p, last0)

    last = lax.cond(jnp.max(jnp.where(excess, 1.0, 0.0)) > 0.0, tie_fix, lambda _: last0, 0)

    qa = qa_ref[...]
    q4 = jnp.concatenate([qa[:, h * HEAD_DIM:(h + 1) * HEAD_DIM] for h in range(A_HEADS)], axis=0)

    def att_body(c, carry):
        m, l, acc = carry
        kvc = kv_ref[pl.ds(pl.multiple_of(c * ck, ck), ck), :]
        k, v = kvc[:, 0:HEAD_DIM], kvc[:, HEAD_DIM:2 * HEAD_DIM]
        key = keys_ref[c]
        sel = (key > thr) | ((key == thr) & (col + c * ck <= last))
        bias = jnp.where(sel, 0.0, NEG_BIG)
        s = _mm_nt(q4, k) + jnp.concatenate([bias] * A_HEADS, axis=0)
        m_new = jnp.maximum(m, jnp.max(s, axis=1, keepdims=True))
        alpha = jnp.exp(m - m_new)
        p = jnp.exp(s - m_new)
        l = alpha * l + jnp.sum(p, axis=1, keepdims=True)
        acc = alpha * acc + _mm(p.astype(BF16), v)
        return m_new, l, acc

    rows = A_HEADS * tq
    m, l, acc = lax.fori_loop(0, nk, att_body, (jnp.full((rows, 1), NEG_BIG, F32),
                                                jnp.zeros((rows, 1), F32), jnp.zeros((rows, HEAD_DIM), F32)))
    o = acc / l
    o_ref[...] = jnp.concatenate([o[h * tq:(h + 1) * tq] for h in range(A_HEADS)], axis=1).astype(BF16)


def _dsa(iq, iw, qa, ik, kva, b, s, tq=128, ck=256):
    n = b * s
    nq = s // tq
    nsel = min(TOPK_MAX, s // 4)
    qrow = lambda w: pl.BlockSpec((tq, w), lambda bi, i: (bi * nq + i, 0))
    seq_blk = lambda w: pl.BlockSpec((s, w), lambda bi, i: (bi, 0))
    return pl.pallas_call(
        functools.partial(_dsa_kernel, tq=tq, ck=ck, nsel=nsel, seq=s),
        out_shape=jax.ShapeDtypeStruct((n, 256), BF16), grid=(b, nq),
        in_specs=[qrow(256), qrow(LANES), qrow(256), seq_blk(LANES), seq_blk(LANES)],
        out_specs=qrow(256),
        scratch_shapes=[pltpu.VMEM((s // ck, tq, ck), I32)],
        compiler_params=_params(("arbitrary", "arbitrary")), name="dsa_attention",
    )(iq, iw, qa, ik, kva)


def _flash_step(q, k, v, carry, bias=None):
    m, l, acc = carry
    s = _mm_nt(q, k)
    if bias is not None:
        s = s + bias
    m_new = jnp.maximum(m, jnp.max(s, axis=1, keepdims=True))
    alpha = jnp.exp(m - m_new)
    p = jnp.exp(s - m_new)
    l = alpha * l + jnp.sum(p, axis=1, keepdims=True)
    acc = alpha * acc + _mm(p.astype(BF16), v)
    return m_new, l, acc


def _mla_kernel(q_ref, k_ref, v_ref, o_ref, *, tq):
    i = pl.program_id(1)
    q = q_ref[...]
    r = lax.broadcasted_iota(I32, (tq, tq), 0)
    c = lax.broadcasted_iota(I32, (tq, tq), 1)
    diag_bias = jnp.where(c <= r, 0.0, NEG_BIG)
    outs = []
    for h in range(B_HEADS):
        qh = q[:, _slab(h)]

        def kv(cidx, h=h):
            rows = pl.ds(pl.multiple_of(cidx * tq, tq), tq)
            return k_ref[rows, _slab(h)], v_ref[rows, h * B_V:(h + 1) * B_V]

        def body(cidx, carry, qh=qh, kv=kv):
            k, v = kv(cidx)
            return _flash_step(qh, k, v, carry)

        init = (jnp.full((tq, 1), NEG_BIG, F32), jnp.zeros((tq, 1), F32), jnp.zeros((tq, B_V), F32))
        carry = lax.fori_loop(0, i, body, init)
        k, v = kv(i)
        m, l, acc = _flash_step(qh, k, v, carry, diag_bias)
        outs.append(acc / l)
    o_ref[...] = jnp.concatenate(outs, axis=1).astype(BF16)


def _mla(qb, kb, vb, b, s, tq=256):
    n = b * s
    nq = s // tq
    qrow = lambda w: pl.BlockSpec((tq, w), lambda bi, i: (bi * nq + i, 0))
    seq_blk = lambda w: pl.BlockSpec((s, w), lambda bi, i: (bi, 0))
    return pl.pallas_call(
        functools.partial(_mla_kernel, tq=tq),
        out_shape=jax.ShapeDtypeStruct((n, B_HEADS * B_V), BF16), grid=(b, nq),
        in_specs=[qrow(512), seq_blk(512), seq_blk(256)], out_specs=qrow(256),
        compiler_params=_params(("arbitrary", "arbitrary")), name="mla_attention",
    )(qb, kb, vb)


def _swa_kernel(sink_ref, q_ref, kp_ref, kc_ref, vp_ref, vc_ref, o_ref, *, tq):
    i = pl.program_id(1)
    q = q_ref[...]
    keys = jnp.concatenate([kp_ref[...], kc_ref[...]], axis=0)
    vals = jnp.concatenate([vp_ref[...], vc_ref[...]], axis=0)
    nkeys = WINDOW + tq
    qpos = lax.broadcasted_iota(I32, (tq, nkeys), 0) + i * tq
    kpos = lax.broadcasted_iota(I32, (tq, nkeys), 1) + i * tq - WINDOW
    visible = (kpos > qpos - WINDOW) & (kpos <= qpos) & (kpos >= 0)
    bias = jnp.where(visible, 0.0, NEG_BIG)
    group = C_HEADS // C_KV_HEADS
    outs = []
    for h in range(C_HEADS):
        kvh = h // group
        k = keys[:, kvh * HEAD_DIM:(kvh + 1) * HEAD_DIM]
        v = vals[:, kvh * HEAD_DIM:(kvh + 1) * HEAD_DIM]
        s = _mm_nt(q[:, h * HEAD_DIM:(h + 1) * HEAD_DIM], k) + bias
        sink = sink_ref[h]
        m = jnp.maximum(jnp.max(s, axis=1, keepdims=True), sink)
        p = jnp.exp(s - m)
        denom = jnp.sum(p, axis=1, keepdims=True) + jnp.exp(sink - m)
        outs.append(_mm((p / denom).astype(BF16), v))
    o_ref[...] = jnp.concatenate(outs, axis=1).astype(BF16)


def _swa(sinks, qc, kc, vc, b, s, tq=256):
    n = b * s
    nq = s // tq
    per = tq // WINDOW
    qrow = lambda w: pl.BlockSpec((tq, w), lambda bi, i: (bi * nq + i, 0))
    prev = pl.BlockSpec((WINDOW, LANES), lambda bi, i: (jnp.maximum((bi * nq + i) * per - 1, 0), 0))
    return pl.pallas_call(
        functools.partial(_swa_kernel, tq=tq),
        out_shape=jax.ShapeDtypeStruct((n, C_HEADS * HEAD_DIM), BF16), grid=(b, nq),
        in_specs=[pl.BlockSpec(memory_space=pltpu.SMEM), qrow(512), prev, qrow(LANES), prev, qrow(LANES)],
        out_specs=qrow(512),
        compiler_params=_params(("arbitrary", "arbitrary")), name="swa_attention",
    )(sinks, qc, kc, kc, vc, vc)


def _merge_kernel(ya_ref, yb_ref, yc_ref, gt_ref, x_ref, wa_ref, wb_ref, wc_ref, wo_ref, g_ref,
                  xo_ref, h_ref):
    d = x_ref.shape[1]
    merged = (gt_ref[:, 0:d] * _mm(ya_ref[...], wa_ref[...])
              + gt_ref[:, d:2 * d] * _mm(yb_ref[...], wb_ref[...])
              + gt_ref[:, 2 * d:3 * d] * _mm(yc_ref[...], wc_ref[...]))
    xn = x_ref[...] + _mm(merged.astype(BF16), wo_ref[...])
    xo_ref[...] = xn
    h = xn * lax.rsqrt(jnp.mean(xn * xn, axis=-1, keepdims=True) + EPS) * g_ref[...]
    h_ref[...] = h.astype(BF16)


def _merge(ya, yb, yc, gt, xf, wa, wb, wc, wo, g, tm=256):
    n, d = xf.shape
    row = lambda w: pl.BlockSpec((tm, w), lambda i: (i, 0))
    full = lambda a: pl.BlockSpec(a.shape, lambda i: (0,) * a.ndim)
    return pl.pallas_call(
        _merge_kernel,
        out_shape=(jax.ShapeDtypeStruct((n, d), F32), jax.ShapeDtypeStruct((n, d), BF16)),
        grid=(n // tm,),
        in_specs=[row(256), row(256), row(512), row(3 * d), row(d), full(wa), full(wb), full(wc), full(wo),
                  full(g)],
        out_specs=(row(d), row(d)),
        compiler_params=_params(("arbitrary",)), name="merge_out_proj",
    )(ya, yb, yc, gt, xf, wa, wb, wc, wo, g)


def _ffn_kernel(h_ref, x_ref, w1_hbm, w3_hbm, w2_hbm, o_ref, w1_ref, w3_ref, w2_ref, sem, *, chunk):
    @pl.when(pl.program_id(0) == 0)
    def _():
        copies = [pltpu.make_async_copy(src, dst, sem.at[j]) for j, (src, dst) in
                  enumerate(((w1_hbm, w1_ref), (w3_hbm, w3_ref), (w2_hbm, w2_ref)))]
        for cp in copies:
            cp.start()
        for cp in copies:
            cp.wait()

    h = h_ref[...]
    acc = x_ref[...]
    for j in range(w1_ref.shape[1] // chunk):
        cols = slice(j * chunk, (j + 1) * chunk)
        a = _mm(h, w1_ref[:, cols])
        mid = a * _sigmoid(a) * _mm(h, w3_ref[:, cols])
        acc = acc + _mm(mid.astype(BF16), w2_ref[cols, :])
    o_ref[...] = acc


def _ffn(h2, xf, w1, w3, w2, tm=256, chunk=1408):
    n, d = xf.shape
    row = lambda w: pl.BlockSpec((tm, w), lambda i: (i, 0))
    hbm = pl.BlockSpec(memory_space=pl.ANY)
    return pl.pallas_call(
        functools.partial(_ffn_kernel, chunk=chunk),
        out_shape=jax.ShapeDtypeStruct((n, d), F32), grid=(n // tm,),
        in_specs=[row(d), row(d), hbm, hbm, hbm], out_specs=row(d),
        scratch_shapes=[pltpu.VMEM(w1.shape, BF16), pltpu.VMEM(w3.shape, BF16), pltpu.VMEM(w2.shape, BF16),
                        pltpu.SemaphoreType.DMA((3,))],
        compiler_params=_params(("arbitrary",)), name="dense_swiglu",
    )(h2, xf, w1, w3, w2)


def _route_kernel(x_ref, g_ref, wr_ref, tri_ref, route_ref, cnt_ref, carry_ref):
    @pl.when(pl.program_id(0) == 0)
    def _():
        carry_ref[...] = jnp.zeros_like(carry_ref)

    x = x_ref[...]
    h = x * lax.rsqrt(jnp.mean(x * x, axis=-1, keepdims=True) + EPS) * g_ref[...]
    logits = jnp.dot(h, wr_ref[...], precision=lax.Precision.HIGHEST, preferred_element_type=F32)
    lane = lax.broadcasted_iota(I32, logits.shape, 1).astype(F32)
    lg = jnp.where(lane < N_EXPERTS, logits, -jnp.inf)
    m1 = jnp.max(lg, axis=1, keepdims=True)
    e1 = jnp.min(jnp.where(lg == m1, lane, float(LANES)), axis=1, keepdims=True)
    lg2 = jnp.where(lane == e1, -jnp.inf, lg)
    m2 = jnp.max(lg2, axis=1, keepdims=True)
    e2 = jnp.min(jnp.where(lg2 == m2, lane, float(LANES)), axis=1, keepdims=True)
    ex = jnp.exp(m2 - m1)
    g1 = 1.0 / (1.0 + ex)
    g2 = ex / (1.0 + ex)
    onehot = jnp.where((lane == e1) | (lane == e2), 1.0, 0.0)
    before = _mm(tri_ref[...], onehot.astype(BF16)) + carry_ref[0:1, :]
    r1 = jnp.sum(jnp.where(lane == e1, before, 0.0), axis=1, keepdims=True)
    r2 = jnp.sum(jnp.where(lane == e2, before, 0.0), axis=1, keepdims=True)
    out = jnp.zeros_like(logits)
    for idx, val in enumerate((e1, e2, g1, g2, r1, r2)):
        out = jnp.where(lane == idx, val, out)
    route_ref[...] = out
    total = carry_ref[0:1, :] + jnp.sum(onehot, axis=0, keepdims=True)
    carry_ref[...] = jnp.broadcast_to(total, carry_ref.shape)
    cnt_ref[...] = jnp.broadcast_to(total, cnt_ref.shape)


def _route(xf, g, wr, tm=256):
    n, d = xf.shape
    tri = jnp.asarray(np.tril(np.ones((tm, tm), np.float32), -1), BF16)
    full = lambda a: pl.BlockSpec(a.shape, lambda i: (0,) * a.ndim)
    return pl.pallas_call(
        _route_kernel,
        out_shape=(jax.ShapeDtypeStruct((n, LANES), F32), jax.ShapeDtypeStruct((8, LANES), F32)),
        grid=(n // tm,),
        in_specs=[pl.BlockSpec((tm, d), lambda i: (i, 0)), full(g), full(wr), full(tri)],
        out_specs=(pl.BlockSpec((tm, LANES), lambda i: (i, 0)), pl.BlockSpec((8, LANES), lambda i: (0, 0))),
        scratch_shapes=[pltpu.VMEM((8, LANES), F32)],
        compiler_params=_params(("arbitrary",)), name="moe_route",
    )(xf, g, wr, tri)


def _scatter_kernel(dest_ref, h_hbm, xin_hbm, xout_hbm, sem, *, tt):
    del xin_hbm
    base = pl.program_id(0) * tt

    def copy(r, k):
        d = dest_ref[0, 0, 2 * r + k]
        return pltpu.make_async_copy(h_hbm.at[pl.ds(base + r, 1), :], xout_hbm.at[pl.ds(d, 1), :], sem)

    def start(r, carry):
        copy(r, 0).start()
        copy(r, 1).start()
        return carry

    def wait(r, carry):
        copy(r, 0).wait()
        copy(r, 1).wait()
        return carry

    lax.fori_loop(0, tt, start, 0)
    lax.fori_loop(0, tt, wait, 0)


def _scatter(dest3, h_u32, rows, tt):
    n, w = h_u32.shape
    hbm = pl.BlockSpec(memory_space=pl.ANY)
    return pl.pallas_call(
        functools.partial(_scatter_kernel, tt=tt),
        out_shape=jax.ShapeDtypeStruct((rows, w), jnp.uint32), grid=(n // tt,),
        in_specs=[pl.BlockSpec((1, 1, 2 * tt), lambda i: (i, 0, 0), memory_space=pltpu.SMEM), hbm, hbm],
        out_specs=hbm, scratch_shapes=[pltpu.SemaphoreType.DMA(())],
        input_output_aliases={2: 0},
        compiler_params=_params(("arbitrary",)), name="moe_scatter",
    )(dest3, h_u32, jnp.zeros((rows, w), jnp.uint32))


def _expert_kernel(be_ref, nv_ref, x_ref, w1_hbm, w3_hbm, w2_hbm, y_ref, w1_ref, w3_ref, w2_ref, sem, *, chunk):
    blk = pl.program_id(0)
    e = be_ref[blk]

    @pl.when((blk == 0) | (e != be_ref[jnp.maximum(blk - 1, 0)]))
    def _():
        copies = [pltpu.make_async_copy(src.at[e], dst, sem.at[j]) for j, (src, dst) in
                  enumerate(((w1_hbm, w1_ref), (w3_hbm, w3_ref), (w2_hbm, w2_ref)))]
        for cp in copies:
            cp.start()
        for cp in copies:
            cp.wait()

    @pl.when(blk < nv_ref[0])
    def _():
        x = x_ref[...]
        acc = jnp.zeros(y_ref.shape, F32)
        for j in range(w1_ref.shape[1] // chunk):
            cols = slice(j * chunk, (j + 1) * chunk)
            a = _mm(x, w1_ref[:, cols])
            mid = a * _sigmoid(a) * _mm(x, w3_ref[:, cols])
            acc = acc + _mm(mid.astype(BF16), w2_ref[cols, :])
        y_ref[...] = acc

    @pl.when(blk >= nv_ref[0])
    def _():
        y_ref[...] = jnp.zeros(y_ref.shape, F32)


def _experts(block_e, nvalid, xbuf, w1, w3, w2, chunk=512):
    rows, d = xbuf.shape
    hbm = pl.BlockSpec(memory_space=pl.ANY)
    blk = pl.BlockSpec((MOE_BLOCK, d), lambda i, be, nv: (i, 0))
    return pl.pallas_call(
        functools.partial(_expert_kernel, chunk=chunk),
        out_shape=jax.ShapeDtypeStruct((rows, d), F32),
        grid_spec=pltpu.PrefetchScalarGridSpec(
            num_scalar_prefetch=2, grid=(rows // MOE_BLOCK,),
            in_specs=[blk, hbm, hbm, hbm], out_specs=blk,
            scratch_shapes=[pltpu.VMEM(w1.shape[1:], BF16), pltpu.VMEM(w3.shape[1:], BF16),
                            pltpu.VMEM(w2.shape[1:], BF16), pltpu.SemaphoreType.DMA((3,))]),
        compiler_params=_params(("arbitrary",)), name="moe_experts",
    )(block_e, nvalid, xbuf, w1, w3, w2)


def _combine_kernel(dest_ref, y_hbm, x_ref, route_ref, o_ref, buf_ref, sem, *, tt):
    def copy(r, k):
        d = dest_ref[0, 0, 2 * r + k]
        return pltpu.make_async_copy(y_hbm.at[pl.ds(d, 1), :], buf_ref.at[k, pl.ds(r, 1), :], sem)

    def start(r, carry):
        copy(r, 0).start()
        copy(r, 1).start()
        return carry

    def wait(r, carry):
        copy(r, 0).wait()
        copy(r, 1).wait()
        return carry

    lax.fori_loop(0, tt, start, 0)
    lax.fori_loop(0, tt, wait, 0)
    rt = route_ref[...]
    o_ref[...] = x_ref[...] + (buf_ref[0] * rt[:, 2:3] + buf_ref[1] * rt[:, 3:4])


def _combine(dest3, ybuf, xf, route, tt):
    n, d = xf.shape
    row = lambda w: pl.BlockSpec((tt, w), lambda i: (i, 0))
    return pl.pallas_call(
        functools.partial(_combine_kernel, tt=tt),
        out_shape=jax.ShapeDtypeStruct((n, d), F32), grid=(n // tt,),
        in_specs=[pl.BlockSpec((1, 1, 2 * tt), lambda i: (i, 0, 0), memory_space=pltpu.SMEM),
                  pl.BlockSpec(memory_space=pl.ANY), row(d), row(LANES)],
        out_specs=row(d),
        scratch_shapes=[pltpu.VMEM((2, tt, d), F32), pltpu.SemaphoreType.DMA(())],
        compiler_params=_params(("arbitrary",)), name="moe_combine",
    )(dest3, ybuf, xf, route)


def _moe(xf, h2, g, w_router, w1, w3, w2, tt=256):
    n, d = xf.shape
    wr = jnp.pad(w_router, ((0, 0), (0, LANES - N_EXPERTS)))
    route, cnt = _route(xf, g, wr)
    counts = cnt[0, :N_EXPERTS].astype(I32)
    padded = (counts + MOE_BLOCK - 1) // MOE_BLOCK * MOE_BLOCK
    pend = jnp.cumsum(padded)
    pstart = pend - padded
    dest = pstart[route[:, 0:2].astype(I32)] + route[:, 4:6].astype(I32)
    dest3 = dest.reshape(n // tt, 1, 2 * tt)
    n_blocks = n * 2 // MOE_BLOCK + N_EXPERTS
    rows = n_blocks * MOE_BLOCK
    block_e = jnp.minimum(jnp.searchsorted(pend, jnp.arange(n_blocks) * MOE_BLOCK, side='right'),
                          N_EXPERTS - 1).astype(I32)
    nvalid = (pend[-1:] // MOE_BLOCK).astype(I32)
    h_u32 = lax.bitcast_convert_type(h2.reshape(n, d // 2, 2), jnp.uint32)
    xbuf = _scatter(dest3, h_u32, rows, tt)
    xbuf = lax.bitcast_convert_type(xbuf, BF16).reshape(rows, d)
    ybuf = _experts(block_e, nvalid, xbuf, w1, w3, w2)
    return _combine(dest3, ybuf, xf, route, tt)


def _tile_row(v, width=LANES):
    v = v.astype(F32)
    return jnp.tile(v, width // v.shape[0])


def _layer_params(l, attn_norm_g, w_in, a_q_norm_g, a_k_norm_g, b_cq_norm_g, b_ckv_norm_g, b_w_uq, b_w_ukv,
                  b_qn_g, b_qr_g, b_kn_g, b_kr_g, c_q_norm_g, c_k_norm_g):
    d = w_in.shape[1]
    sizes = (A_HEADS * HEAD_DIM, HEAD_DIM, HEAD_DIM, IDX_HEADS * IDX_DIM, IDX_DIM, IDX_HEADS,
             b_w_uq.shape[1], b_w_ukv.shape[1], B_ROPE, C_HEADS * HEAD_DIM, C_KV_HEADS * HEAD_DIM,
             C_KV_HEADS * HEAD_DIM, 3 * d)
    qa, ka, va, iq, ik, iw, cq, ckv, kr, qc, kc, vc, gates = jnp.split(w_in[l], np.cumsum(sizes)[:-1].tolist(), axis=1)
    z = lambda k: jnp.zeros((d, k), F32)
    w_r = jnp.concatenate([qa, ka, va, iq, ik, iw, z(LANES - IDX_DIM - IDX_HEADS), cq, ckv,
                           z(B_NOPE), kr, z(LANES - B_NOPE - B_ROPE), qc, kc, vc, gates], axis=1).astype(BF16)
    uq = b_w_uq[l].reshape(-1, B_HEADS, B_NOPE + B_ROPE)
    uq = jnp.pad(uq, ((0, 0), (0, 0), (0, LANES - B_NOPE - B_ROPE))).reshape(-1, B_HEADS * LANES).astype(BF16)
    ukv = b_w_ukv[l].reshape(-1, B_HEADS, B_NOPE + B_V)
    uk = jnp.pad(ukv[:, :, :B_NOPE], ((0, 0), (0, 0), (0, LANES - B_NOPE))).reshape(-1, B_HEADS * LANES)
    uv = ukv[:, :, B_NOPE:].reshape(-1, B_HEADS * B_V)
    wukv = jnp.concatenate([uk, uv], axis=1).astype(BF16)
    zeros = lambda k: jnp.zeros((k,), F32)
    grows = jnp.stack([
        _tile_row(a_q_norm_g[l]),
        jnp.concatenate([a_k_norm_g[l], jnp.ones((HEAD_DIM,), F32)]),
        jnp.concatenate([b_qn_g[l], b_qr_g[l], zeros(LANES - B_NOPE - B_ROPE)]),
        jnp.concatenate([b_kn_g[l], zeros(LANES - B_NOPE)]),
        jnp.concatenate([zeros(B_NOPE), b_kr_g[l], zeros(LANES - B_NOPE - B_ROPE)]),
        _tile_row(c_q_norm_g[l]),
        _tile_row(c_k_norm_g[l]),
        b_ckv_norm_g[l],
    ]).astype(F32)
    return (attn_norm_g[l][None, :], w_r, uq, wukv, b_cq_norm_g[l][None, :], grows)


def _seg_matrices():
    lane = np.arange(LANES)
    g64 = lane // 64
    m64 = (g64[:, None] == g64[None, :]) / 64.0
    gqb = np.where(lane < B_NOPE, 0, np.where(lane < B_NOPE + B_ROPE, 1, 2))
    size = np.where(lane < B_NOPE, B_NOPE, B_ROPE)
    mqb = (gqb[:, None] == gqb[None, :]) / size[None, :]
    return jnp.asarray(np.stack([m64, mqb]), BF16)


def kernel(x, positions, attn_norm_g, w_in, a_q_norm_g, a_k_norm_g, b_cq_norm_g, b_ckv_norm_g, b_w_uq, b_w_ukv,
           b_qn_g, b_qr_g, b_kn_g, b_kr_g, c_q_norm_g, c_k_norm_g, c_sinks, w_a_out, w_b_out, w_c_out, w_o,
           ffn_norm_g, ffn_w1, ffn_w3, ffn_w2, router_w, moe_w1, moe_w3, moe_w2):
    b, s, d = x.shape
    n = b * s
    depth = w_in.shape[0]
    xf = x.reshape(n, d)
    tabs = _rope_tables(positions.reshape(n, 1).astype(F32))
    mseg = _seg_matrices()
    for l in range(depth):
        g, w_r, uq, wukv, gcq, grows = _layer_params(
            l, attn_norm_g, w_in, a_q_norm_g, a_k_norm_g, b_cq_norm_g, b_ckv_norm_g, b_w_uq, b_w_ukv,
            b_qn_g, b_qr_g, b_kn_g, b_kr_g, c_q_norm_g, c_k_norm_g)
        qa, kva, iq, ik, iw, qb, kb, vb, qc, kc, vc, gt = _proj(xf, g, w_r, uq, wukv, gcq, grows, mseg, tabs)
        ya = _dsa(iq, iw, qa, ik, kva, b, s)
        yb = _mla(qb, kb, vb, b, s)
        yc = _swa(c_sinks[l], qc, kc, vc, b, s)
        xf, h2 = _merge(ya, yb, yc, gt, xf, w_a_out[l].astype(BF16), w_b_out[l].astype(BF16),
                        w_c_out[l].astype(BF16), w_o[l].astype(BF16), ffn_norm_g[l][None, :])
        if l % 2 == 0:
            xf = _ffn(h2, xf, ffn_w1[l // 2].astype(BF16), ffn_w3[l // 2].astype(BF16), ffn_w2[l // 2].astype(BF16))
        else:
            xf = _moe(xf, h2, ffn_norm_g[l][None, :], router_w[l // 2], moe_w1[l // 2].astype(BF16),
                      moe_w3[l // 2].astype(BF16), moe_w2[l // 2].astype(BF16))
    return xf.reshape(b, s, d)
```

```python
import functools

import numpy as np
import jax
import jax.numpy as jnp
from jax import lax
from jax.experimental import pallas as pl
from jax.experimental.pallas import tpu as pltpu

F32, BF16, I32 = jnp.float32, jnp.bfloat16, jnp.int32

EPS = 1e-6
ROPE_THETA = 10000.0
HEAD_DIM = 64
A_HEADS = 4
IDX_HEADS = 8
IDX_DIM = 32
TOPK_MAX = 256
B_HEADS = 4
B_NOPE = 64
B_ROPE = 32
B_V = 64
C_HEADS = 8
C_KV_HEADS = 2
WINDOW = 128
N_EXPERTS = 8
MOE_BLOCK = 256

LANES = 128
IDX_CHUNK = 256
VMEM_LIMIT = 56 * 1024 * 1024
INT_MIN = -2 ** 31
NEG_BIG = -1e30

C_QA, C_KVA, C_IQ, C_IKW, C_CQ, C_CKV, C_KR, C_QC, C_KC, C_VC, C_G = (
    0, 256, 384, 640, 768, 1024, 1152, 1280, 1792, 1920, 2048)


def _params(sem):
    return pltpu.CompilerParams(dimension_semantics=sem, vmem_limit_bytes=VMEM_LIMIT)


def _mm(a, b):
    return jnp.dot(a, b, preferred_element_type=F32)


def _mm_nt(a, b):
    return lax.dot_general(a, b, (((1,), (1,)), ((), ())), preferred_element_type=F32)


def _sigmoid(v):
    return 1.0 / (1.0 + jnp.exp(-v))


def _rope_table_kernel(pos_ref, f_ref, sg_ref, c64_ref, s64_ref, c32_ref, s32_ref):
    pos = pos_ref[...]
    a = pos * f_ref[0:1, :]
    c64_ref[...] = jnp.cos(a)
    s64_ref[...] = jnp.sin(a) * sg_ref[0:1, :]
    a = pos * f_ref[1:2, :]
    c32_ref[...] = jnp.cos(a)
    s32_ref[...] = jnp.sin(a) * sg_ref[1:2, :]


def _rope_tables(pos_f, tm=1024):
    n = pos_f.shape[0]
    lane = np.arange(LANES)
    f32 = ROPE_THETA ** (-jnp.arange(32, dtype=F32) / 32)
    f16 = ROPE_THETA ** (-jnp.arange(16, dtype=F32) / 16)
    freqs = jnp.stack([f32[lane % 32], f16[lane % 16]])
    signs = jnp.asarray(np.stack([np.where(lane % 64 < 32, -1.0, 1.0),
                                  np.where(lane % 32 < 16, -1.0, 1.0)]), F32)
    tab = jax.ShapeDtypeStruct((n, LANES), F32)
    row = pl.BlockSpec((tm, LANES), lambda i: (i, 0))
    par = pl.BlockSpec((2, LANES), lambda i: (0, 0))
    return pl.pallas_call(
        _rope_table_kernel, out_shape=(tab,) * 4, grid=(n // tm,),
        in_specs=[pl.BlockSpec((tm, 1), lambda i: (i, 0)), par, par],
        out_specs=(row,) * 4, compiler_params=_params(("arbitrary",)), name="rope_tables",
    )(pos_f, freqs, signs)


def _swap_half(y, half):
    lane = lax.broadcasted_iota(I32, y.shape, 1)
    return jnp.where((lane & half) == 0, pltpu.roll(y, LANES - half, 1), pltpu.roll(y, half, 1))


def _rope(y, cos, sin_signed, half):
    return y * cos + _swap_half(y, half) * sin_signed


def _seg_mean_sq(y, mseg):
    sq = y * y
    hi = sq.astype(BF16)
    lo = (sq - hi.astype(F32)).astype(BF16)
    return _mm(hi, mseg) + _mm(lo, mseg)


def _seg_norm(y, mseg, gain):
    return y * lax.rsqrt(_seg_mean_sq(y, mseg) + EPS) * gain


def _slab(s):
    return slice(s * LANES, (s + 1) * LANES)


def _proj_kernel(x_ref, g_ref, w_ref, wuq_ref, wukv_ref, gcq_ref, grows_ref, mseg_ref,
                 c64_ref, s64_ref, c32_ref, s32_ref,
                 qa_ref, kva_ref, iq_ref, ikt_ref, iw_ref, qb_ref, kb_ref, vb_ref,
                 qc_ref, kc_ref, vc_ref, gt_ref):
    x = x_ref[...]
    h = (x * lax.rsqrt(jnp.mean(x * x, axis=-1, keepdims=True) + EPS) * g_ref[...]).astype(BF16)
    c64, s64, c32, s32 = c64_ref[...], s64_ref[...], c32_ref[...], s32_ref[...]
    m64, mqb = mseg_ref[0], mseg_ref[1]
    lane = lax.broadcasted_iota(I32, c64.shape, 1)
    in_rope = (lane >= B_NOPE) & (lane < B_NOPE + B_ROPE)
    cb = jnp.where(in_rope, c32, 1.0)
    sb = jnp.where(in_rope, s32, 0.0)

    p = _mm(h, w_ref[:, C_QA:C_CQ])
    for s in range(2):
        y = _rope(_seg_norm(p[:, _slab(s)], m64, grows_ref[0:1, :]), c64, s64, 32)
        qa_ref[:, _slab(s)] = (y * HEAD_DIM ** -0.5).astype(BF16)
    y = p[:, _slab(2)]
    yr = _rope(_seg_norm(y, m64, grows_ref[1:2, :]), c64, s64, 32)
    kva_ref[...] = jnp.where(lane < HEAD_DIM, yr, y).astype(BF16)
    for s in range(2):
        iq_ref[:, _slab(s)] = _rope(p[:, _slab(3 + s)], c32, s32, 16).astype(BF16)
    y = p[:, _slab(5)]
    ikt = _rope(y, c32, s32, 16).T
    for j in range(ikt_ref.shape[0]):
        ikt_ref[j] = ikt[0:IDX_DIM, j * IDX_CHUNK:(j + 1) * IDX_CHUNK].astype(BF16)
    iw_ref[...] = y * (IDX_HEADS * IDX_DIM) ** -0.5

    p = _mm(h, w_ref[:, C_CQ:C_QC])
    cq = p[:, 0:256]
    cqn = cq * lax.rsqrt(jnp.mean(cq * cq, axis=-1, keepdims=True) + EPS) * gcq_ref[...]
    qb = _mm(cqn.astype(BF16), wuq_ref[...])
    for s in range(B_HEADS):
        y = _rope(_seg_norm(qb[:, _slab(s)], mqb, grows_ref[2:3, :]), cb, sb, 16)
        qb_ref[:, _slab(s)] = (y * (B_NOPE + B_ROPE) ** -0.5).astype(BF16)
    ckv = p[:, 256:384]
    ckvn = ckv * lax.rsqrt(jnp.mean(ckv * ckv, axis=-1, keepdims=True) + EPS) * grows_ref[7:8, :]
    kvb = _mm(ckvn.astype(BF16), wukv_ref[...])
    krs = p[:, 384:512]
    kr = krs * lax.rsqrt(jnp.sum(krs * krs, axis=-1, keepdims=True) * (1.0 / B_ROPE) + EPS) * grows_ref[4:5, :]
    kr = _rope(kr, cb, sb, 16)
    for s in range(B_HEADS):
        kb_ref[:, _slab(s)] = (_seg_norm(kvb[:, _slab(s)], m64, grows_ref[3:4, :]) + kr).astype(BF16)
    vb_ref[...] = kvb[:, 512:768].astype(BF16)

    p = _mm(h, w_ref[:, C_QC:C_G])
    for s in range(4):
        y = _rope(_seg_norm(p[:, _slab(s)], m64, grows_ref[5:6, :]), c64, s64, 32)
        qc_ref[:, _slab(s)] = (y * HEAD_DIM ** -0.5).astype(BF16)
    y = _rope(_seg_norm(p[:, _slab(4)], m64, grows_ref[6:7, :]), c64, s64, 32)
    kc_ref[...] = y.astype(BF16)
    vc_ref[...] = p[:, _slab(5)].astype(BF16)

    for c in range(3):
        lo = C_G + c * 1024
        gt_ref[:, c * 1024:(c + 1) * 1024] = _sigmoid(_mm(h, w_ref[:, lo:lo + 1024]))


def _proj(xf, g, w_r, wuq, wukv, gcq, grows, mseg, tabs, tm=256):
    n, d = xf.shape
    row = lambda w: pl.BlockSpec((tm, w), lambda i: (i, 0))
    full = lambda a: pl.BlockSpec(a.shape, lambda i: (0,) * a.ndim)
    widths = [(256, BF16), (128, BF16), (256, BF16), None, (128, F32), (512, BF16),
              (512, BF16), (256, BF16), (512, BF16), (128, BF16), (128, BF16), (3072, F32)]
    ikt_shape = jax.ShapeDtypeStruct((n // IDX_CHUNK, IDX_DIM, IDX_CHUNK), BF16)
    ikt_spec = pl.BlockSpec((tm // IDX_CHUNK, IDX_DIM, IDX_CHUNK), lambda i: (i, 0, 0))
    return pl.pallas_call(
        _proj_kernel,
        out_shape=tuple(ikt_shape if w is None else jax.ShapeDtypeStruct((n, w[0]), w[1]) for w in widths),
        grid=(n // tm,),
        in_specs=[row(d), full(g), full(w_r), full(wuq), full(wukv), full(gcq), full(grows), full(mseg)]
        + [row(LANES)] * 4,
        out_specs=tuple(ikt_spec if w is None else row(w[0]) for w in widths),
        compiler_params=_params(("arbitrary",)), name="in_proj",
    )(xf, g, w_r, wuq, wukv, gcq, grows, mseg, *tabs)


def _dsa_kernel(iq_ref, iw_ref, qa_ref, ikt_ref, kv_ref, o_ref, keys_ref, *, tq, nsel, seq):
    ck = IDX_CHUNK
    sub = 128
    i = pl.program_id(1)
    npair = (i * tq + tq + 2 * ck - 1) // (2 * ck)
    nk = 2 * npair
    iq = iq_ref[...]
    iw = iw_ref[...]
    iq_h = [[iq[r * sub:(r + 1) * sub, h * IDX_DIM:(h + 1) * IDX_DIM] for h in range(IDX_HEADS)]
            for r in range(tq // sub)]
    iw_b = [[jnp.broadcast_to(iw[r * sub:(r + 1) * sub, IDX_DIM + h:IDX_DIM + h + 1], (sub, LANES))
             for h in range(IDX_HEADS)] for r in range(tq // sub)]
    qpos_sub = lax.broadcasted_iota(I32, (sub, ck), 0) + i * tq
    col_sub = lax.broadcasted_iota(I32, (sub, ck), 1)
    col = lax.broadcasted_iota(I32, (tq, ck), 1)

    def score_body(c, carry):
        ikc = ikt_ref[c]
        for r in range(tq // sub):
            acc = [jnp.zeros((sub, LANES), F32) for _ in range(ck // LANES)]
            for h in range(IDX_HEADS):
                sc = _mm(iq_h[r][h], ikc)
                for s in range(ck // LANES):
                    acc[s] = acc[s] + jnp.maximum(sc[:, _slab(s)], 0.0) * iw_b[r][h]
            bits = lax.bitcast_convert_type(jnp.concatenate(acc, axis=1), I32)
            key = bits ^ ((bits >> 31) & 0x7FFFFFFF)
            keys_ref[c, r * sub:(r + 1) * sub, :] = jnp.where(col_sub + c * ck <= qpos_sub + r * sub, key, INT_MIN)
        return carry

    lax.fori_loop(0, nk, score_body, 0)

    lane_sub = lax.broadcasted_iota(I32, (sub, LANES), 1)

    ones_blk = jnp.ones((LANES, LANES), BF16)

    def count(pred, *ops):
        def half(r):
            rows = slice(r * sub, (r + 1) * sub)
            ops_r = [v[rows] for v in ops]

            def body(pair, acc):
                for j in range(2):
                    c = 2 * pair + j
                    for s in range(ck // LANES):
                        hit = pred(keys_ref[c, rows, _slab(s)], c * ck + s * LANES, *ops_r)
                        acc = acc + jnp.where(hit, 1.0, 0.0)
                return acc
            return lax.fori_loop(0, npair, body, jnp.zeros((sub, LANES), F32))
        acc = jnp.concatenate([half(r) for r in range(tq // sub)], axis=0)
        return _mm(acc.astype(BF16), ones_blk)

    def bis_cond(state):
        return (state[0] < 32) & (state[3] < 0.5)

    def bis_body(state):
        it, thr, frozen, _ = state
        all_frozen = jnp.min(frozen)
        cand = thr + jnp.left_shift(jnp.int32(1), 31 - it)
        cnt = count(lambda k, base, t: k >= t, cand)
        thr = jnp.where(frozen > 0.5, thr, jnp.where(cnt >= nsel, cand, thr))
        frozen = jnp.where(cnt == nsel, 1.0, frozen)
        return it + 1, thr, frozen, all_frozen

    _, thr_b, _, _ = lax.while_loop(bis_cond, bis_body, (jnp.int32(0), jnp.full((tq, LANES), INT_MIN, I32),
                                                         jnp.zeros((tq, LANES), F32), jnp.float32(0.0)))

    need = nsel - count(lambda k, base, t: k > t, thr_b)
    excess = (count(lambda k, base, t: k == t, thr_b) > need) & (thr_b != INT_MIN)
    last0 = jnp.where(thr_b == INT_MIN, -1, seq).astype(I32)
    nbits = seq.bit_length() - 1

    def tie_fix(_):
        def body(it, p):
            cand = p + jnp.left_shift(jnp.int32(1), nbits - 1 - it)
            ties_before = count(lambda k, base, t, cd: (k == t) & (lane_sub + base < cd), thr_b, cand)
            return jnp.where(ties_before < need, cand, p)
        p = lax.fori_loop(0, nbits, body, jnp.zeros((tq, LANES), I32))
        return jnp.where(excess, p, last0)

    last_b = lax.cond(jnp.max(jnp.where(excess, 1.0, 0.0)) > 0.0, tie_fix, lambda _: last0, 0)

    qa = qa_ref[...]
    pairs = A_HEADS // 2
    q2 = [jnp.concatenate([qa[:, (2 * g + j) * HEAD_DIM:(2 * g + j + 1) * HEAD_DIM] for j in range(2)], axis=0)
          for g in range(pairs)]

    lane_q =lax.broadcasted_iota(I32, (tq, LANES), 1)

    def att_body(c, carry):
        kvc = kv_ref[pl.ds(pl.multiple_of(c * ck, ck), ck), :]
        k, v = kvc[:, 0:HEAD_DIM], kvc[:, HEAD_DIM:2 * HEAD_DIM]
        bias = []
        for s in range(ck // LANES):
            key = keys_ref[c, :, _slab(s)]
            sel = (key > thr_b) | ((key == thr_b) & (lane_q + (c * ck + s * LANES) <= last_b))
            bias.append(jnp.where(sel, 0.0, NEG_BIG))
        bias = jnp.concatenate(bias, axis=1)
        bias2 = jnp.concatenate([bias, bias], axis=0)
        return tuple(_flash_step(q2[g], k, v, carry[g], bias2) for g in range(pairs))

    init = (jnp.full((2 * tq, 1), NEG_BIG, F32), jnp.zeros((2 * tq, 1), F32), jnp.zeros((2 * tq, HEAD_DIM), F32))
    res = lax.fori_loop(0, nk, att_body, (init,) * pairs)
    outs = []
    for m, l, acc in res:
        o = acc / l
        outs += [o[0:tq], o[tq:2 * tq]]
    o_ref[...] = jnp.concatenate(outs, axis=1).astype(BF16)


def _dsa(iq, iw, qa, ikt, kva, b, s, tq=256):
    n = b * s
    nq = s // tq
    nchunk = s // IDX_CHUNK
    nsel = min(TOPK_MAX, s // 4)
    qrow = lambda w: pl.BlockSpec((tq, w), lambda bi, i: (bi * nq + i, 0))
    return pl.pallas_call(
        functools.partial(_dsa_kernel, tq=tq, nsel=nsel, seq=s),
        out_shape=jax.ShapeDtypeStruct((n, 256), BF16), grid=(b, nq),
        in_specs=[qrow(256), qrow(LANES), qrow(256),
                  pl.BlockSpec((nchunk, IDX_DIM, IDX_CHUNK), lambda bi, i: (bi, 0, 0)),
                  pl.BlockSpec((s, LANES), lambda bi, i: (bi, 0))],
        out_specs=qrow(256),
        scratch_shapes=[pltpu.VMEM((nchunk, tq, IDX_CHUNK), I32)],
        compiler_params=_params(("arbitrary", "arbitrary")), name="dsa_attention",
    )(iq, iw, qa, ikt, kva)


def _flash_step(q, k, v, carry, bias=None):
    m, l, acc = carry
    s = _mm_nt(q, k)
    if bias is not None:
        s = s + bias
    m_new = jnp.maximum(m, jnp.max(s, axis=1, keepdims=True))
    alpha = jnp.exp(m - m_new)
    p = jnp.exp(s - m_new)
    l = alpha * l + jnp.sum(p, axis=1, keepdims=True)
    acc = alpha * acc + _mm(p.astype(BF16), v)
    return m_new, l, acc


def _mla_kernel(q_ref, k_ref, v_ref, o_ref, *, tq):
    i = pl.program_id(1)
    q = q_ref[...]
    r = lax.broadcasted_iota(I32, (tq, tq), 0)
    c = lax.broadcasted_iota(I32, (tq, tq), 1)
    diag_bias = jnp.where(c <= r, 0.0, NEG_BIG)
    qh = [q[:, _slab(h)] for h in range(B_HEADS)]

    def step(cidx, carry, bias=None):
        rows = pl.ds(pl.multiple_of(cidx * tq, tq), tq)
        return tuple(_flash_step(qh[h], k_ref[rows, _slab(h)], v_ref[rows, h * B_V:(h + 1) * B_V], carry[h], bias)
                     for h in range(B_HEADS))

    init = (jnp.full((tq, 1), NEG_BIG, F32), jnp.zeros((tq, 1), F32), jnp.zeros((tq, B_V), F32))
    carry = lax.fori_loop(0, i, step, (init,) * B_HEADS)
    res = step(i, carry, diag_bias)
    o_ref[...] = jnp.concatenate([acc / l for _, l, acc in res], axis=1).astype(BF16)


def _mla(qb, kb, vb, b, s, tq=256):
    n = b * s
    nq = s // tq
    qrow = lambda w: pl.BlockSpec((tq, w), lambda bi, i: (bi * nq + i, 0))
    seq_blk = lambda w: pl.BlockSpec((s, w), lambda bi, i: (bi, 0))
    return pl.pallas_call(
        functools.partial(_mla_kernel, tq=tq),
        out_shape=jax.ShapeDtypeStruct((n, B_HEADS * B_V), BF16), grid=(b, nq),
        in_specs=[qrow(512), seq_blk(512), seq_blk(256)], out_specs=qrow(256),
        compiler_params=_params(("arbitrary", "arbitrary")), name="mla_attention",
    )(qb, kb, vb)


def _swa_kernel(sink_ref, q_ref, kp_ref, kc_ref, vp_ref, vc_ref, o_ref, *, tq):
    i = pl.program_id(1)
    q = q_ref[...]
    keys = jnp.concatenate([kp_ref[...], kc_ref[...]], axis=0)
    vals = jnp.concatenate([vp_ref[...], vc_ref[...]], axis=0)
    nkeys = WINDOW + tq
    qpos = lax.broadcasted_iota(I32, (tq, nkeys), 0) + i * tq
    kpos = lax.broadcasted_iota(I32, (tq, nkeys), 1) + i * tq - WINDOW
    visible = (kpos > qpos - WINDOW) & (kpos <= qpos) & (kpos >= 0)
    bias = jnp.where(visible, 0.0, NEG_BIG)
    group = C_HEADS // C_KV_HEADS
    outs = []
    for h in range(C_HEADS):
        kvh = h // group
        k = keys[:, kvh * HEAD_DIM:(kvh + 1) * HEAD_DIM]
        v = vals[:, kvh * HEAD_DIM:(kvh + 1) * HEAD_DIM]
        s = _mm_nt(q[:, h * HEAD_DIM:(h + 1) * HEAD_DIM], k) + bias
        sink = sink_ref[h]
        m = jnp.maximum(jnp.max(s, axis=1, keepdims=True), sink)
        p = jnp.exp(s - m)
        denom = jnp.sum(p, axis=1, keepdims=True) + jnp.exp(sink - m)
        outs.append(_mm((p / denom).astype(BF16), v))
    o_ref[...] = jnp.concatenate(outs, axis=1).astype(BF16)


def _swa(sinks, qc, kc, vc, b, s, tq=256):
    n = b * s
    nq = s // tq
    per = tq // WINDOW
    qrow = lambda w: pl.BlockSpec((tq, w), lambda bi, i: (bi * nq + i, 0))
    prev = pl.BlockSpec((WINDOW, LANES), lambda bi, i: (jnp.maximum((bi * nq + i) * per - 1, 0), 0))
    return pl.pallas_call(
        functools.partial(_swa_kernel, tq=tq),
        out_shape=jax.ShapeDtypeStruct((n, C_HEADS * HEAD_DIM), BF16), grid=(b, nq),
        in_specs=[pl.BlockSpec(memory_space=pltpu.SMEM), qrow(512), prev, qrow(LANES), prev, qrow(LANES)],
        out_specs=qrow(512),
        compiler_params=_params(("arbitrary", "arbitrary")), name="swa_attention",
    )(sinks, qc, kc, kc, vc, vc)


def _merge_kernel(ya_ref, yb_ref, yc_ref, gt_ref, x_ref, wa_ref, wb_ref, wc_ref, wo_ref, g_ref,
                  xo_ref, h_ref):
    d = x_ref.shape[1]
    merged = (gt_ref[:, 0:d] * _mm(ya_ref[...], wa_ref[...])
              + gt_ref[:, d:2 * d] * _mm(yb_ref[...], wb_ref[...])
              + gt_ref[:, 2 * d:3 * d] * _mm(yc_ref[...], wc_ref[...]))
    xn = x_ref[...] + _mm(merged.astype(BF16), wo_ref[...])
    xo_ref[...] = xn
    h = xn * lax.rsqrt(jnp.mean(xn * xn, axis=-1, keepdims=True) + EPS) * g_ref[...]
    h_ref[...] = h.astype(h_ref.dtype)


def _merge(ya, yb, yc, gt, xf, wa, wb, wc, wo, g, h_dtype, tm=256):
    n, d = xf.shape
    row = lambda w: pl.BlockSpec((tm, w), lambda i: (i, 0))
    full = lambda a: pl.BlockSpec(a.shape, lambda i: (0,) * a.ndim)
    return pl.pallas_call(
        _merge_kernel,
        out_shape=(jax.ShapeDtypeStruct((n, d), F32), jax.ShapeDtypeStruct((n, d), h_dtype)),
        grid=(n // tm,),
        in_specs=[row(256), row(256), row(512), row(3 * d), row(d), full(wa), full(wb), full(wc), full(wo),
                  full(g)],
        out_specs=(row(d), row(d)),
        compiler_params=_params(("arbitrary",)), name="merge_out_proj",
    )(ya, yb, yc, gt, xf, wa, wb, wc, wo, g)


def _ffn_kernel(h_ref, x_ref, w1_hbm, w3_hbm, w2_hbm, o_ref, w1_ref, w3_ref, w2_ref, sem, *, chunk):
    @pl.when(pl.program_id(0) == 0)
    def _():
        copies = [pltpu.make_async_copy(src, dst, sem.at[j]) for j, (src, dst) in
                  enumerate(((w1_hbm, w1_ref), (w3_hbm, w3_ref), (w2_hbm, w2_ref)))]
        for cp in copies:
            cp.start()
        for cp in copies:
            cp.wait()

    h = h_ref[...]
    acc = x_ref[...]
    for j in range(w1_ref.shape[1] // chunk):
        cols = slice(j * chunk, (j + 1) * chunk)
        a = _mm(h, w1_ref[:, cols])
        mid = a * _sigmoid(a) * _mm(h, w3_ref[:, cols])
        acc = acc + _mm(mid.astype(BF16), w2_ref[cols, :])
    o_ref[...] = acc


def _ffn(h2, xf, w1, w3, w2, tm=256, chunk=1408):
    n, d = xf.shape
    row = lambda w: pl.BlockSpec((tm, w), lambda i: (i, 0))
    hbm = pl.BlockSpec(memory_space=pl.ANY)
    return pl.pallas_call(
        functools.partial(_ffn_kernel, chunk=chunk),
        out_shape=jax.ShapeDtypeStruct((n, d), F32), grid=(n // tm,),
        in_specs=[row(d), row(d), hbm, hbm, hbm], out_specs=row(d),
        scratch_shapes=[pltpu.VMEM(w1.shape, BF16), pltpu.VMEM(w3.shape, BF16), pltpu.VMEM(w2.shape, BF16),
                        pltpu.SemaphoreType.DMA((3,))],
        compiler_params=_params(("arbitrary",)), name="dense_swiglu",
    )(h2, xf, w1, w3, w2)


def _route_kernel(h_ref, wr_ref, tri_ref, route_ref, cnt_ref, carry_ref):
    @pl.when(pl.program_id(0) == 0)
    def _():
        carry_ref[...] = jnp.zeros_like(carry_ref)

    logits = jnp.dot(h_ref[...], wr_ref[...], precision=lax.Precision.HIGHEST, preferred_element_type=F32)
    lane = lax.broadcasted_iota(I32, logits.shape, 1).astype(F32)
    lg = jnp.where(lane < N_EXPERTS, logits, -jnp.inf)
    m1 = jnp.max(lg, axis=1, keepdims=True)
    e1 = jnp.min(jnp.where(lg == m1, lane, float(LANES)), axis=1, keepdims=True)
    lg2 = jnp.where(lane == e1, -jnp.inf, lg)
    m2 = jnp.max(lg2, axis=1, keepdims=True)
    e2 = jnp.min(jnp.where(lg2 == m2, lane, float(LANES)), axis=1, keepdims=True)
    ex = jnp.exp(m2 - m1)
    g1 = 1.0 / (1.0 + ex)
    g2 = ex / (1.0 + ex)
    onehot = jnp.where((lane == e1) | (lane == e2), 1.0, 0.0)
    before = _mm(tri_ref[...], onehot.astype(BF16)) + carry_ref[0:1, :]
    r1 = jnp.sum(jnp.where(lane == e1, before, 0.0), axis=1, keepdims=True)
    r2 = jnp.sum(jnp.where(lane == e2, before, 0.0), axis=1, keepdims=True)
    out = jnp.zeros_like(logits)
    for idx, val in enumerate((e1, e2, g1, g2, r1, r2)):
        out = jnp.where(lane == idx, val, out)
    route_ref[...] = out
    total = carry_ref[0:1, :] + jnp.sum(onehot, axis=0, keepdims=True)
    carry_ref[...] = jnp.broadcast_to(total, carry_ref.shape)
    cnt_ref[...] = jnp.broadcast_to(total, cnt_ref.shape)


def _route(h2, wr, tm=256):
    n, d = h2.shape
    tri = jnp.asarray(np.tril(np.ones((tm, tm), np.float32), -1), BF16)
    full = lambda a: pl.BlockSpec(a.shape, lambda i: (0,) * a.ndim)
    return pl.pallas_call(
        _route_kernel,
        out_shape=(jax.ShapeDtypeStruct((n, LANES), F32), jax.ShapeDtypeStruct((8, LANES), F32)),
        grid=(n // tm,),
        in_specs=[pl.BlockSpec((tm, d), lambda i: (i, 0)), full(wr), full(tri)],
        out_specs=(pl.BlockSpec((tm, LANES), lambda i: (i, 0)), pl.BlockSpec((8, LANES), lambda i: (0, 0))),
        scratch_shapes=[pltpu.VMEM((8, LANES), F32)],
        compiler_params=_params(("arbitrary",)), name="moe_route",
    )(h2, wr, tri)


def _scatter_kernel(dest_ref, h_ref, xin_hbm, xout_hbm, sem, *, tt):
    del xin_hbm

    def copy(r, k):
        d = dest_ref[0, 0, 2 * r + k]
        return pltpu.make_async_copy(h_ref.at[pl.ds(r, 1), :], xout_hbm.at[pl.ds(d, 1), :], sem)

    def start(r, carry):
        copy(r, 0).start()
        copy(r, 1).start()
        return carry

    def wait(r, carry):
        copy(r, 0).wait()
        copy(r, 1).wait()
        return carry

    lax.fori_loop(0, tt, start, 0)
    lax.fori_loop(0, tt, wait, 0)


def _scatter(dest3, h2, rows, tt):
    n, w = h2.shape
    hbm = pl.BlockSpec(memory_space=pl.ANY)
    return pl.pallas_call(
        functools.partial(_scatter_kernel, tt=tt),
        out_shape=jax.ShapeDtypeStruct((rows, w), h2.dtype), grid=(n // tt,),
        in_specs=[pl.BlockSpec((1, 1, 2 * tt), lambda i: (i, 0, 0), memory_space=pltpu.SMEM),
                  pl.BlockSpec((tt, w), lambda i: (i, 0)), hbm],
        out_specs=hbm, scratch_shapes=[pltpu.SemaphoreType.DMA(())],
        input_output_aliases={2: 0},
        compiler_params=_params(("arbitrary",)), name="moe_scatter",
    )(dest3, h2, jnp.zeros((rows, w), h2.dtype))


def _expert_kernel(be_ref, nv_ref, x_ref, w1_hbm, w3_hbm, w2_hbm, y_ref, w1_ref, w3_ref, w2_ref, sem, *, chunk):
    blk = pl.program_id(0)
    e = be_ref[blk]

    @pl.when((blk == 0) | (e != be_ref[jnp.maximum(blk - 1, 0)]))
    def _():
        copies = [pltpu.make_async_copy(src.at[e], dst, sem.at[j]) for j, (src, dst) in
                  enumerate(((w1_hbm, w1_ref), (w3_hbm, w3_ref), (w2_hbm, w2_ref)))]
        for cp in copies:
            cp.start()
        for cp in copies:
            cp.wait()

    @pl.when(blk < nv_ref[0])
    def _():
        x = x_ref[...].astype(BF16)
        acc = jnp.zeros(y_ref.shape, F32)
        for j in range(w1_ref.shape[1] // chunk):
            cols = slice(j * chunk, (j + 1) * chunk)
            a = _mm(x, w1_ref[:, cols])
            mid = a * _sigmoid(a) * _mm(x, w3_ref[:, cols])
            acc = acc + _mm(mid.astype(BF16), w2_ref[cols, :])
        y_ref[...] = acc

    @pl.when(blk >= nv_ref[0])
    def _():
        y_ref[...] = jnp.zeros(y_ref.shape, F32)


def _experts(block_e, nvalid, xbuf, w1, w3, w2, chunk=512):
    rows, d = xbuf.shape
    hbm = pl.BlockSpec(memory_space=pl.ANY)
    blk = pl.BlockSpec((MOE_BLOCK, d), lambda i, be, nv: (i, 0))
    return pl.pallas_call(
        functools.partial(_expert_kernel, chunk=chunk),
        out_shape=jax.ShapeDtypeStruct((rows, d), F32),
        grid_spec=pltpu.PrefetchScalarGridSpec(
            num_scalar_prefetch=2, grid=(rows // MOE_BLOCK,),
            in_specs=[blk, hbm, hbm, hbm], out_specs=blk,
            scratch_shapes=[pltpu.VMEM(w1.shape[1:], BF16), pltpu.VMEM(w3.shape[1:], BF16),
                            pltpu.VMEM(w2.shape[1:], BF16), pltpu.SemaphoreType.DMA((3,))]),
        compiler_params=_params(("arbitrary",)), name="moe_experts",
    )(block_e, nvalid, xbuf, w1, w3, w2)


def _combine_kernel(dest_ref, y_hbm, x_ref, route_ref, o_ref, buf_ref, sem, *, tt):
    def copy(r, k):
        d = dest_ref[0, 0, 2 * r + k]
        return pltpu.make_async_copy(y_hbm.at[pl.ds(d, 1), :], buf_ref.at[k, pl.ds(r, 1), :], sem)

    def start(r, carry):
        copy(r, 0).start()
        copy(r, 1).start()
        return carry

    def wait(r, carry):
        copy(r, 0).wait()
        copy(r, 1).wait()
        return carry

    lax.fori_loop(0, tt, start, 0)
    lax.fori_loop(0, tt, wait, 0)
    rt = route_ref[...]
    o_ref[...] = x_ref[...] + (buf_ref[0] * rt[:, 2:3] + buf_ref[1] * rt[:, 3:4])


def _combine(dest3, ybuf, xf, route, tt):
    n, d = xf.shape
    row = lambda w: pl.BlockSpec((tt, w), lambda i: (i, 0))
    return pl.pallas_call(
        functools.partial(_combine_kernel, tt=tt),
        out_shape=jax.ShapeDtypeStruct((n, d), F32), grid=(n // tt,),
        in_specs=[pl.BlockSpec((1, 1, 2 * tt), lambda i: (i, 0, 0), memory_space=pltpu.SMEM),
                  pl.BlockSpec(memory_space=pl.ANY), row(d), row(LANES)],
        out_specs=row(d),
        scratch_shapes=[pltpu.VMEM((2, tt, d), F32), pltpu.SemaphoreType.DMA(())],
        compiler_params=_params(("arbitrary",)), name="moe_combine",
    )(dest3, ybuf, xf, route)


def _moe(xf, h2, w_router, w1, w3, w2, tt=256):
    n, d = xf.shape
    wr = jnp.pad(w_router, ((0, 0), (0, LANES - N_EXPERTS)))
    route, cnt = _route(h2, wr)
    counts = cnt[0, :N_EXPERTS].astype(I32)
    padded = (counts + MOE_BLOCK - 1) // MOE_BLOCK * MOE_BLOCK
    pend = jnp.cumsum(padded)
    pstart = pend - padded
    dest = pstart[route[:, 0:2].astype(I32)] + route[:, 4:6].astype(I32)
    dest3 = dest.reshape(n // tt, 1, 2 * tt)
    n_blocks = n * 2 // MOE_BLOCK + N_EXPERTS
    rows = n_blocks * MOE_BLOCK
    block_e = jnp.minimum(jnp.searchsorted(pend, jnp.arange(n_blocks) * MOE_BLOCK, side='right'),
                          N_EXPERTS - 1).astype(I32)
    nvalid = (pend[-1:] // MOE_BLOCK).astype(I32)
    xbuf = _scatter(dest3, h2, rows, tt)
    ybuf = _experts(block_e, nvalid, xbuf, w1, w3, w2)
    return _combine(dest3, ybuf, xf, route, tt)


def _tile_row(v, width=LANES):
    v = v.astype(F32)
    return jnp.tile(v, width // v.shape[0])


def _layer_params(l, attn_norm_g, w_in, a_q_norm_g, a_k_norm_g, b_cq_norm_g, b_ckv_norm_g, b_w_uq, b_w_ukv,
                  b_qn_g, b_qr_g, b_kn_g, b_kr_g, c_q_norm_g, c_k_norm_g):
    d = w_in.shape[1]
    sizes = (A_HEADS * HEAD_DIM, HEAD_DIM, HEAD_DIM, IDX_HEADS * IDX_DIM, IDX_DIM, IDX_HEADS,
             b_w_uq.shape[1], b_w_ukv.shape[1], B_ROPE, C_HEADS * HEAD_DIM, C_KV_HEADS * HEAD_DIM,
             C_KV_HEADS * HEAD_DIM, 3 * d)
    qa, ka, va, iq, ik, iw, cq, ckv, kr, qc, kc, vc, gates = jnp.split(w_in[l], np.cumsum(sizes)[:-1].tolist(), axis=1)
    z = lambda k: jnp.zeros((d, k), F32)
    w_r = jnp.concatenate([qa, ka, va, iq, ik, iw, z(LANES - IDX_DIM - IDX_HEADS), cq, ckv,
                           z(B_NOPE), kr, z(LANES - B_NOPE - B_ROPE), qc, kc, vc, gates], axis=1).astype(BF16)
    uq = b_w_uq[l].reshape(-1, B_HEADS, B_NOPE + B_ROPE)
    uq = jnp.pad(uq, ((0, 0), (0, 0), (0, LANES - B_NOPE - B_ROPE))).reshape(-1, B_HEADS * LANES).astype(BF16)
    ukv = b_w_ukv[l].reshape(-1, B_HEADS, B_NOPE + B_V)
    uk = jnp.pad(ukv[:, :, :B_NOPE], ((0, 0), (0, 0), (0, LANES - B_NOPE))).reshape(-1, B_HEADS * LANES)
    uv = ukv[:, :, B_NOPE:].reshape(-1, B_HEADS * B_V)
    wukv = jnp.concatenate([uk, uv], axis=1).astype(BF16)
    zeros = lambda k: jnp.zeros((k,), F32)
    grows = jnp.stack([
        _tile_row(a_q_norm_g[l]),
        jnp.concatenate([a_k_norm_g[l], jnp.ones((HEAD_DIM,), F32)]),
        jnp.concatenate([b_qn_g[l], b_qr_g[l], zeros(LANES - B_NOPE - B_ROPE)]),
        jnp.concatenate([b_kn_g[l], zeros(LANES - B_NOPE)]),
        jnp.concatenate([zeros(B_NOPE), b_kr_g[l], zeros(LANES - B_NOPE - B_ROPE)]),
        _tile_row(c_q_norm_g[l]),
        _tile_row(c_k_norm_g[l]),
        b_ckv_norm_g[l],
    ]).astype(F32)
    return (attn_norm_g[l][None, :], w_r, uq, wukv, b_cq_norm_g[l][None, :], grows)


def _seg_matrices():
    lane = np.arange(LANES)
    g64 = lane // 64
    m64 = (g64[:, None] == g64[None, :]) / 64.0
    gqb = np.where(lane < B_NOPE, 0, np.where(lane < B_NOPE + B_ROPE, 1, 2))
    size = np.where(lane < B_NOPE, B_NOPE, B_ROPE)
    mqb = (gqb[:, None] == gqb[None, :]) / size[None, :]
    return jnp.asarray(np.stack([m64, mqb]), BF16)


def kernel(x, positions, attn_norm_g, w_in, a_q_norm_g, a_k_norm_g, b_cq_norm_g, b_ckv_norm_g, b_w_uq, b_w_ukv,
           b_qn_g, b_qr_g, b_kn_g, b_kr_g, c_q_norm_g, c_k_norm_g, c_sinks, w_a_out, w_b_out, w_c_out, w_o,
           ffn_norm_g, ffn_w1, ffn_w3, ffn_w2, router_w, moe_w1, moe_w3, moe_w2):
    b, s, d = x.shape
    n = b * s
    depth = w_in.shape[0]
    xf = x.reshape(n, d)
    tabs = _rope_tables(positions.reshape(n, 1).astype(F32))
    mseg = _seg_matrices()
    for l in range(depth):
        g, w_r, uq, wukv, gcq, grows = _layer_params(
            l, attn_norm_g, w_in, a_q_norm_g, a_k_norm_g, b_cq_norm_g, b_ckv_norm_g, b_w_uq, b_w_ukv,
            b_qn_g, b_qr_g, b_kn_g, b_kr_g, c_q_norm_g, c_k_norm_g)
        qa, kva, iq, ikt, iw, qb, kb, vb, qc, kc, vc, gt = _proj(xf, g, w_r, uq, wukv, gcq, grows, mseg, tabs)
        ya = _dsa(iq, iw, qa, ikt, kva, b, s)
        yb = _mla(qb, kb, vb, b, s)
        yc = _swa(c_sinks[l], qc, kc, vc, b, s)
        dense = l % 2 == 0
        xf, h2 = _merge(ya, yb, yc, gt, xf, w_a_out[l].astype(BF16), w_b_out[l].astype(BF16),
                        w_c_out[l].astype(BF16), w_o[l].astype(BF16), ffn_norm_g[l][None, :],
                        BF16 if dense else F32)
        if dense:
            xf = _ffn(h2, xf, ffn_w1[l // 2].astype(BF16), ffn_w3[l // 2].astype(BF16), ffn_w2[l // 2].astype(BF16))
        else:
            xf = _moe(xf, h2, router_w[l // 2], moe_w1[l // 2].astype(BF16),
                      moe_w3[l // 2].astype(BF16), moe_w2[l // 2].astype(BF16))
    return xf.reshape(b, s, d)
```

```python
import functools

import numpy as np
import jax
import jax.numpy as jnp
from jax import lax
from jax.experimental import pallas as pl
from jax.experimental.pallas import tpu as pltpu

F32, BF16, I32 = jnp.float32, jnp.bfloat16, jnp.int32

EPS = 1e-6
ROPE_THETA = 10000.0
HEAD_DIM = 64
A_HEADS = 4
IDX_HEADS = 8
IDX_DIM = 32
TOPK_MAX = 256
B_HEADS = 4
B_NOPE = 64
B_ROPE = 32
B_V = 64
C_HEADS = 8
C_KV_HEADS = 2
WINDOW = 128
N_EXPERTS = 8
MOE_BLOCK = 256

LANES = 128
SUB = 8
VMEM_LIMIT = 56 * 1024 * 1024
INT_MIN = -2 ** 31
NEG_BIG = -1e30
LOG2E = 1.4426950408889634

C_QA, C_KVA, C_IQ, C_IKW, C_CQ, C_CKV, C_KR, C_QC, C_KC, C_VC, C_G = (
    0, 256, 384, 640, 768, 1024, 1152, 1280, 1792, 1920, 2048)


def _params(sem):
    return pltpu.CompilerParams(dimension_semantics=sem, vmem_limit_bytes=VMEM_LIMIT)


def _mm(a, b):
    return jnp.dot(a, b, preferred_element_type=F32)


def _mm_nt(a, b):
    return lax.dot_general(a, b, (((1,), (1,)), ((), ())), preferred_element_type=F32)


def _sigmoid(v):
    return 1.0 / (1.0 + jnp.exp(-v))


def _rope_table_kernel(pos_ref, f_ref, sg_ref, c64_ref, s64_ref, c32_ref, s32_ref):
    pos = pos_ref[...]
    a = pos * f_ref[0:1, :]
    c64_ref[...] = jnp.cos(a)
    s64_ref[...] = jnp.sin(a) * sg_ref[0:1, :]
    a = pos * f_ref[1:2, :]
    c32_ref[...] = jnp.cos(a)
    s32_ref[...] = jnp.sin(a) * sg_ref[1:2, :]


def _rope_tables(pos_f, tm=1024):
    n = pos_f.shape[0]
    lane = np.arange(LANES)
    f32 = ROPE_THETA ** (-jnp.arange(32, dtype=F32) / 32)
    f16 = ROPE_THETA ** (-jnp.arange(16, dtype=F32) / 16)
    freqs = jnp.stack([f32[lane % 32], f16[lane % 16]])
    signs = jnp.asarray(np.stack([np.where(lane % 64 < 32, -1.0, 1.0),
                                  np.where(lane % 32 < 16, -1.0, 1.0)]), F32)
    tab = jax.ShapeDtypeStruct((n, LANES), F32)
    row = pl.BlockSpec((tm, LANES), lambda i: (i, 0))
    par = pl.BlockSpec((2, LANES), lambda i: (0, 0))
    return pl.pallas_call(
        _rope_table_kernel, out_shape=(tab,) * 4, grid=(n // tm,),
        in_specs=[pl.BlockSpec((tm, 1), lambda i: (i, 0)), par, par],
        out_specs=(row,) * 4, compiler_params=_params(("arbitrary",)), name="rope_tables",
    )(pos_f, freqs, signs)


def _swap_half(y, half):
    lane = lax.broadcasted_iota(I32, y.shape, 1)
    return jnp.where((lane & half) == 0, pltpu.roll(y, LANES - half, 1), pltpu.roll(y, half, 1))


def _rope(y, cos, sin_signed, half):
    return y * cos + _swap_half(y, half) * sin_signed


def _seg_mean_sq(y, mseg):
    sq = y * y
    hi = sq.astype(BF16)
    lo = (sq - hi.astype(F32)).astype(BF16)
    return _mm(hi, mseg) + _mm(lo, mseg)


def _seg_norm(y, mseg, gain):
    return y * lax.rsqrt(_seg_mean_sq(y, mseg) + EPS) * gain


def _slab(s):
    return slice(s * LANES, (s + 1) * LANES)


def _proj_kernel(x_ref, g_ref, w_ref, wuq_ref, wukv_ref, gcq_ref, grows_ref, mseg_ref,
                 c64_ref, s64_ref, c32_ref, s32_ref,
                 qat_ref, kva_ref, vat_ref, iqt_ref, ik_ref, iwt_ref, qbt_ref, kb_ref, vbt_ref,
                 qc_ref, kc_ref, vc_ref, gt_ref):
    x = x_ref[...]
    h = (x * lax.rsqrt(jnp.mean(x * x, axis=-1, keepdims=True) + EPS) * g_ref[...]).astype(BF16)
    c64, s64, c32, s32 = c64_ref[...], s64_ref[...], c32_ref[...], s32_ref[...]
    m64, mqb = mseg_ref[0], mseg_ref[1]
    lane = lax.broadcasted_iota(I32, c64.shape, 1)
    in_rope = (lane >= B_NOPE) & (lane < B_NOPE + B_ROPE)
    cb = jnp.where(in_rope, c32, 1.0)
    sb = jnp.where(in_rope, s32, 0.0)

    p = _mm(h, w_ref[:, C_QA:C_CQ])
    for s in range(2):
        y = _rope(_seg_norm(p[:, _slab(s)], m64, grows_ref[0:1, :]), c64, s64, 32)
        qat_ref[0, _slab(s), :] = (y * (LOG2E * HEAD_DIM ** -0.5)).T.astype(BF16)
    y = p[:, _slab(2)]
    yr = _rope(_seg_norm(y, m64, grows_ref[1:2, :]), c64, s64, 32)
    kva_ref[...] = jnp.where(lane < HEAD_DIM, yr, y).astype(BF16)
    vat_ref[0] = y.T[HEAD_DIM:2 * HEAD_DIM, :].astype(BF16)
    for s in range(2):
        iqt_ref[0, _slab(s), :] = _rope(p[:, _slab(3 + s)], c32, s32, 16).T.astype(BF16)
    y = p[:, _slab(5)]
    ik_ref[...] = jnp.where(lane < IDX_DIM, _rope(y, c32, s32, 16), 0.0).astype(BF16)
    iwt_ref[0] = (y * (IDX_HEADS * IDX_DIM) ** -0.5).T[IDX_DIM:IDX_DIM + IDX_HEADS, :]

    p = _mm(h, w_ref[:, C_CQ:C_QC])
    cq = p[:, 0:256]
    cqn = cq * lax.rsqrt(jnp.mean(cq * cq, axis=-1, keepdims=True) + EPS) * gcq_ref[...]
    qb = _mm(cqn.astype(BF16), wuq_ref[...])
    for s in range(B_HEADS):
        y = _rope(_seg_norm(qb[:, _slab(s)], mqb, grows_ref[2:3, :]), cb, sb, 16)
        qbt_ref[0, _slab(s), :] = (y * (LOG2E * (B_NOPE + B_ROPE) ** -0.5)).T.astype(BF16)
    ckv = p[:, 256:384]
    ckvn = ckv * lax.rsqrt(jnp.mean(ckv * ckv, axis=-1, keepdims=True) + EPS) * grows_ref[7:8, :]
    kvb = _mm(ckvn.astype(BF16), wukv_ref[...])
    krs = p[:, 384:512]
    kr = krs * lax.rsqrt(jnp.sum(krs * krs, axis=-1, keepdims=True) * (1.0 / B_ROPE) + EPS) * grows_ref[4:5, :]
    kr = _rope(kr, cb, sb, 16)
    for s in range(B_HEADS):
        kb_ref[:, _slab(s)] = (_seg_norm(kvb[:, _slab(s)], m64, grows_ref[3:4, :]) + kr).astype(BF16)
    for s in range(2):
        vbt_ref[0, _slab(s), :] = kvb[:, 512 + s * LANES:512 + (s + 1) * LANES].T.astype(BF16)

    p = _mm(h, w_ref[:, C_QC:C_G])
    for s in range(4):
        y = _rope(_seg_norm(p[:, _slab(s)], m64, grows_ref[5:6, :]), c64, s64, 32)
        qc_ref[:, _slab(s)] = (y * HEAD_DIM ** -0.5).astype(BF16)
    y = _rope(_seg_norm(p[:, _slab(4)], m64, grows_ref[6:7, :]), c64, s64, 32)
    kc_ref[...] = y.astype(BF16)
    vc_ref[...] = p[:, _slab(5)].astype(BF16)

    for c in range(3):
        lo = C_G + c * 1024
        gt_ref[:, c * 1024:(c + 1) * 1024] = _sigmoid(_mm(h, w_ref[:, lo:lo + 1024]))


def _proj(xf, g, w_r, wuq, wukv, gcq, grows, mseg, tabs, tm=256):
    n, d = xf.shape
    row = lambda w: pl.BlockSpec((tm, w), lambda i: (i, 0))
    full = lambda a: pl.BlockSpec(a.shape, lambda i: (0,) * a.ndim)
    outs = [("t", 256, BF16), ("r", 128, BF16), ("t", HEAD_DIM, BF16), ("t", 256, BF16), ("r", 128, BF16),
            ("t", IDX_HEADS, F32), ("t", 512, BF16), ("r", 512, BF16), ("t", 256, BF16),
            ("r", 512, BF16), ("r", 128, BF16), ("r", 128, BF16), ("r", 3072, F32)]
    shape = lambda k, w, dt: jax.ShapeDtypeStruct((n // tm, w, tm) if k == "t" else (n, w), dt)
    spec = lambda k, w: pl.BlockSpec((1, w, tm), lambda i: (i, 0, 0)) if k == "t" else row(w)
    return pl.pallas_call(
        _proj_kernel,
        out_shape=tuple(shape(*o) for o in outs),
        grid=(n // tm,),
        in_specs=[row(d), full(g), full(w_r), full(wuq), full(wukv), full(gcq), full(grows), full(mseg)]
        + [row(LANES)] * 4,
        out_specs=tuple(spec(k, w) for k, w, _ in outs),
        compiler_params=_params(("arbitrary",)), name="in_proj",
    )(xf, g, w_r, wuq, wukv, gcq, grows, mseg, *tabs)


def _flash_logits(ks, qts):
    return tuple(_mm(k, qt) for k, qt in zip(ks, qts))


def _flash_update(ss, vts, carries, bias=None):
    if bias is not None:
        ss = [s + bias for s in ss]
    m_new = [jnp.maximum(c[0], jnp.max(s, axis=0, keepdims=True)) for c, s in zip(carries, ss)]
    ps = [jnp.exp2(s - m) for s, m in zip(ss, m_new)]
    pvs = [_mm(vt, p.astype(BF16)) for vt, p in zip(vts, ps)]
    out = []
    for (m, l, acc), mn, p, pv in zip(carries, m_new, ps, pvs):
        alpha = jnp.exp2(m - mn)
        out.append((mn, alpha * l + jnp.sum(p, axis=0, keepdims=True), alpha * acc + pv))
    return tuple(out)


def _flash_init(dv, tq):
    return jnp.full((1, tq), NEG_BIG, F32), jnp.zeros((1, tq), F32), jnp.zeros((dv, tq), F32)


def _flash_out(results):
    return jnp.concatenate([acc / l for _, l, acc in results], axis=0).T.astype(BF16)


def _dsa_kernel(iqt_ref, iwt_ref, qat_ref, ik_ref, kv_ref, vat_ref, o_ref, keys_ref, gmax_ref, *, tq, nsel, seq):
    ck = tq
    i = pl.program_id(1)
    nk = i + 1
    iqt = iqt_ref[0]
    iwt = iwt_ref[0]
    half = ck // 2
    qpos_h = lax.broadcasted_iota(I32, (half, tq), 1) + i * tq
    krow_h = lax.broadcasted_iota(I32, (half, tq), 0)
    krow = lax.broadcasted_iota(I32, (SUB, tq), 0)
    gmax_ref[...] = jnp.full(gmax_ref.shape, -jnp.inf, F32)

    def to_key(score):
        bits = lax.bitcast_convert_type(score, I32)
        return bits ^ ((bits >> 31) & 0x7FFFFFFF)

    def score_body(c, carry):
        for r in range(2):
            rows = pl.ds(pl.multiple_of(c * ck + r * half, half), half)
            ikc = ik_ref[rows, :][:, 0:IDX_DIM]
            acc = jnp.zeros((half, tq), F32)
            for h in range(IDX_HEADS):
                sc = _mm(ikc, iqt[h * IDX_DIM:(h + 1) * IDX_DIM, :])
                acc = acc + jnp.maximum(sc, 0.0) * iwt[h:h + 1, :]
            causal = krow_h + (c * ck + r * half) <= qpos_h
            keys_ref[c, r * half:(r + 1) * half, :] = jnp.where(causal, to_key(acc), INT_MIN)
            gmax_ref[r * half:(r + 1) * half, :] = jnp.maximum(gmax_ref[r * half:(r + 1) * half, :],
                                                               jnp.where(causal, acc, -jnp.inf))
        return carry

    lax.fori_loop(0, nk, score_body, 0)

    def rep(row):
        return jnp.broadcast_to(row, (SUB, tq))

    def count(pred, *ops):
        nacc = 4

        def body(c, accs):
            accs = list(accs)
            for g in range(ck // SUB):
                hit = pred(keys_ref[c, g * SUB:(g + 1) * SUB, :], c * ck + g * SUB, *ops)
                accs[g % nacc] = accs[g % nacc] + jnp.where(hit, 1.0, 0.0)
            return tuple(accs)
        accs = lax.fori_loop(0, nk, body, (jnp.zeros((SUB, tq), F32),) * nacc)
        return rep(jnp.sum(sum(accs[1:], accs[0]), axis=0, keepdims=True))

    gm = gmax_ref[...]
    smin = rep(jnp.min(gm, axis=0, keepdims=True))
    lo0 = jnp.where(smin == -jnp.inf, INT_MIN, to_key(smin))
    hi0 = to_key(rep(jnp.max(gm, axis=0, keepdims=True)))

    def bis_cond(state):
        return (state[0] < 34) & (state[3] < 0.5)

    def bis_body(state):
        it, lo, hi, _ = state
        all_done = jnp.min(jnp.where(lo == hi, 1.0, 0.0))
        gap = hi - lo
        mid = lo + lax.shift_right_logical(gap, 1) + (gap & 1)
        cnt = count(lambda k, base, t: k >= t, mid)
        lo = jnp.where(cnt >= nsel, mid, lo)
        hi = jnp.where(cnt > nsel, hi, jnp.where(cnt == nsel, mid, mid - 1))
        return it + 1, lo, hi, all_done

    _, thr, _, _ = lax.while_loop(bis_cond, bis_body, (jnp.int32(0), lo0, hi0, jnp.float32(0.0)))

    need = nsel - count(lambda k, base, t: k > t, thr)
    excess = (count(lambda k, base, t: k == t, thr) > need) & (thr != INT_MIN)
    last0 = jnp.where(thr == INT_MIN, -1, seq).astype(I32)
    nbits = seq.bit_length() - 1

    def tie_fix(_):
        def body(it, p):
            cand = p + jnp.left_shift(jnp.int32(1), nbits - 1 - it)
            ties_before = count(lambda k, base, t, cd: (k == t) & (krow + base < cd), thr, cand)
            return jnp.where(ties_before < need, cand, p)
        p = lax.fori_loop(0, nbits, body, jnp.zeros((SUB, tq), I32))
        return jnp.where(excess, p, last0)

    last = lax.cond(jnp.max(jnp.where(excess, 1.0, 0.0)) > 0.0, tie_fix, lambda _: last0, 0)

    qat = qat_ref[0]
    thr_t = jnp.concatenate([thr] * (ck // SUB), axis=0)
    last_t = jnp.concatenate([last] * (ck // SUB), axis=0)
    kpos_t = lax.broadcasted_iota(I32, (ck, tq), 0)

    qts = [qat[h * HEAD_DIM:(h + 1) * HEAD_DIM, :] for h in range(A_HEADS)]

    def step(c, width, carry):
        k = kv_ref[pl.ds(pl.multiple_of(c * ck, ck), width * ck), :][:, 0:HEAD_DIM]
        vt = jnp.concatenate([vat_ref[c + j] for j in range(width)], axis=1)
        bias = []
        for j in range(width):
            key = keys_ref[c + j]
            sel = (key > thr_t) | ((key == thr_t) & (kpos_t + (c + j) * ck <= last_t))
            bias.append(jnp.where(sel, 0.0, NEG_BIG))
        ss = _flash_logits([k] * A_HEADS, qts)
        return _flash_update(ss, [vt] * A_HEADS, carry, jnp.concatenate(bias, axis=0))

    carry = (_flash_init(HEAD_DIM, tq),) * A_HEADS
    carry = lax.fori_loop(0, nk // 2, lambda c2, cr: step(2 * c2, 2, cr), carry)
    carry = lax.fori_loop((nk // 2) * 2, nk, lambda c, cr: step(c, 1, cr), carry)
    o_ref[...] = _flash_out(carry)


def _dsa(iqt, iwt, qat, ik, kva, vat, b, s, tq=256):
    n = b * s
    nq = s // tq
    nsel = min(TOPK_MAX, s // 4)
    assert nsel <= tq and iqt.shape[2] == tq
    qtile = lambda w: pl.BlockSpec((1, w, tq), lambda bi, i: (bi * nq + i, 0, 0))
    seq_rows = pl.BlockSpec((s, LANES), lambda bi, i: (bi, 0))
    return pl.pallas_call(
        functools.partial(_dsa_kernel, tq=tq, nsel=nsel, seq=s),
        out_shape=jax.ShapeDtypeStruct((n, A_HEADS * HEAD_DIM), BF16), grid=(b, nq),
        in_specs=[qtile(IDX_HEADS * IDX_DIM), qtile(IDX_HEADS), qtile(A_HEADS * HEAD_DIM), seq_rows, seq_rows,
                  pl.BlockSpec((nq, HEAD_DIM, tq), lambda bi, i: (bi, 0, 0))],
        out_specs=pl.BlockSpec((tq, A_HEADS * HEAD_DIM), lambda bi, i: (bi * nq + i, 0)),
        scratch_shapes=[pltpu.VMEM((nq, tq, tq), I32), pltpu.VMEM((tq, tq), F32)],
        compiler_params=_params(("arbitrary", "arbitrary")), name="dsa_attention",
    )(iqt, iwt, qat, ik, kva, vat)


def _mla_kernel(qt_ref, k_ref, vt_ref, o_ref, *, tq):
    i = pl.program_id(1)
    qt = qt_ref[0]
    kpos = lax.broadcasted_iota(I32, (tq, tq), 0)
    qpos = lax.broadcasted_iota(I32, (tq, tq), 1)
    diag_bias = jnp.where(kpos <= qpos, 0.0, NEG_BIG)

    heads = range(B_HEADS)
    qts = [qt[_slab(h), :] for h in heads]

    def step(c, width, carry, bias=None):
        rows = pl.ds(pl.multiple_of(c * tq, tq), width * tq)
        vt = jnp.concatenate([vt_ref[c + j] for j in range(width)], axis=1)
        ss = _flash_logits([k_ref[rows, _slab(h)] for h in heads], qts)
        return _flash_update(ss, [vt[h * B_V:(h + 1) * B_V, :] for h in heads], carry, bias)

    carry = (_flash_init(B_V, tq),) * B_HEADS
    carry = lax.fori_loop(0, i // 2, lambda c2, cr: step(2 * c2, 2, cr), carry)
    carry = lax.fori_loop((i // 2) * 2, i, lambda c, cr: step(c, 1, cr), carry)
    o_ref[...] = _flash_out(step(i, 1, carry, diag_bias))


def _mla(qbt, kb, vbt, b, s, tq=256):
    n = b * s
    nq = s // tq
    assert qbt.shape[2] == tq
    return pl.pallas_call(
        functools.partial(_mla_kernel, tq=tq),
        out_shape=jax.ShapeDtypeStruct((n, B_HEADS * B_V), BF16), grid=(b, nq),
        in_specs=[pl.BlockSpec((1, B_HEADS * LANES, tq), lambda bi, i: (bi * nq + i, 0, 0)),
                  pl.BlockSpec((s, B_HEADS * LANES), lambda bi, i: (bi, 0)),
                  pl.BlockSpec((nq, B_HEADS * B_V, tq), lambda bi, i: (bi, 0, 0))],
        out_specs=pl.BlockSpec((tq, B_HEADS * B_V), lambda bi, i: (bi * nq + i, 0)),
        compiler_params=_params(("arbitrary", "arbitrary")), name="mla_attention",
    )(qbt, kb, vbt)


def _swa_kernel(sink_ref, q_ref, kp_ref, kc_ref, vp_ref, vc_ref, o_ref, *, tq):
    i = pl.program_id(1)
    q = q_ref[...]
    keys = jnp.concatenate([kp_ref[...], kc_ref[...]], axis=0)
    vals = jnp.concatenate([vp_ref[...], vc_ref[...]], axis=0)
    nkeys = WINDOW + tq
    qpos = lax.broadcasted_iota(I32, (tq, nkeys), 0) + i * tq
    kpos = lax.broadcasted_iota(I32, (tq, nkeys), 1) + i * tq - WINDOW
    visible = (kpos > qpos - WINDOW) & (kpos <= qpos) & (kpos >= 0)
    bias = jnp.where(visible, 0.0, NEG_BIG)
    group = C_HEADS // C_KV_HEADS
    outs = []
    for h in range(C_HEADS):
        kvh = h // group
        k = keys[:, kvh * HEAD_DIM:(kvh + 1) * HEAD_DIM]
        v = vals[:, kvh * HEAD_DIM:(kvh + 1) * HEAD_DIM]
        s = _mm_nt(q[:, h * HEAD_DIM:(h + 1) * HEAD_DIM], k) + bias
        sink = sink_ref[h]
        m = jnp.maximum(jnp.max(s, axis=1, keepdims=True), sink)
        p = jnp.exp(s - m)
        denom = jnp.sum(p, axis=1, keepdims=True) + jnp.exp(sink - m)
        outs.append(_mm((p / denom).astype(BF16), v))
    o_ref[...] = jnp.concatenate(outs, axis=1).astype(BF16)


def _swa(sinks, qc, kc, vc, b, s, tq=256):
    n = b * s
    nq = s // tq
    per = tq // WINDOW
    qrow = lambda w: pl.BlockSpec((tq, w), lambda bi, i: (bi * nq + i, 0))
    prev = pl.BlockSpec((WINDOW, LANES), lambda bi, i: (jnp.maximum((bi * nq + i) * per - 1, 0), 0))
    return pl.pallas_call(
        functools.partial(_swa_kernel, tq=tq),
        out_shape=jax.ShapeDtypeStruct((n, C_HEADS * HEAD_DIM), BF16), grid=(b, nq),
        in_specs=[pl.BlockSpec(memory_space=pltpu.SMEM), qrow(512), prev, qrow(LANES), prev, qrow(LANES)],
        out_specs=qrow(512),
        compiler_params=_params(("arbitrary", "arbitrary")), name="swa_attention",
    )(sinks, qc, kc, kc, vc, vc)


def _merge_kernel(ya_ref, yb_ref, yc_ref, gt_ref, x_ref, wa_ref, wb_ref, wc_ref, wo_ref, g_ref,
                  xo_ref, h_ref):
    d = x_ref.shape[1]
    merged = (gt_ref[:, 0:d] * _mm(ya_ref[...], wa_ref[...])
              + gt_ref[:, d:2 * d] * _mm(yb_ref[...], wb_ref[...])
              + gt_ref[:, 2 * d:3 * d] * _mm(yc_ref[...], wc_ref[...]))
    xn = x_ref[...] + _mm(merged.astype(BF16), wo_ref[...])
    xo_ref[...] = xn
    h = xn * lax.rsqrt(jnp.mean(xn * xn, axis=-1, keepdims=True) + EPS) * g_ref[...]
    h_ref[...] = h.astype(h_ref.dtype)


def _merge(ya, yb, yc, gt, xf, wa, wb, wc, wo, g, h_dtype, tm=256):
    n, d = xf.shape
    row = lambda w: pl.BlockSpec((tm, w), lambda i: (i, 0))
    full = lambda a: pl.BlockSpec(a.shape, lambda i: (0,) * a.ndim)
    return pl.pallas_call(
        _merge_kernel,
        out_shape=(jax.ShapeDtypeStruct((n, d), F32), jax.ShapeDtypeStruct((n, d), h_dtype)),
        grid=(n // tm,),
        in_specs=[row(256), row(256), row(512), row(3 * d), row(d), full(wa), full(wb), full(wc), full(wo),
                  full(g)],
        out_specs=(row(d), row(d)),
        compiler_params=_params(("arbitrary",)), name="merge_out_proj",
    )(ya, yb, yc, gt, xf, wa, wb, wc, wo, g)


def _ffn_kernel(h_ref, x_ref, w1_hbm, w3_hbm, w2_hbm, o_ref, w1_ref, w3_ref, w2_ref, sem, *, chunk):
    @pl.when(pl.program_id(0) == 0)
    def _():
        copies = [pltpu.make_async_copy(src, dst, sem.at[j]) for j, (src, dst) in
                  enumerate(((w1_hbm, w1_ref), (w3_hbm, w3_ref), (w2_hbm, w2_ref)))]
        for cp in copies:
            cp.start()
        for cp in copies:
            cp.wait()

    h = h_ref[...]
    acc = x_ref[...]
    for j in range(w1_ref.shape[1] // chunk):
        cols = slice(j * chunk, (j + 1) * chunk)
        a = _mm(h, w1_ref[:, cols])
        mid = a * _sigmoid(a) * _mm(h, w3_ref[:, cols])
        acc = acc + _mm(mid.astype(BF16), w2_ref[cols, :])
    o_ref[...] = acc


def _ffn(h2, xf, w1, w3, w2, tm=256, chunk=1408):
    n, d = xf.shape
    row = lambda w: pl.BlockSpec((tm, w), lambda i: (i, 0))
    hbm = pl.BlockSpec(memory_space=pl.ANY)
    return pl.pallas_call(
        functools.partial(_ffn_kernel, chunk=chunk),
        out_shape=jax.ShapeDtypeStruct((n, d), F32), grid=(n // tm,),
        in_specs=[row(d), row(d), hbm, hbm, hbm], out_specs=row(d),
        scratch_shapes=[pltpu.VMEM(w1.shape, BF16), pltpu.VMEM(w3.shape, BF16), pltpu.VMEM(w2.shape, BF16),
                        pltpu.SemaphoreType.DMA((3,))],
        compiler_params=_params(("arbitrary",)), name="dense_swiglu",
    )(h2, xf, w1, w3, w2)


def _route_kernel(h_ref, wr_ref, tri_ref, route_ref, cnt_ref, carry_ref):
    @pl.when(pl.program_id(0) == 0)
    def _():
        carry_ref[...] = jnp.zeros_like(carry_ref)

    logits = jnp.dot(h_ref[...], wr_ref[...], precision=lax.Precision.HIGHEST, preferred_element_type=F32)
    lane = lax.broadcasted_iota(I32, logits.shape, 1).astype(F32)
    lg = jnp.where(lane < N_EXPERTS, logits, -jnp.inf)
    m1 = jnp.max(lg, axis=1, keepdims=True)
    e1 = jnp.min(jnp.where(lg == m1, lane, float(LANES)), axis=1, keepdims=True)
    lg2 = jnp.where(lane == e1, -jnp.inf, lg)
    m2 = jnp.max(lg2, axis=1, keepdims=True)
    e2 = jnp.min(jnp.where(lg2 == m2, lane, float(LANES)), axis=1, keepdims=True)
    ex = jnp.exp(m2 - m1)
    g1 = 1.0 / (1.0 + ex)
    g2 = ex / (1.0 + ex)
    onehot = jnp.where((lane == e1) | (lane == e2), 1.0, 0.0)
    before = _mm(tri_ref[...], onehot.astype(BF16)) + carry_ref[0:1, :]
    r1 = jnp.sum(jnp.where(lane == e1, before, 0.0), axis=1, keepdims=True)
    r2 = jnp.sum(jnp.where(lane == e2, before, 0.0), axis=1, keepdims=True)
    out = jnp.zeros_like(logits)
    for idx, val in enumerate((e1, e2, g1, g2, r1, r2)):
        out = jnp.where(lane == idx, val, out)
    route_ref[...] = out
    total = carry_ref[0:1, :] + jnp.sum(onehot, axis=0, keepdims=True)
    carry_ref[...] = jnp.broadcast_to(total, carry_ref.shape)
    cnt_ref[...] = jnp.broadcast_to(total, cnt_ref.shape)


def _route(h2, wr, tm=256):
    n, d = h2.shape
    tri = jnp.asarray(np.tril(np.ones((tm, tm), np.float32), -1), BF16)
    full = lambda a: pl.BlockSpec(a.shape, lambda i: (0,) * a.ndim)
    return pl.pallas_call(
        _route_kernel,
        out_shape=(jax.ShapeDtypeStruct((n, LANES), F32), jax.ShapeDtypeStruct((8, LANES), F32)),
        grid=(n // tm,),
        in_specs=[pl.BlockSpec((tm, d), lambda i: (i, 0)), full(wr), full(tri)],
        out_specs=(pl.BlockSpec((tm, LANES), lambda i: (i, 0)), pl.BlockSpec((8, LANES), lambda i: (0, 0))),
        scratch_shapes=[pltpu.VMEM((8, LANES), F32)],
        compiler_params=_params(("arbitrary",)), name="moe_route",
    )(h2, wr, tri)


def _scatter_kernel(dest_ref, h_ref, xin_hbm, xout_hbm, sem, *, tt):
    del xin_hbm

    def copy(r, k):
        d = dest_ref[0, 0, 2 * r + k]
        return pltpu.make_async_copy(h_ref.at[pl.ds(r, 1), :], xout_hbm.at[pl.ds(d, 1), :], sem)

    def start(r, carry):
        copy(r, 0).start()
        copy(r, 1).start()
        return carry

    def wait(r, carry):
        copy(r, 0).wait()
        copy(r, 1).wait()
        return carry

    lax.fori_loop(0, tt, start, 0)
    lax.fori_loop(0, tt, wait, 0)


def _scatter(dest3, h2, rows, tt):
    n, w = h2.shape
    hbm = pl.BlockSpec(memory_space=pl.ANY)
    return pl.pallas_call(
        functools.partial(_scatter_kernel, tt=tt),
        out_shape=jax.ShapeDtypeStruct((rows, w), h2.dtype), grid=(n // tt,),
        in_specs=[pl.BlockSpec((1, 1, 2 * tt), lambda i: (i, 0, 0), memory_space=pltpu.SMEM),
                  pl.BlockSpec((tt, w), lambda i: (i, 0)), hbm],
        out_specs=hbm, scratch_shapes=[pltpu.SemaphoreType.DMA(())],
        input_output_aliases={2: 0},
        compiler_params=_params(("arbitrary",)), name="moe_scatter",
    )(dest3, h2, jnp.zeros((rows, w), h2.dtype))


def _expert_kernel(be_ref, nv_ref, x_ref, w1_hbm, w3_hbm, w2_hbm, y_ref, w1_ref, w3_ref, w2_ref, sem, *, chunk):
    blk = pl.program_id(0)
    e = be_ref[blk]

    @pl.when((blk == 0) | (e != be_ref[jnp.maximum(blk - 1, 0)]))
    def _():
        copies = [pltpu.make_async_copy(src.at[e], dst, sem.at[j]) for j, (src, dst) in
                  enumerate(((w1_hbm, w1_ref), (w3_hbm, w3_ref), (w2_hbm, w2_ref)))]
        for cp in copies:
            cp.start()
        for cp in copies:
            cp.wait()

    @pl.when(blk < nv_ref[0])
    def _():
        x = x_ref[...].astype(BF16)
        acc = jnp.zeros(y_ref.shape, F32)
        for j in range(w1_ref.shape[1] // chunk):
            cols = slice(j * chunk, (j + 1) * chunk)
            a = _mm(x, w1_ref[:, cols])
            mid = a * _sigmoid(a) * _mm(x, w3_ref[:, cols])
            acc = acc + _mm(mid.astype(BF16), w2_ref[cols, :])
        y_ref[...] = acc

    @pl.when(blk >= nv_ref[0])
    def _():
        y_ref[...] = jnp.zeros(y_ref.shape, F32)


def _experts(block_e, nvalid, xbuf, w1, w3, w2, chunk=512):
    rows, d = xbuf.shape
    hbm = pl.BlockSpec(memory_space=pl.ANY)
    blk = pl.BlockSpec((MOE_BLOCK, d), lambda i, be, nv: (i, 0))
    return pl.pallas_call(
        functools.partial(_expert_kernel, chunk=chunk),
        out_shape=jax.ShapeDtypeStruct((rows, d), F32),
        grid_spec=pltpu.PrefetchScalarGridSpec(
            num_scalar_prefetch=2, grid=(rows // MOE_BLOCK,),
            in_specs=[blk, hbm, hbm, hbm], out_specs=blk,
            scratch_shapes=[pltpu.VMEM(w1.shape[1:], BF16), pltpu.VMEM(w3.shape[1:], BF16),
                            pltpu.VMEM(w2.shape[1:], BF16), pltpu.SemaphoreType.DMA((3,))]),
        compiler_params=_params(("arbitrary",)), name="moe_experts",
    )(block_e, nvalid, xbuf, w1, w3, w2)


def _combine_kernel(dest_ref, y_hbm, x_ref, route_ref, o_ref, buf_ref, sem, *, tt):
    def copy(r, k):
        d = dest_ref[0, 0, 2 * r + k]
        return pltpu.make_async_copy(y_hbm.at[pl.ds(d, 1), :], buf_ref.at[k, pl.ds(r, 1), :], sem)

    def start(r, carry):
        copy(r, 0).start()
        copy(r, 1).start()
        return carry

    def wait(r, carry):
        copy(r, 0).wait()
        copy(r, 1).wait()
        return carry

    lax.fori_loop(0, tt, start, 0)
    lax.fori_loop(0, tt, wait, 0)
    rt = route_ref[...]
    o_ref[...] = x_ref[...] + (buf_ref[0] * rt[:, 2:3] + buf_ref[1] * rt[:, 3:4])


def _combine(dest3, ybuf, xf, route, tt):
    n, d = xf.shape
    row = lambda w: pl.BlockSpec((tt, w), lambda i: (i, 0))
    return pl.pallas_call(
        functools.partial(_combine_kernel, tt=tt),
        out_shape=jax.ShapeDtypeStruct((n, d), F32), grid=(n // tt,),
        in_specs=[pl.BlockSpec((1, 1, 2 * tt), lambda i: (i, 0, 0), memory_space=pltpu.SMEM),
                  pl.BlockSpec(memory_space=pl.ANY), row(d), row(LANES)],
        out_specs=row(d),
        scratch_shapes=[pltpu.VMEM((2, tt, d), F32), pltpu.SemaphoreType.DMA(())],
        compiler_params=_params(("arbitrary",)), name="moe_combine",
    )(dest3, ybuf, xf, route)


def _moe(xf, h2, w_router, w1, w3, w2, tt=256):
    n, d = xf.shape
    wr = jnp.pad(w_router, ((0, 0), (0, LANES - N_EXPERTS)))
    route, cnt = _route(h2, wr)
    counts = cnt[0, :N_EXPERTS].astype(I32)
    padded = (counts + MOE_BLOCK - 1) // MOE_BLOCK * MOE_BLOCK
    pend = jnp.cumsum(padded)
    pstart = pend - padded
    dest = pstart[route[:, 0:2].astype(I32)] + route[:, 4:6].astype(I32)
    dest3 = dest.reshape(n // tt, 1, 2 * tt)
    n_blocks = n * 2 // MOE_BLOCK + N_EXPERTS
    rows = n_blocks * MOE_BLOCK
    block_e = jnp.minimum(jnp.searchsorted(pend, jnp.arange(n_blocks) * MOE_BLOCK, side='right'),
                          N_EXPERTS - 1).astype(I32)
    nvalid = (pend[-1:] // MOE_BLOCK).astype(I32)
    xbuf = _scatter(dest3, h2, rows, tt)
    ybuf = _experts(block_e, nvalid, xbuf, w1, w3, w2)
    return _combine(dest3, ybuf, xf, route, tt)


def _tile_row(v, width=LANES):
    v = v.astype(F32)
    return jnp.tile(v, width // v.shape[0])


def _layer_params(l, attn_norm_g, w_in, a_q_norm_g, a_k_norm_g, b_cq_norm_g, b_ckv_norm_g, b_w_uq, b_w_ukv,
                  b_qn_g, b_qr_g, b_kn_g, b_kr_g, c_q_norm_g, c_k_norm_g):
    d = w_in.shape[1]
    sizes = (A_HEADS * HEAD_DIM, HEAD_DIM, HEAD_DIM, IDX_HEADS * IDX_DIM, IDX_DIM, IDX_HEADS,
             b_w_uq.shape[1], b_w_ukv.shape[1], B_ROPE, C_HEADS * HEAD_DIM, C_KV_HEADS * HEAD_DIM,
             C_KV_HEADS * HEAD_DIM, 3 * d)
    qa, ka, va, iq, ik, iw, cq, ckv, kr, qc, kc, vc, gates = jnp.split(w_in[l], np.cumsum(sizes)[:-1].tolist(), axis=1)
    z = lambda k: jnp.zeros((d, k), F32)
    w_r = jnp.concatenate([qa, ka, va, iq, ik, iw, z(LANES - IDX_DIM - IDX_HEADS), cq, ckv,
                           z(B_NOPE), kr, z(LANES - B_NOPE - B_ROPE), qc, kc, vc, gates], axis=1).astype(BF16)
    uq = b_w_uq[l].reshape(-1, B_HEADS, B_NOPE + B_ROPE)
    uq = jnp.pad(uq, ((0, 0), (0, 0), (0, LANES - B_NOPE - B_ROPE))).reshape(-1, B_HEADS * LANES).astype(BF16)
    ukv = b_w_ukv[l].reshape(-1, B_HEADS, B_NOPE + B_V)
    uk = jnp.pad(ukv[:, :, :B_NOPE], ((0, 0), (0, 0), (0, LANES - B_NOPE))).reshape(-1, B_HEADS * LANES)
    uv = ukv[:, :, B_NOPE:].reshape(-1, B_HEADS * B_V)
    wukv = jnp.concatenate([uk, uv], axis=1).astype(BF16)
    zeros = lambda k: jnp.zeros((k,), F32)
    grows = jnp.stack([
        _tile_row(a_q_norm_g[l]),
        jnp.concatenate([a_k_norm_g[l], jnp.ones((HEAD_DIM,), F32)]),
        jnp.concatenate([b_qn_g[l], b_qr_g[l], zeros(LANES - B_NOPE - B_ROPE)]),
        jnp.concatenate([b_kn_g[l], zeros(LANES - B_NOPE)]),
        jnp.concatenate([zeros(B_NOPE), b_kr_g[l], zeros(LANES - B_NOPE - B_ROPE)]),
        _tile_row(c_q_norm_g[l]),
        _tile_row(c_k_norm_g[l]),
        b_ckv_norm_g[l],
    ]).astype(F32)
    return (attn_norm_g[l][None, :], w_r, uq, wukv, b_cq_norm_g[l][None, :], grows)


def _seg_matrices():
    lane = np.arange(LANES)
    g64 = lane // 64
    m64 = (g64[:, None] == g64[None, :]) / 64.0
    gqb = np.where(lane < B_NOPE, 0, np.where(lane < B_NOPE + B_ROPE, 1, 2))
    size = np.where(lane < B_NOPE, B_NOPE, B_ROPE)
    mqb = (gqb[:, None] == gqb[None, :]) / size[None, :]
    return jnp.asarray(np.stack([m64, mqb]), BF16)


def kernel(x, positions, attn_norm_g, w_in, a_q_norm_g, a_k_norm_g, b_cq_norm_g, b_ckv_norm_g, b_w_uq, b_w_ukv,
           b_qn_g, b_qr_g, b_kn_g, b_kr_g, c_q_norm_g, c_k_norm_g, c_sinks, w_a_out, w_b_out, w_c_out, w_o,
           ffn_norm_g, ffn_w1, ffn_w3, ffn_w2, router_w, moe_w1, moe_w3, moe_w2):
    b, s, d = x.shape
    n = b * s
    depth = w_in.shape[0]
    xf = x.reshape(n, d)
    tabs = _rope_tables(positions.reshape(n, 1).astype(F32))
    mseg = _seg_matrices()
    for l in range(depth):
        g, w_r, uq, wukv, gcq, grows = _layer_params(
            l, attn_norm_g, w_in, a_q_norm_g, a_k_norm_g, b_cq_norm_g, b_ckv_norm_g, b_w_uq, b_w_ukv,
            b_qn_g, b_qr_g, b_kn_g, b_kr_g, c_q_norm_g, c_k_norm_g)
        qat, kva, vat, iqt, ik, iwt, qbt, kb, vbt, qc, kc, vc, gt = _proj(xf, g, w_r, uq, wukv, gcq, grows, mseg,
                                                                         tabs)
        ya = _dsa(iqt, iwt, qat, ik, kva, vat, b, s)
        yb = _mla(qbt, kb, vbt, b, s)
        yc = _swa(c_sinks[l], qc, kc, vc, b, s)
        dense = l % 2 == 0
        xf, h2 = _merge(ya, yb, yc, gt, xf, w_a_out[l].astype(BF16), w_b_out[l].astype(BF16),
                        w_c_out[l].astype(BF16), w_o[l].astype(BF16), ffn_norm_g[l][None, :],
                        BF16 if dense else F32)
        if dense:
            xf = _ffn(h2, xf, ffn_w1[l // 2].astype(BF16), ffn_w3[l // 2].astype(BF16), ffn_w2[l // 2].astype(BF16))
        else:
            xf = _moe(xf, h2, router_w[l // 2], moe_w1[l // 2].astype(BF16),
                      moe_w3[l // 2].astype(BF16), moe_w2[l // 2].astype(BF16))
    return xf.reshape(b, s, d)
```

```python
import functools

import numpy as np
import jax
import jax.numpy as jnp
from jax import lax
from jax.experimental import pallas as pl
from jax.experimental.pallas import tpu as pltpu

F32, BF16, I32 = jnp.float32, jnp.bfloat16, jnp.int32

EPS = 1e-6
ROPE_THETA = 10000.0
HEAD_DIM = 64
A_HEADS = 4
IDX_HEADS = 8
IDX_DIM = 32
TOPK_MAX = 256
B_HEADS = 4
B_NOPE = 64
B_ROPE = 32
B_V = 64
C_HEADS = 8
C_KV_HEADS = 2
WINDOW = 128
N_EXPERTS = 8
MOE_BLOCK = 256

LANES = 128
SUB = 8
VMEM_LIMIT = 56 * 1024 * 1024
INT_MIN = -2 ** 31
NEG_BIG = -1e30
LOG2E = 1.4426950408889634

C_QA, C_KVA, C_IQ, C_IKW, C_CQ, C_CKV, C_KR, C_QC, C_KC, C_VC, C_G = (
    0, 256, 384, 640, 768, 1024, 1152, 1280, 1792, 1920, 2048)


def _params(sem):
    return pltpu.CompilerParams(dimension_semantics=sem, vmem_limit_bytes=VMEM_LIMIT)


def _mm(a, b):
    return jnp.dot(a, b, preferred_element_type=F32)


def _mm_nt(a, b):
    return lax.dot_general(a, b, (((1,), (1,)), ((), ())), preferred_element_type=F32)


def _sigmoid(v):
    return 1.0 / (1.0 + jnp.exp(-v))


def _rope_table_kernel(pos_ref, f_ref, sg_ref, c64_ref, s64_ref, c32_ref, s32_ref):
    pos = pos_ref[...]
    a = pos * f_ref[0:1, :]
    c64_ref[...] = jnp.cos(a)
    s64_ref[...] = jnp.sin(a) * sg_ref[0:1, :]
    a = pos * f_ref[1:2, :]
    c32_ref[...] = jnp.cos(a)
    s32_ref[...] = jnp.sin(a) * sg_ref[1:2, :]


def _rope_tables(pos_f, tm=1024):
    n = pos_f.shape[0]
    lane = np.arange(LANES)
    f32 = ROPE_THETA ** (-jnp.arange(32, dtype=F32) / 32)
    f16 = ROPE_THETA ** (-jnp.arange(16, dtype=F32) / 16)
    freqs = jnp.stack([f32[lane % 32], f16[lane % 16]])
    signs = jnp.asarray(np.stack([np.where(lane % 64 < 32, -1.0, 1.0),
                                  np.where(lane % 32 < 16, -1.0, 1.0)]), F32)
    tab = jax.ShapeDtypeStruct((n, LANES), F32)
    row = pl.BlockSpec((tm, LANES), lambda i: (i, 0))
    par = pl.BlockSpec((2, LANES), lambda i: (0, 0))
    return pl.pallas_call(
        _rope_table_kernel, out_shape=(tab,) * 4, grid=(n // tm,),
        in_specs=[pl.BlockSpec((tm, 1), lambda i: (i, 0)), par, par],
        out_specs=(row,) * 4, compiler_params=_params(("arbitrary",)), name="rope_tables",
    )(pos_f, freqs, signs)


def _swap_half(y, half):
    lane = lax.broadcasted_iota(I32, y.shape, 1)
    return jnp.where((lane & half) == 0, pltpu.roll(y, LANES - half, 1), pltpu.roll(y, half, 1))


def _rope(y, cos, sin_signed, half):
    return y * cos + _swap_half(y, half) * sin_signed


def _seg_mean_sq(y, mseg):
    sq = y * y
    hi = sq.astype(BF16)
    lo = (sq - hi.astype(F32)).astype(BF16)
    return _mm(hi, mseg) + _mm(lo, mseg)


def _seg_norm(y, mseg, gain):
    return y * lax.rsqrt(_seg_mean_sq(y, mseg) + EPS) * gain


def _slab(s):
    return slice(s * LANES, (s + 1) * LANES)


def _proj_kernel(x_ref, g_ref, w_ref, wuq_ref, wukv_ref, gcq_ref, grows_ref, mseg_ref,
                 c64_ref, s64_ref, c32_ref, s32_ref,
                 qat_ref, kva_ref, vat_ref, iqt_ref, ik_ref, iwt_ref, qbt_ref, kb_ref, vbt_ref,
                 qc_ref, kc_ref, vc_ref, gt_ref):
    x = x_ref[...]
    h = (x * lax.rsqrt(jnp.mean(x * x, axis=-1, keepdims=True) + EPS) * g_ref[...]).astype(BF16)
    c64, s64, c32, s32 = c64_ref[...], s64_ref[...], c32_ref[...], s32_ref[...]
    m64, mqb = mseg_ref[0], mseg_ref[1]
    lane = lax.broadcasted_iota(I32, c64.shape, 1)
    in_rope = (lane >= B_NOPE) & (lane < B_NOPE + B_ROPE)
    cb = jnp.where(in_rope, c32, 1.0)
    sb = jnp.where(in_rope, s32, 0.0)

    p = _mm(h, w_ref[:, C_QA:C_CQ])
    for s in range(2):
        y = _rope(_seg_norm(p[:, _slab(s)], m64, grows_ref[0:1, :]), c64, s64, 32)
        qat_ref[0, _slab(s), :] = (y * (LOG2E * HEAD_DIM ** -0.5)).T.astype(BF16)
    y = p[:, _slab(2)]
    yr = _rope(_seg_norm(y, m64, grows_ref[1:2, :]), c64, s64, 32)
    kva_ref[...] = jnp.where(lane < HEAD_DIM, yr, y).astype(BF16)
    vat_ref[0] = y.T[HEAD_DIM:2 * HEAD_DIM, :].astype(BF16)
    for s in range(2):
        iqt_ref[0, _slab(s), :] = _rope(p[:, _slab(3 + s)], c32, s32, 16).T.astype(BF16)
    y = p[:, _slab(5)]
    ik_ref[...] = jnp.where(lane < IDX_DIM, _rope(y, c32, s32, 16), 0.0).astype(BF16)
    iwt_ref[0] = (y * (IDX_HEADS * IDX_DIM) ** -0.5).T[IDX_DIM:IDX_DIM + IDX_HEADS, :]

    p = _mm(h, w_ref[:, C_CQ:C_QC])
    cq = p[:, 0:256]
    cqn = cq * lax.rsqrt(jnp.mean(cq * cq, axis=-1, keepdims=True) + EPS) * gcq_ref[...]
    qb = _mm(cqn.astype(BF16), wuq_ref[...])
    for s in range(B_HEADS):
        y = _rope(_seg_norm(qb[:, _slab(s)], mqb, grows_ref[2:3, :]), cb, sb, 16)
        qbt_ref[0, _slab(s), :] = (y * (LOG2E * (B_NOPE + B_ROPE) ** -0.5)).T.astype(BF16)
    ckv = p[:, 256:384]
    ckvn = ckv * lax.rsqrt(jnp.mean(ckv * ckv, axis=-1, keepdims=True) + EPS) * grows_ref[7:8, :]
    kvb = _mm(ckvn.astype(BF16), wukv_ref[...])
    krs = p[:, 384:512]
    kr = krs * lax.rsqrt(jnp.sum(krs * krs, axis=-1, keepdims=True) * (1.0 / B_ROPE) + EPS) * grows_ref[4:5, :]
    kr = _rope(kr, cb, sb, 16)
    for s in range(B_HEADS):
        kb_ref[:, _slab(s)] = (_seg_norm(kvb[:, _slab(s)], m64, grows_ref[3:4, :]) + kr).astype(BF16)
    for s in range(2):
        vbt_ref[0, _slab(s), :] = kvb[:, 512 + s * LANES:512 + (s + 1) * LANES].T.astype(BF16)

    p = _mm(h, w_ref[:, C_QC:C_G])
    for s in range(4):
        y = _rope(_seg_norm(p[:, _slab(s)], m64, grows_ref[5:6, :]), c64, s64, 32)
        qc_ref[:, _slab(s)] = (y * HEAD_DIM ** -0.5).astype(BF16)
    y = _rope(_seg_norm(p[:, _slab(4)], m64, grows_ref[6:7, :]), c64, s64, 32)
    kc_ref[...] = y.astype(BF16)
    vc_ref[...] = p[:, _slab(5)].astype(BF16)

    for c in range(3):
        lo = C_G + c * 1024
        gt_ref[:, c * 1024:(c + 1) * 1024] = _sigmoid(_mm(h, w_ref[:, lo:lo + 1024]))


def _proj(xf, g, w_r, wuq, wukv, gcq, grows, mseg, tabs, tm=256):
    n, d = xf.shape
    row = lambda w: pl.BlockSpec((tm, w), lambda i: (i, 0))
    full = lambda a: pl.BlockSpec(a.shape, lambda i: (0,) * a.ndim)
    outs = [("t", 256, BF16), ("r", 128, BF16), ("t", HEAD_DIM, BF16), ("t", 256, BF16), ("r", 128, BF16),
            ("t", IDX_HEADS, F32), ("t", 512, BF16), ("r", 512, BF16), ("t", 256, BF16),
            ("r", 512, BF16), ("r", 128, BF16), ("r", 128, BF16), ("r", 3072, F32)]
    shape = lambda k, w, dt: jax.ShapeDtypeStruct((n // tm, w, tm) if k == "t" else (n, w), dt)
    spec = lambda k, w: pl.BlockSpec((1, w, tm), lambda i: (i, 0, 0)) if k == "t" else row(w)
    return pl.pallas_call(
        _proj_kernel,
        out_shape=tuple(shape(*o) for o in outs),
        grid=(n // tm,),
        in_specs=[row(d), full(g), full(w_r), full(wuq), full(wukv), full(gcq), full(grows), full(mseg)]
        + [row(LANES)] * 4,
        out_specs=tuple(spec(k, w) for k, w, _ in outs),
        compiler_params=_params(("arbitrary",)), name="in_proj",
    )(xf, g, w_r, wuq, wukv, gcq, grows, mseg, *tabs)


def _flash_logits(ks, qts):
    return tuple(_mm(k, qt) for k, qt in zip(ks, qts))


def _flash_update(ss, vts, carries, bias=None):
    if bias is not None:
        ss = [s + bias for s in ss]
    m_new = [jnp.maximum(c[0], jnp.max(s, axis=0, keepdims=True)) for c, s in zip(carries, ss)]
    ps = [jnp.exp2(s - m) for s, m in zip(ss, m_new)]
    pvs = [_mm(vt, p.astype(BF16)) for vt, p in zip(vts, ps)]
    out = []
    for (m, l, acc), mn, p, pv in zip(carries, m_new, ps, pvs):
        alpha = jnp.exp2(m - mn)
        out.append((mn, alpha * l + jnp.sum(p, axis=0, keepdims=True), alpha * acc + pv))
    return tuple(out)


def _flash_init(dv, tq):
    return jnp.full((1, tq), NEG_BIG, F32), jnp.zeros((1, tq), F32), jnp.zeros((dv, tq), F32)


def _flash_out(results):
    return jnp.concatenate([acc / l for _, l, acc in results], axis=0).T.astype(BF16)


def _dsa_kernel(iqt_ref, iwt_ref, qat_ref, ik_ref, kv_ref, vat_ref, o_ref, keys_ref, gmax_ref, *, tq, nsel, seq):
    ck = tq
    i = pl.program_id(1)
    nk = i + 1
    iqt = iqt_ref[0]
    iwt = iwt_ref[0]
    half = ck // 2
    qpos_h = lax.broadcasted_iota(I32, (half, tq), 1) + i * tq
    krow_h = lax.broadcasted_iota(I32, (half, tq), 0)
    krow = lax.broadcasted_iota(I32, (SUB, tq), 0)
    gmax_ref[...] = jnp.full(gmax_ref.shape, -jnp.inf, F32)

    def to_key(score):
        bits = lax.bitcast_convert_type(score, I32)
        return bits ^ ((bits >> 31) & 0x7FFFFFFF)

    def score_body(c, carry):
        for r in range(ck // half):
            rows = pl.ds(pl.multiple_of(c * ck + r * half, half), half)
            ikc = ik_ref[rows, :][:, 0:IDX_DIM]
            acc = jnp.zeros((half, tq), F32)
            for h in range(IDX_HEADS):
                sc = _mm(ikc, iqt[h * IDX_DIM:(h + 1) * IDX_DIM, :])
                acc = acc + jnp.maximum(sc, 0.0) * iwt[h:h + 1, :]
            causal = krow_h + (c * ck + r * half) <= qpos_h
            keys_ref[c, r * half:(r + 1) * half, :] = jnp.where(causal, to_key(acc), INT_MIN)
            gmax_ref[r * half:(r + 1) * half, :] = jnp.maximum(gmax_ref[r * half:(r + 1) * half, :],
                                                               jnp.where(causal, acc, -jnp.inf))
        return carry

    lax.fori_loop(0, nk, score_body, 0)

    def rep(row):
        return jnp.broadcast_to(row, (SUB, tq))

    def tile(v):
        return jnp.concatenate([v] * (ck // SUB), axis=0)

    def count(preds, *ops):
        nacc = 4 // len(preds)

        def body(c, accs):
            accs = [list(a) for a in accs]
            for g in range(ck // SUB):
                k = keys_ref[c, g * SUB:(g + 1) * SUB, :]
                for a, pred in zip(accs, preds):
                    a[g % nacc] = a[g % nacc] + jnp.where(pred(k, *ops), 1.0, 0.0)
            return tuple(tuple(a) for a in accs)
        accs = lax.fori_loop(0, nk, body, ((jnp.zeros((SUB, tq), F32),) * nacc,) * len(preds))
        return [rep(jnp.sum(sum(a[1:], a[0]), axis=0, keepdims=True)) for a in accs]

    gm = gmax_ref[...]
    smin = rep(jnp.min(gm, axis=0, keepdims=True))
    lo0 = jnp.where(smin == -jnp.inf, INT_MIN, to_key(smin))
    hi0 = to_key(rep(jnp.max(gm, axis=0, keepdims=True)))
    n_pos, n_nonneg = count([lambda k: k > 0, lambda k: k >= 0])
    at_zero = (n_pos < nsel) & (n_nonneg >= nsel)
    lo0 = jnp.where(at_zero, 0, jnp.where(n_pos >= nsel, jnp.maximum(lo0, 1), lo0))
    hi0 = jnp.where(at_zero, 0, jnp.where(n_nonneg < nsel, jnp.minimum(hi0, -1), hi0))
    unknown = float(2 * seq)
    cnt0 = jnp.where(at_zero, n_nonneg, unknown)

    def bis_cond(state):
        return (state[0] < 34) & (state[4] < 0.5)

    def bis_body(state):
        it, lo, hi, n_lo, _ = state
        all_done = jnp.min(jnp.where(lo == hi, 1.0, 0.0))
        gap = hi - lo
        mid = lo + lax.shift_right_logical(gap, 1) + (gap & 1)
        cnt, = count([lambda k, t: k >= t], mid)
        up = cnt >= nsel
        lo = jnp.where(up, mid, lo)
        n_lo = jnp.where(up, cnt, n_lo)
        hi = jnp.where(cnt > nsel, hi, jnp.where(cnt == nsel, mid, mid - 1))
        return it + 1, lo, hi, n_lo, all_done

    _, thr, _, n_thr, _ = lax.while_loop(bis_cond, bis_body, (jnp.int32(0), lo0, hi0, cnt0, jnp.float32(0.0)))

    tied = (n_thr != nsel) & (thr != INT_MIN)

    @pl.when(jnp.max(jnp.where(tied, 1.0, 0.0)) > 0.0)
    def _():
        n_above, = count([lambda k, t: k > t], thr)
        keep_t = tile(nsel - n_above)
        thr_t = tile(thr)
        tied_t = tile(tied)
        tri = jnp.where(lax.broadcasted_iota(I32, (ck, ck), 1) <= lax.broadcasted_iota(I32, (ck, ck), 0),
                        1.0, 0.0).astype(BF16)

        def strike(c, seen):
            key = keys_ref[c]
            tie = (key == thr_t) & tied_t
            rank = _mm(tri, jnp.where(tie, 1.0, 0.0).astype(BF16)) + tile(seen)
            keys_ref[c] = jnp.where(tie & (rank > keep_t), INT_MIN, key)
            return rep(rank[ck - 1:ck, :])
        lax.fori_loop(0, nk, strike, jnp.zeros((SUB, tq), F32))

    qat = qat_ref[0]
    sel_t = tile(jnp.where(thr == INT_MIN, INT_MIN + 1, thr))
    qts = [qat[h * HEAD_DIM:(h + 1) * HEAD_DIM, :] for h in range(A_HEADS)]

    def step(c, width, carry):
        k = kv_ref[pl.ds(pl.multiple_of(c * ck, ck), width * ck), :][:, 0:HEAD_DIM]
        vt = jnp.concatenate([vat_ref[c + j] for j in range(width)], axis=1)
        bias = jnp.concatenate([jnp.where(keys_ref[c + j] >= sel_t, 0.0, NEG_BIG) for j in range(width)], axis=0)
        ss = _flash_logits([k] * A_HEADS, qts)
        return _flash_update(ss, [vt] * A_HEADS, carry, bias)

    carry = (_flash_init(HEAD_DIM, tq),) * A_HEADS
    carry = lax.fori_loop(0, nk // 2, lambda c2, cr: step(2 * c2, 2, cr), carry)
    carry = lax.fori_loop((nk // 2) * 2, nk, lambda c, cr: step(c, 1, cr), carry)
    o_ref[...] = _flash_out(carry)


def _dsa(iqt, iwt, qat, ik, kva, vat, b, s, tq=256):
    n = b * s
    nq = s // tq
    nsel = min(TOPK_MAX, s // 4)
    assert nsel <= tq and iqt.shape[2] == tq
    qtile = lambda w: pl.BlockSpec((1, w, tq), lambda bi, i: (bi * nq + i, 0, 0))
    seq_rows = pl.BlockSpec((s, LANES), lambda bi, i: (bi, 0))
    return pl.pallas_call(
        functools.partial(_dsa_kernel, tq=tq, nsel=nsel, seq=s),
        out_shape=jax.ShapeDtypeStruct((n, A_HEADS * HEAD_DIM), BF16), grid=(b, nq),
        in_specs=[qtile(IDX_HEADS * IDX_DIM), qtile(IDX_HEADS), qtile(A_HEADS * HEAD_DIM), seq_rows, seq_rows,
                  pl.BlockSpec((nq, HEAD_DIM, tq), lambda bi, i: (bi, 0, 0))],
        out_specs=pl.BlockSpec((tq, A_HEADS * HEAD_DIM), lambda bi, i: (bi * nq + i, 0)),
        scratch_shapes=[pltpu.VMEM((nq, tq, tq), I32), pltpu.VMEM((tq, tq), F32)],
        compiler_params=_params(("arbitrary", "arbitrary")), name="dsa_attention",
    )(iqt, iwt, qat, ik, kva, vat)


def _mla_kernel(qt_ref, k_ref, vt_ref, o_ref, *, tq):
    i = pl.program_id(1)
    qt = qt_ref[0]
    kpos = lax.broadcasted_iota(I32, (tq, tq), 0)
    qpos = lax.broadcasted_iota(I32, (tq, tq), 1)
    diag_bias = jnp.where(kpos <= qpos, 0.0, NEG_BIG)

    heads = range(B_HEADS)
    qts = [qt[_slab(h), :] for h in heads]

    def step(c, width, carry, bias=None):
        rows = pl.ds(pl.multiple_of(c * tq, tq), width * tq)
        vt = jnp.concatenate([vt_ref[c + j] for j in range(width)], axis=1)
        ss = _flash_logits([k_ref[rows, _slab(h)] for h in heads], qts)
        return _flash_update(ss, [vt[h * B_V:(h + 1) * B_V, :] for h in heads], carry, bias)

    carry = (_flash_init(B_V, tq),) * B_HEADS
    carry = lax.fori_loop(0, i // 2, lambda c2, cr: step(2 * c2, 2, cr), carry)
    carry = lax.fori_loop((i // 2) * 2, i, lambda c, cr: step(c, 1, cr), carry)
    o_ref[...] = _flash_out(step(i, 1, carry, diag_bias))


def _mla(qbt, kb, vbt, b, s, tq=256):
    n = b * s
    nq = s // tq
    assert qbt.shape[2] == tq
    return pl.pallas_call(
        functools.partial(_mla_kernel, tq=tq),
        out_shape=jax.ShapeDtypeStruct((n, B_HEADS * B_V), BF16), grid=(b, nq),
        in_specs=[pl.BlockSpec((1, B_HEADS * LANES, tq), lambda bi, i: (bi * nq + i, 0, 0)),
                  pl.BlockSpec((s, B_HEADS * LANES), lambda bi, i: (bi, 0)),
                  pl.BlockSpec((nq, B_HEADS * B_V, tq), lambda bi, i: (bi, 0, 0))],
        out_specs=pl.BlockSpec((tq, B_HEADS * B_V), lambda bi, i: (bi * nq + i, 0)),
        compiler_params=_params(("arbitrary", "arbitrary")), name="mla_attention",
    )(qbt, kb, vbt)


def _swa_kernel(sink_ref, q_ref, kp_ref, kc_ref, vp_ref, vc_ref, o_ref, *, tq):
    i = pl.program_id(1)
    q = q_ref[...]
    keys = jnp.concatenate([kp_ref[...], kc_ref[...]], axis=0)
    vals = jnp.concatenate([vp_ref[...], vc_ref[...]], axis=0)
    nkeys = WINDOW + tq
    qpos = lax.broadcasted_iota(I32, (tq, nkeys), 0) + i * tq
    kpos = lax.broadcasted_iota(I32, (tq, nkeys), 1) + i * tq - WINDOW
    visible = (kpos > qpos - WINDOW) & (kpos <= qpos) & (kpos >= 0)
    bias = jnp.where(visible, 0.0, NEG_BIG)
    group = C_HEADS // C_KV_HEADS
    heads = range(C_HEADS)
    ks = [keys[:, j * HEAD_DIM:(j + 1) * HEAD_DIM] for j in range(C_KV_HEADS)]
    vs = [vals[:, j * HEAD_DIM:(j + 1) * HEAD_DIM] for j in range(C_KV_HEADS)]
    ss = [_mm_nt(q[:, h * HEAD_DIM:(h + 1) * HEAD_DIM], ks[h // group]) + bias for h in heads]
    ms = [jnp.maximum(jnp.max(s, axis=1, keepdims=True), sink_ref[h]) for h, s in zip(heads, ss)]
    ps = [jnp.exp(s - m) for s, m in zip(ss, ms)]
    ws = [p / (jnp.sum(p, axis=1, keepdims=True) + jnp.exp(sink_ref[h] - m)) for h, p, m in zip(heads, ps, ms)]
    outs = [_mm(w.astype(BF16), vs[h // group]) for h, w in zip(heads, ws)]
    o_ref[...] = jnp.concatenate(outs, axis=1).astype(BF16)


def _swa(sinks, qc, kc, vc, b, s, tq=256):
    n = b * s
    nq = s // tq
    per = tq // WINDOW
    qrow = lambda w: pl.BlockSpec((tq, w), lambda bi, i: (bi * nq + i, 0))
    prev = pl.BlockSpec((WINDOW, LANES), lambda bi, i: (jnp.maximum((bi * nq + i) * per - 1, 0), 0))
    return pl.pallas_call(
        functools.partial(_swa_kernel, tq=tq),
        out_shape=jax.ShapeDtypeStruct((n, C_HEADS * HEAD_DIM), BF16), grid=(b, nq),
        in_specs=[pl.BlockSpec(memory_space=pltpu.SMEM), qrow(512), prev, qrow(LANES), prev, qrow(LANES)],
        out_specs=qrow(512),
        compiler_params=_params(("arbitrary", "arbitrary")), name="swa_attention",
    )(sinks, qc, kc, kc, vc, vc)


def _merge_kernel(ya_ref, yb_ref, yc_ref, gt_ref, x_ref, wa_ref, wb_ref, wc_ref, wo_ref, g_ref,
                  xo_ref, h_ref):
    d = x_ref.shape[1]
    merged = (gt_ref[:, 0:d] * _mm(ya_ref[...], wa_ref[...])
              + gt_ref[:, d:2 * d] * _mm(yb_ref[...], wb_ref[...])
              + gt_ref[:, 2 * d:3 * d] * _mm(yc_ref[...], wc_ref[...]))
    xn = x_ref[...] + _mm(merged.astype(BF16), wo_ref[...])
    xo_ref[...] = xn
    h = xn * lax.rsqrt(jnp.mean(xn * xn, axis=-1, keepdims=True) + EPS) * g_ref[...]
    h_ref[...] = h.astype(h_ref.dtype)


def _merge(ya, yb, yc, gt, xf, wa, wb, wc, wo, g, h_dtype, tm=256):
    n, d = xf.shape
    row = lambda w: pl.BlockSpec((tm, w), lambda i: (i, 0))
    full = lambda a: pl.BlockSpec(a.shape, lambda i: (0,) * a.ndim)
    return pl.pallas_call(
        _merge_kernel,
        out_shape=(jax.ShapeDtypeStruct((n, d), F32), jax.ShapeDtypeStruct((n, d), h_dtype)),
        grid=(n // tm,),
        in_specs=[row(256), row(256), row(512), row(3 * d), row(d), full(wa), full(wb), full(wc), full(wo),
                  full(g)],
        out_specs=(row(d), row(d)),
        compiler_params=_params(("arbitrary",)), name="merge_out_proj",
    )(ya, yb, yc, gt, xf, wa, wb, wc, wo, g)


def _ffn_kernel(h_ref, x_ref, w1_hbm, w3_hbm, w2_hbm, o_ref, w1_ref, w3_ref, w2_ref, sem, *, chunk):
    @pl.when(pl.program_id(0) == 0)
    def _():
        copies = [pltpu.make_async_copy(src, dst, sem.at[j]) for j, (src, dst) in
                  enumerate(((w1_hbm, w1_ref), (w3_hbm, w3_ref), (w2_hbm, w2_ref)))]
        for cp in copies:
            cp.start()
        for cp in copies:
            cp.wait()

    h = h_ref[...]
    acc = x_ref[...]
    for j in range(w1_ref.shape[1] // chunk):
        cols = slice(j * chunk, (j + 1) * chunk)
        a = _mm(h, w1_ref[:, cols])
        mid = a * _sigmoid(a) * _mm(h, w3_ref[:, cols])
        acc = acc + _mm(mid.astype(BF16), w2_ref[cols, :])
    o_ref[...] = acc


def _ffn(h2, xf, w1, w3, w2, tm=256, chunk=1408):
    n, d = xf.shape
    row = lambda w: pl.BlockSpec((tm, w), lambda i: (i, 0))
    hbm = pl.BlockSpec(memory_space=pl.ANY)
    return pl.pallas_call(
        functools.partial(_ffn_kernel, chunk=chunk),
        out_shape=jax.ShapeDtypeStruct((n, d), F32), grid=(n // tm,),
        in_specs=[row(d), row(d), hbm, hbm, hbm], out_specs=row(d),
        scratch_shapes=[pltpu.VMEM(w1.shape, BF16), pltpu.VMEM(w3.shape, BF16), pltpu.VMEM(w2.shape, BF16),
                        pltpu.SemaphoreType.DMA((3,))],
        compiler_params=_params(("arbitrary",)), name="dense_swiglu",
    )(h2, xf, w1, w3, w2)


def _route_kernel(h_ref, wr_ref, tri_ref, route_ref, cnt_ref, carry_ref):
    @pl.when(pl.program_id(0) == 0)
    def _():
        carry_ref[...] = jnp.zeros_like(carry_ref)

    logits = jnp.dot(h_ref[...], wr_ref[...], precision=lax.Precision.HIGHEST, preferred_element_type=F32)
    lane = lax.broadcasted_iota(I32, logits.shape, 1).astype(F32)
    lg = jnp.where(lane < N_EXPERTS, logits, -jnp.inf)
    m1 = jnp.max(lg, axis=1, keepdims=True)
    e1 = jnp.min(jnp.where(lg == m1, lane, float(LANES)), axis=1, keepdims=True)
    lg2 = jnp.where(lane == e1, -jnp.inf, lg)
    m2 = jnp.max(lg2, axis=1, keepdims=True)
    e2 = jnp.min(jnp.where(lg2 == m2, lane, float(LANES)), axis=1, keepdims=True)
    ex = jnp.exp(m2 - m1)
    g1 = 1.0 / (1.0 + ex)
    g2 = ex / (1.0 + ex)
    onehot = jnp.where((lane == e1) | (lane == e2), 1.0, 0.0)
    before = _mm(tri_ref[...], onehot.astype(BF16)) + carry_ref[0:1, :]
    r1 = jnp.sum(jnp.where(lane == e1, before, 0.0), axis=1, keepdims=True)
    r2 = jnp.sum(jnp.where(lane == e2, before, 0.0), axis=1, keepdims=True)
    out = jnp.zeros_like(logits)
    for idx, val in enumerate((e1, e2, g1, g2, r1, r2)):
        out = jnp.where(lane == idx, val, out)
    route_ref[...] = out
    total = carry_ref[0:1, :] + jnp.sum(onehot, axis=0, keepdims=True)
    carry_ref[...] = jnp.broadcast_to(total, carry_ref.shape)
    cnt_ref[...] = jnp.broadcast_to(total, cnt_ref.shape)


def _route(h2, wr, tm=256):
    n, d = h2.shape
    tri = jnp.asarray(np.tril(np.ones((tm, tm), np.float32), -1), BF16)
    full = lambda a: pl.BlockSpec(a.shape, lambda i: (0,) * a.ndim)
    return pl.pallas_call(
        _route_kernel,
        out_shape=(jax.ShapeDtypeStruct((n, LANES), F32), jax.ShapeDtypeStruct((8, LANES), F32)),
        grid=(n // tm,),
        in_specs=[pl.BlockSpec((tm, d), lambda i: (i, 0)), full(wr), full(tri)],
        out_specs=(pl.BlockSpec((tm, LANES), lambda i: (i, 0)), pl.BlockSpec((8, LANES), lambda i: (0, 0))),
        scratch_shapes=[pltpu.VMEM((8, LANES), F32)],
        compiler_params=_params(("arbitrary",)), name="moe_route",
    )(h2, wr, tri)


def _scatter_kernel(dest_ref, h_ref, xin_hbm, xout_hbm, sem, *, tt):
    del xin_hbm

    def copy(r, k):
        d = dest_ref[0, 0, 2 * r + k]
        return pltpu.make_async_copy(h_ref.at[pl.ds(r, 1), :], xout_hbm.at[pl.ds(d, 1), :], sem)

    def start(r, carry):
        copy(r, 0).start()
        copy(r, 1).start()
        return carry

    def wait(r, carry):
        copy(r, 0).wait()
        copy(r, 1).wait()
        return carry

    lax.fori_loop(0, tt, start, 0)
    lax.fori_loop(0, tt, wait, 0)


def _scatter(dest3, h2, rows, tt):
    n, w = h2.shape
    hbm = pl.BlockSpec(memory_space=pl.ANY)
    return pl.pallas_call(
        functools.partial(_scatter_kernel, tt=tt),
        out_shape=jax.ShapeDtypeStruct((rows, w), h2.dtype), grid=(n // tt,),
        in_specs=[pl.BlockSpec((1, 1, 2 * tt), lambda i: (i, 0, 0), memory_space=pltpu.SMEM),
                  pl.BlockSpec((tt, w), lambda i: (i, 0)), hbm],
        out_specs=hbm, scratch_shapes=[pltpu.SemaphoreType.DMA(())],
        input_output_aliases={2: 0},
        compiler_params=_params(("arbitrary",)), name="moe_scatter",
    )(dest3, h2, jnp.zeros((rows, w), h2.dtype))


def _expert_kernel(be_ref, nv_ref, x_ref, w1_hbm, w3_hbm, w2_hbm, y_ref, w1_ref, w3_ref, w2_ref, sem, *, chunk):
    blk = pl.program_id(0)
    e = be_ref[blk]

    @pl.when((blk == 0) | (e != be_ref[jnp.maximum(blk - 1, 0)]))
    def _():
        copies = [pltpu.make_async_copy(src.at[e], dst, sem.at[j]) for j, (src, dst) in
                  enumerate(((w1_hbm, w1_ref), (w3_hbm, w3_ref), (w2_hbm, w2_ref)))]
        for cp in copies:
            cp.start()
        for cp in copies:
            cp.wait()

    @pl.when(blk < nv_ref[0])
    def _():
        x = x_ref[...].astype(BF16)
        acc = jnp.zeros(y_ref.shape, F32)
        for j in range(w1_ref.shape[1] // chunk):
            cols = slice(j * chunk, (j + 1) * chunk)
            a = _mm(x, w1_ref[:, cols])
            mid = a * _sigmoid(a) * _mm(x, w3_ref[:, cols])
            acc = acc + _mm(mid.astype(BF16), w2_ref[cols, :])
        y_ref[...] = acc

    @pl.when(blk >= nv_ref[0])
    def _():
        y_ref[...] = jnp.zeros(y_ref.shape, F32)


def _experts(block_e, nvalid, xbuf, w1, w3, w2, chunk=512):
    rows, d = xbuf.shape
    hbm = pl.BlockSpec(memory_space=pl.ANY)
    blk = pl.BlockSpec((MOE_BLOCK, d), lambda i, be, nv: (i, 0))
    return pl.pallas_call(
        functools.partial(_expert_kernel, chunk=chunk),
        out_shape=jax.ShapeDtypeStruct((rows, d), F32),
        grid_spec=pltpu.PrefetchScalarGridSpec(
            num_scalar_prefetch=2, grid=(rows // MOE_BLOCK,),
            in_specs=[blk, hbm, hbm, hbm], out_specs=blk,
            scratch_shapes=[pltpu.VMEM(w1.shape[1:], BF16), pltpu.VMEM(w3.shape[1:], BF16),
                            pltpu.VMEM(w2.shape[1:], BF16), pltpu.SemaphoreType.DMA((3,))]),
        compiler_params=_params(("arbitrary",)), name="moe_experts",
    )(block_e, nvalid, xbuf, w1, w3, w2)


def _combine_kernel(dest_ref, y_hbm, x_ref, route_ref, o_ref, buf_ref, sem, *, tt):
    def copy(r, k):
        d = dest_ref[0, 0, 2 * r + k]
        return pltpu.make_async_copy(y_hbm.at[pl.ds(d, 1), :], buf_ref.at[k, pl.ds(r, 1), :], sem)

    def start(r, carry):
        copy(r, 0).start()
        copy(r, 1).start()
        return carry

    def wait(r, carry):
        copy(r, 0).wait()
        copy(r, 1).wait()
        return carry

    lax.fori_loop(0, tt, start, 0)
    lax.fori_loop(0, tt, wait, 0)
    rt = route_ref[...]
    o_ref[...] = x_ref[...] + (buf_ref[0] * rt[:, 2:3] + buf_ref[1] * rt[:, 3:4])


def _combine(dest3, ybuf, xf, route, tt):
    n, d = xf.shape
    row = lambda w: pl.BlockSpec((tt, w), lambda i: (i, 0))
    return pl.pallas_call(
        functools.partial(_combine_kernel, tt=tt),
        out_shape=jax.ShapeDtypeStruct((n, d), F32), grid=(n // tt,),
        in_specs=[pl.BlockSpec((1, 1, 2 * tt), lambda i: (i, 0, 0), memory_space=pltpu.SMEM),
                  pl.BlockSpec(memory_space=pl.ANY), row(d), row(LANES)],
        out_specs=row(d),
        scratch_shapes=[pltpu.VMEM((2, tt, d), F32), pltpu.SemaphoreType.DMA(())],
        compiler_params=_params(("arbitrary",)), name="moe_combine",
    )(dest3, ybuf, xf, route)


def _moe(xf, h2, w_router, w1, w3, w2, tt=256):
    n, d = xf.shape
    wr = jnp.pad(w_router, ((0, 0), (0, LANES - N_EXPERTS)))
    route, cnt = _route(h2, wr)
    counts = cnt[0, :N_EXPERTS].astype(I32)
    padded = (counts + MOE_BLOCK - 1) // MOE_BLOCK * MOE_BLOCK
    pend = jnp.cumsum(padded)
    pstart = pend - padded
    dest = pstart[route[:, 0:2].astype(I32)] + route[:, 4:6].astype(I32)
    dest3 = dest.reshape(n // tt, 1, 2 * tt)
    n_blocks = n * 2 // MOE_BLOCK + N_EXPERTS
    rows = n_blocks * MOE_BLOCK
    block_e = jnp.minimum(jnp.searchsorted(pend, jnp.arange(n_blocks) * MOE_BLOCK, side='right'),
                          N_EXPERTS - 1).astype(I32)
    nvalid = (pend[-1:] // MOE_BLOCK).astype(I32)
    xbuf = _scatter(dest3, h2, rows, tt)
    ybuf = _experts(block_e, nvalid, xbuf, w1, w3, w2)
    return _combine(dest3, ybuf, xf, route, tt)


def _tile_row(v, width=LANES):
    v = v.astype(F32)
    return jnp.tile(v, width // v.shape[0])


def _layer_params(l, attn_norm_g, w_in, a_q_norm_g, a_k_norm_g, b_cq_norm_g, b_ckv_norm_g, b_w_uq, b_w_ukv,
                  b_qn_g, b_qr_g, b_kn_g, b_kr_g, c_q_norm_g, c_k_norm_g):
    d = w_in.shape[1]
    sizes = (A_HEADS * HEAD_DIM, HEAD_DIM, HEAD_DIM, IDX_HEADS * IDX_DIM, IDX_DIM, IDX_HEADS,
             b_w_uq.shape[1], b_w_ukv.shape[1], B_ROPE, C_HEADS * HEAD_DIM, C_KV_HEADS * HEAD_DIM,
             C_KV_HEADS * HEAD_DIM, 3 * d)
    qa, ka, va, iq, ik, iw, cq, ckv, kr, qc, kc, vc, gates = jnp.split(w_in[l], np.cumsum(sizes)[:-1].tolist(), axis=1)
    z = lambda k: jnp.zeros((d, k), F32)
    w_r = jnp.concatenate([qa, ka, va, iq, ik, iw, z(LANES - IDX_DIM - IDX_HEADS), cq, ckv,
                           z(B_NOPE), kr, z(LANES - B_NOPE - B_ROPE), qc, kc, vc, gates], axis=1).astype(BF16)
    uq = b_w_uq[l].reshape(-1, B_HEADS, B_NOPE + B_ROPE)
    uq = jnp.pad(uq, ((0, 0), (0, 0), (0, LANES - B_NOPE - B_ROPE))).reshape(-1, B_HEADS * LANES).astype(BF16)
    ukv = b_w_ukv[l].reshape(-1, B_HEADS, B_NOPE + B_V)
    uk = jnp.pad(ukv[:, :, :B_NOPE], ((0, 0), (0, 0), (0, LANES - B_NOPE))).reshape(-1, B_HEADS * LANES)
    uv = ukv[:, :, B_NOPE:].reshape(-1, B_HEADS * B_V)
    wukv = jnp.concatenate([uk, uv], axis=1).astype(BF16)
    zeros = lambda k: jnp.zeros((k,), F32)
    grows = jnp.stack([
        _tile_row(a_q_norm_g[l]),
        jnp.concatenate([a_k_norm_g[l], jnp.ones((HEAD_DIM,), F32)]),
        jnp.concatenate([b_qn_g[l], b_qr_g[l], zeros(LANES - B_NOPE - B_ROPE)]),
        jnp.concatenate([b_kn_g[l], zeros(LANES - B_NOPE)]),
        jnp.concatenate([zeros(B_NOPE), b_kr_g[l], zeros(LANES - B_NOPE - B_ROPE)]),
        _tile_row(c_q_norm_g[l]),
        _tile_row(c_k_norm_g[l]),
        b_ckv_norm_g[l],
    ]).astype(F32)
    return (attn_norm_g[l][None, :], w_r, uq, wukv, b_cq_norm_g[l][None, :], grows)


def _seg_matrices():
    lane = np.arange(LANES)
    g64 = lane // 64
    m64 = (g64[:, None] == g64[None, :]) / 64.0
    gqb = np.where(lane < B_NOPE, 0, np.where(lane < B_NOPE + B_ROPE, 1, 2))
    size = np.where(lane < B_NOPE, B_NOPE, B_ROPE)
    mqb = (gqb[:, None] == gqb[None, :]) / size[None, :]
    return jnp.asarray(np.stack([m64, mqb]), BF16)


def kernel(x, positions, attn_norm_g, w_in, a_q_norm_g, a_k_norm_g, b_cq_norm_g, b_ckv_norm_g, b_w_uq, b_w_ukv,
           b_qn_g, b_qr_g, b_kn_g, b_kr_g, c_q_norm_g, c_k_norm_g, c_sinks, w_a_out, w_b_out, w_c_out, w_o,
           ffn_norm_g, ffn_w1, ffn_w3, ffn_w2, router_w, moe_w1, moe_w3, moe_w2):
    b, s, d = x.shape
    n = b * s
    depth = w_in.shape[0]
    xf = x.reshape(n, d)
    tabs = _rope_tables(positions.reshape(n, 1).astype(F32))
    mseg = _seg_matrices()
    for l in range(depth):
        g, w_r, uq, wukv, gcq, grows = _layer_params(
            l, attn_norm_g, w_in, a_q_norm_g, a_k_norm_g, b_cq_norm_g, b_ckv_norm_g, b_w_uq, b_w_ukv,
            b_qn_g, b_qr_g, b_kn_g, b_kr_g, c_q_norm_g, c_k_norm_g)
        qat, kva, vat, iqt, ik, iwt, qbt, kb, vbt, qc, kc, vc, gt = _proj(xf, g, w_r, uq, wukv, gcq, grows, mseg,
                                                                         tabs)
        ya = _dsa(iqt, iwt, qat, ik, kva, vat, b, s)
        yb = _mla(qbt, kb, vbt, b, s)
        yc = _swa(c_sinks[l], qc, kc, vc, b, s)
        dense = l % 2 == 0
        xf, h2 = _merge(ya, yb, yc, gt, xf, w_a_out[l].astype(BF16), w_b_out[l].astype(BF16),
                        w_c_out[l].astype(BF16), w_o[l].astype(BF16), ffn_norm_g[l][None, :],
                        BF16 if dense else F32)
        if dense:
            xf = _ffn(h2, xf, ffn_w1[l // 2].astype(BF16), ffn_w3[l // 2].astype(BF16), ffn_w2[l // 2].astype(BF16))
        else:
            xf = _moe(xf, h2, router_w[l // 2], moe_w1[l // 2].astype(BF16),
                      moe_w3[l // 2].astype(BF16), moe_w2[l // 2].astype(BF16))
    return xf.reshape(b, s, d)
```

```python
import functools

import numpy as np
import jax
import jax.numpy as jnp
from jax import lax
from jax.experimental import pallas as pl
from jax.experimental.pallas import tpu as pltpu

F32, BF16, I32 = jnp.float32, jnp.bfloat16, jnp.int32

EPS = 1e-6
ROPE_THETA = 10000.0
HEAD_DIM = 64
A_HEADS = 4
IDX_HEADS = 8
IDX_DIM = 32
TOPK_MAX = 256
B_HEADS = 4
B_NOPE = 64
B_ROPE = 32
B_V = 64
C_HEADS = 8
C_KV_HEADS = 2
WINDOW = 128
N_EXPERTS = 8
MOE_BLOCK = 256

LANES = 128
SUB = 8
VMEM_LIMIT = 56 * 1024 * 1024
INT_MIN = -2 ** 31
NEG_BIG = -1e30
LOG2E = 1.4426950408889634

C_QA, C_KVA, C_IQ, C_IKW, C_CQ, C_CKV, C_KR, C_QC, C_KC, C_VC, C_G = (
    0, 256, 384, 640, 768, 1024, 1152, 1280, 1792, 1920, 2048)


def _params(sem):
    return pltpu.CompilerParams(dimension_semantics=sem, vmem_limit_bytes=VMEM_LIMIT)


def _mm(a, b):
    return jnp.dot(a, b, preferred_element_type=F32)


def _mm_nt(a, b):
    return lax.dot_general(a, b, (((1,), (1,)), ((), ())), preferred_element_type=F32)


def _sigmoid(v):
    return 1.0 / (1.0 + jnp.exp(-v))


def _rope_table_kernel(pos_ref, f_ref, sg_ref, c64_ref, s64_ref, c32_ref, s32_ref):
    pos = pos_ref[...]
    a = pos * f_ref[0:1, :]
    c64_ref[...] = jnp.cos(a)
    s64_ref[...] = jnp.sin(a) * sg_ref[0:1, :]
    a = pos * f_ref[1:2, :]
    c32_ref[...] = jnp.cos(a)
    s32_ref[...] = jnp.sin(a) * sg_ref[1:2, :]


def _rope_tables(pos_f, tm=1024):
    n = pos_f.shape[0]
    lane = np.arange(LANES)
    f32 = ROPE_THETA ** (-jnp.arange(32, dtype=F32) / 32)
    f16 = ROPE_THETA ** (-jnp.arange(16, dtype=F32) / 16)
    freqs = jnp.stack([f32[lane % 32], f16[lane % 16]])
    signs = jnp.asarray(np.stack([np.where(lane % 64 < 32, -1.0, 1.0),
                                  np.where(lane % 32 < 16, -1.0, 1.0)]), F32)
    tab = jax.ShapeDtypeStruct((n, LANES), F32)
    row = pl.BlockSpec((tm, LANES), lambda i: (i, 0))
    par = pl.BlockSpec((2, LANES), lambda i: (0, 0))
    return pl.pallas_call(
        _rope_table_kernel, out_shape=(tab,) * 4, grid=(n // tm,),
        in_specs=[pl.BlockSpec((tm, 1), lambda i: (i, 0)), par, par],
        out_specs=(row,) * 4, compiler_params=_params(("arbitrary",)), name="rope_tables",
    )(pos_f, freqs, signs)


def _swap_half(y, half):
    lane = lax.broadcasted_iota(I32, y.shape, 1)
    return jnp.where((lane & half) == 0, pltpu.roll(y, LANES - half, 1), pltpu.roll(y, half, 1))


def _rope(y, cos, sin_signed, half):
    return y * cos + _swap_half(y, half) * sin_signed


def _seg_mean_sq(y, mseg):
    sq = y * y
    hi = sq.astype(BF16)
    lo = (sq - hi.astype(F32)).astype(BF16)
    return _mm(hi, mseg) + _mm(lo, mseg)


def _seg_norm(y, mseg, gain):
    return y * lax.rsqrt(_seg_mean_sq(y, mseg) + EPS) * gain


def _slab(s):
    return slice(s * LANES, (s + 1) * LANES)


def _proj_kernel(x_ref, g_ref, w_ref, wuq_ref, wukv_ref, gcq_ref, grows_ref, mseg_ref,
                 c64_ref, s64_ref, c32_ref, s32_ref,
                 qat_ref, kva_ref, vat_ref, iqt_ref, ik_ref, iwt_ref, qbt_ref, kb_ref, vbt_ref,
                 qc_ref, kc_ref, vc_ref, gt_ref):
    x = x_ref[...]
    h = (x * lax.rsqrt(jnp.mean(x * x, axis=-1, keepdims=True) + EPS) * g_ref[...]).astype(BF16)
    c64, s64, c32, s32 = c64_ref[...], s64_ref[...], c32_ref[...], s32_ref[...]
    m64, mqb = mseg_ref[0], mseg_ref[1]
    lane = lax.broadcasted_iota(I32, c64.shape, 1)
    in_rope = (lane >= B_NOPE) & (lane < B_NOPE + B_ROPE)
    cb = jnp.where(in_rope, c32, 1.0)
    sb = jnp.where(in_rope, s32, 0.0)

    p = _mm(h, w_ref[:, C_QA:C_CQ])
    for s in range(2):
        y = _rope(_seg_norm(p[:, _slab(s)], m64, grows_ref[0:1, :]), c64, s64, 32)
        qat_ref[0, _slab(s), :] = (y * (LOG2E * HEAD_DIM ** -0.5)).T.astype(BF16)
    y = p[:, _slab(2)]
    yr = _rope(_seg_norm(y, m64, grows_ref[1:2, :]), c64, s64, 32)
    kva_ref[...] = jnp.where(lane < HEAD_DIM, yr, y).astype(BF16)
    vat_ref[0] = y.T[HEAD_DIM:2 * HEAD_DIM, :].astype(BF16)
    for s in range(2):
        iqt_ref[0, _slab(s), :] = _rope(p[:, _slab(3 + s)], c32, s32, 16).T.astype(BF16)
    y = p[:, _slab(5)]
    ik_ref[...] = jnp.where(lane < IDX_DIM, _rope(y, c32, s32, 16), 0.0).astype(BF16)
    iwt_ref[0] = (y * (IDX_HEADS * IDX_DIM) ** -0.5).T[IDX_DIM:IDX_DIM + IDX_HEADS, :]

    p = _mm(h, w_ref[:, C_CQ:C_QC])
    cq = p[:, 0:256]
    cqn = cq * lax.rsqrt(jnp.mean(cq * cq, axis=-1, keepdims=True) + EPS) * gcq_ref[...]
    qb = _mm(cqn.astype(BF16), wuq_ref[...])
    for s in range(B_HEADS):
        y = _rope(_seg_norm(qb[:, _slab(s)], mqb, grows_ref[2:3, :]), cb, sb, 16)
        qbt_ref[0, _slab(s), :] = (y * (LOG2E * (B_NOPE + B_ROPE) ** -0.5)).T.astype(BF16)
    ckv = p[:, 256:384]
    ckvn = ckv * lax.rsqrt(jnp.mean(ckv * ckv, axis=-1, keepdims=True) + EPS) * grows_ref[7:8, :]
    kvb = _mm(ckvn.astype(BF16), wukv_ref[...])
    krs = p[:, 384:512]
    kr = krs * lax.rsqrt(jnp.sum(krs * krs, axis=-1, keepdims=True) * (1.0 / B_ROPE) + EPS) * grows_ref[4:5, :]
    kr = _rope(kr, cb, sb, 16)
    for s in range(B_HEADS):
        kb_ref[:, _slab(s)] = (_seg_norm(kvb[:, _slab(s)], m64, grows_ref[3:4, :]) + kr).astype(BF16)
    for s in range(2):
        vbt_ref[0, _slab(s), :] = kvb[:, 512 + s * LANES:512 + (s + 1) * LANES].T.astype(BF16)

    p = _mm(h, w_ref[:, C_QC:C_G])
    for s in range(4):
        y = _rope(_seg_norm(p[:, _slab(s)], m64, grows_ref[5:6, :]), c64, s64, 32)
        qc_ref[:, _slab(s)] = (y * HEAD_DIM ** -0.5).astype(BF16)
    y = _rope(_seg_norm(p[:, _slab(4)], m64, grows_ref[6:7, :]), c64, s64, 32)
    kc_ref[...] = y.astype(BF16)
    vc_ref[...] = p[:, _slab(5)].astype(BF16)

    for c in range(3):
        lo = C_G + c * 1024
        gt_ref[:, c * 1024:(c + 1) * 1024] = _sigmoid(_mm(h, w_ref[:, lo:lo + 1024]))


def _proj(xf, g, w_r, wuq, wukv, gcq, grows, mseg, tabs, tm=256):
    n, d = xf.shape
    row = lambda w: pl.BlockSpec((tm, w), lambda i: (i, 0))
    full = lambda a: pl.BlockSpec(a.shape, lambda i: (0,) * a.ndim)
    outs = [("t", 256, BF16), ("r", 128, BF16), ("t", HEAD_DIM, BF16), ("t", 256, BF16), ("r", 128, BF16),
            ("t", IDX_HEADS, F32), ("t", 512, BF16), ("r", 512, BF16), ("t", 256, BF16),
            ("r", 512, BF16), ("r", 128, BF16), ("r", 128, BF16), ("r", 3072, F32)]
    shape = lambda k, w, dt: jax.ShapeDtypeStruct((n // tm, w, tm) if k == "t" else (n, w), dt)
    spec = lambda k, w: pl.BlockSpec((1, w, tm), lambda i: (i, 0, 0)) if k == "t" else row(w)
    return pl.pallas_call(
        _proj_kernel,
        out_shape=tuple(shape(*o) for o in outs),
        grid=(n // tm,),
        in_specs=[row(d), full(g), full(w_r), full(wuq), full(wukv), full(gcq), full(grows), full(mseg)]
        + [row(LANES)] * 4,
        out_specs=tuple(spec(k, w) for k, w, _ in outs),
        compiler_params=_params(("arbitrary",)), name="in_proj",
    )(xf, g, w_r, wuq, wukv, gcq, grows, mseg, *tabs)


def _flash_logits(ks, qts):
    return tuple(_mm(k, qt) for k, qt in zip(ks, qts))


def _flash_update(ss, vts, carries, bias=None):
    if bias is not None:
        ss = [s + bias for s in ss]
    m_new = [jnp.maximum(c[0], jnp.max(s, axis=0, keepdims=True)) for c, s in zip(carries, ss)]
    ps = [jnp.exp2(s - m) for s, m in zip(ss, m_new)]
    pvs = [_mm(vt, p.astype(BF16)) for vt, p in zip(vts, ps)]
    out = []
    for (m, l, acc), mn, p, pv in zip(carries, m_new, ps, pvs):
        alpha = jnp.exp2(m - mn)
        out.append((mn, alpha * l + jnp.sum(p, axis=0, keepdims=True), alpha * acc + pv))
    return tuple(out)


def _flash_init(dv, tq):
    return jnp.full((1, tq), NEG_BIG, F32), jnp.zeros((1, tq), F32), jnp.zeros((dv, tq), F32)


def _flash_out(results):
    return jnp.concatenate([acc / l for _, l, acc in results], axis=0).T.astype(BF16)


def _dsa_kernel(iqt_ref, iwt_ref, qat_ref, ik_ref, kv_ref, vat_ref, o_ref, keys_ref, hi_ref, lo_ref, gmax_ref, *,
                tq, nsel, seq):
    ck = tq
    i = pl.program_id(1)
    nk = i + 1
    iqt = iqt_ref[0]
    iwt = iwt_ref[0]
    half = ck // 2
    qpos_h = lax.broadcasted_iota(I32, (half, tq), 1) + i * tq
    krow_h = lax.broadcasted_iota(I32, (half, tq), 0)
    krow = lax.broadcasted_iota(I32, (SUB, tq), 0)
    gmax_ref[...] = jnp.full(gmax_ref.shape, -jnp.inf, F32)

    def to_key(score):
        bits = lax.bitcast_convert_type(score, I32)
        return bits ^ ((bits >> 31) & 0x7FFFFFFF)

    def score_body(c, carry):
        for r in range(ck // half):
            rows = pl.ds(pl.multiple_of(c * ck + r * half, half), half)
            ikc = ik_ref[rows, :][:, 0:IDX_DIM]
            acc = jnp.zeros((half, tq), F32)
            for h in range(IDX_HEADS):
                sc = _mm(ikc, iqt[h * IDX_DIM:(h + 1) * IDX_DIM, :])
                acc = acc + jnp.maximum(sc, 0.0) * iwt[h:h + 1, :]
            causal = krow_h + (c * ck + r * half) <= qpos_h
            key = jnp.where(causal, to_key(acc), INT_MIN)
            keys_ref[c, r * half:(r + 1) * half, :] = key
            hi_ref[c, r * half:(r + 1) * half, :] = (key >> 16).astype(jnp.int16)
            lo_ref[c, r * half:(r + 1) * half, :] = (((key ^ 0x8000) << 16) >> 16).astype(jnp.int16)
            gmax_ref[r * half:(r + 1) * half, :] = jnp.maximum(gmax_ref[r * half:(r + 1) * half, :],
                                                               jnp.where(causal, acc, -jnp.inf))
        return carry

    lax.fori_loop(0, nk, score_body, 0)

    def rep(row):
        return jnp.broadcast_to(row, (SUB, tq))

    def tile(v):
        return jnp.concatenate([v] * (ck // SUB), axis=0)

    def count(preds, *ops):
        nacc = 4 // len(preds)

        def body(c, accs):
            accs = [list(a) for a in accs]
            for g in range(ck // SUB):
                k = keys_ref[c, g * SUB:(g + 1) * SUB, :]
                for a, pred in zip(accs, preds):
                    a[g % nacc] = a[g % nacc] + jnp.where(pred(k, *ops), 1.0, 0.0)
            return tuple(tuple(a) for a in accs)
        accs = lax.fori_loop(0, nk, body, ((jnp.zeros((SUB, tq), F32),) * nacc,) * len(preds))
        return [rep(jnp.sum(sum(a[1:], a[0]), axis=0, keepdims=True)) for a in accs]

    gm = gmax_ref[...]
    smin = rep(jnp.min(gm, axis=0, keepdims=True))
    lo0 = jnp.where(smin == -jnp.inf, INT_MIN, to_key(smin))
    hi0 = to_key(rep(jnp.max(gm, axis=0, keepdims=True)))
    n_pos, n_nonneg = count([lambda k: k > 0, lambda k: k >= 0])
    at_zero = (n_pos < nsel) & (n_nonneg >= nsel)
    lo0 = jnp.where(at_zero, 0, jnp.where(n_pos >= nsel, jnp.maximum(lo0, 1), lo0))
    hi0 = jnp.where(at_zero, 0, jnp.where(n_nonneg < nsel, jnp.minimum(hi0, -1), hi0))
    unknown = float(2 * seq)
    low16 = -2 ** 15

    def pack16(v):
        return jnp.concatenate([v, v], axis=0).astype(jnp.int16)

    def count16(ref, t):
        t16 = pack16(t)
        nacc, rows = 4, 2 * SUB

        def body(c, accs):
            accs = list(accs)
            for g in range(ck // rows):
                hit = ref[c, g * rows:(g + 1) * rows, :] >= t16
                accs[g % nacc] = accs[g % nacc] + jnp.where(hit, jnp.int16(1), jnp.int16(0))
            return tuple(accs)
        accs = lax.fori_loop(0, nk, body, (jnp.zeros((rows, tq), jnp.int16),) * nacc)
        total = sum(a.astype(F32) for a in accs)
        return rep(jnp.sum(total, axis=0, keepdims=True))

    def bisect16(ref, lo, hi, above, want, inert):
        def cond(state):
            return (state[0] < 18) & (state[6] < 0.5)

        def body(state):
            it, lo, hi, n_lo, n_above, exact, _ = state
            all_done = jnp.min(jnp.where(lo == hi, 1.0, 0.0))
            mid = lo + ((hi - lo + 1) >> 1)
            cnt = count16(ref, mid)
            up = cnt >= want
            hit = (cnt == want) & (inert < 0.5)
            lo = jnp.where(up, mid, lo)
            n_lo = jnp.where(up, cnt, n_lo)
            n_above = jnp.where(up, n_above, cnt)
            hi = jnp.where(hit, mid, jnp.where(up, hi, mid - 1))
            exact = jnp.where(hit, 1.0, exact)
            return it + 1, lo, hi, n_lo, n_above, exact, all_done

        zero = jnp.zeros((SUB, tq), F32)
        out = lax.while_loop(cond, body, (jnp.int32(0), lo, hi, zero + unknown, above, zero, jnp.float32(0.0)))
        return out[1], out[3], out[4], out[5]

    zeros = jnp.zeros((SUB, tq), F32)
    top, _, n_above, exact1 = bisect16(hi_ref, lo0 >> 16, hi0 >> 16, jnp.where(n_nonneg < nsel, n_nonneg, 0.0),
                                       float(nsel), zeros)
    done1 = at_zero | (exact1 > 0.5)
    top16 = pack16(top)

    def mask_body(c, carry):
        for g in range(ck // (2 * SUB)):
            rows = slice(g * 2 * SUB, (g + 1) * 2 * SUB)
            lo_ref[c, rows, :] = jnp.where(hi_ref[c, rows, :] == top16, lo_ref[c, rows, :], jnp.int16(low16))
        return carry

    lax.fori_loop(0, nk, mask_body, 0)
    full = jnp.full((SUB, tq), low16, I32)
    bottom, n_bottom, _, _ = bisect16(lo_ref, full, jnp.where(done1, low16, -low16 - 1), zeros, nsel - n_above,
                                      jnp.where(done1, 1.0, 0.0))
    thr = jnp.where(at_zero, 0, jnp.where(done1, top << 16, (top << 16) + (bottom - low16)))
    n_thr = jnp.where(at_zero, n_nonneg, jnp.where(done1, float(nsel), n_above + n_bottom))

    tied = (n_thr != nsel) & (thr != INT_MIN)

    @pl.when(jnp.max(jnp.where(tied, 1.0, 0.0)) > 0.0)
    def _():
        n_above, = count([lambda k, t: k > t], thr)
        keep_t = tile(nsel - n_above)
        thr_t = tile(thr)
        tied_t = tile(tied)
        tri = jnp.where(lax.broadcasted_iota(I32, (ck, ck), 1) <= lax.broadcasted_iota(I32, (ck, ck), 0),
                        1.0, 0.0).astype(BF16)

        def strike(c, seen):
            key = keys_ref[c]
            tie = (key == thr_t) & tied_t
            rank = _mm(tri, jnp.where(tie, 1.0, 0.0).astype(BF16)) + tile(seen)
            keys_ref[c] = jnp.where(tie & (rank > keep_t), INT_MIN, key)
            return rep(rank[ck - 1:ck, :])
        lax.fori_loop(0, nk, strike, jnp.zeros((SUB, tq), F32))

    qat = qat_ref[0]
    sel_t = tile(jnp.where(thr == INT_MIN, INT_MIN + 1, thr))
    qts = [qat[h * HEAD_DIM:(h + 1) * HEAD_DIM, :] for h in range(A_HEADS)]

    def step(c, width, carry):
        k = kv_ref[pl.ds(pl.multiple_of(c * ck, ck), width * ck), :][:, 0:HEAD_DIM]
        vt = jnp.concatenate([vat_ref[c + j] for j in range(width)], axis=1)
        bias = jnp.concatenate([jnp.where(keys_ref[c + j] >= sel_t, 0.0, NEG_BIG) for j in range(width)], axis=0)
        ss = _flash_logits([k] * A_HEADS, qts)
        return _flash_update(ss, [vt] * A_HEADS, carry, bias)

    carry = (_flash_init(HEAD_DIM, tq),) * A_HEADS
    carry = lax.fori_loop(0, nk // 2, lambda c2, cr: step(2 * c2, 2, cr), carry)
    carry = lax.fori_loop((nk // 2) * 2, nk, lambda c, cr: step(c, 1, cr), carry)
    o_ref[...] = _flash_out(carry)


def _dsa(iqt, iwt, qat, ik, kva, vat, b, s, tq=256):
    n = b * s
    nq = s // tq
    nsel = min(TOPK_MAX, s // 4)
    assert nsel <= tq and iqt.shape[2] == tq
    qtile = lambda w: pl.BlockSpec((1, w, tq), lambda bi, i: (bi * nq + i, 0, 0))
    seq_rows = pl.BlockSpec((s, LANES), lambda bi, i: (bi, 0))
    return pl.pallas_call(
        functools.partial(_dsa_kernel, tq=tq, nsel=nsel, seq=s),
        out_shape=jax.ShapeDtypeStruct((n, A_HEADS * HEAD_DIM), BF16), grid=(b, nq),
        in_specs=[qtile(IDX_HEADS * IDX_DIM), qtile(IDX_HEADS), qtile(A_HEADS * HEAD_DIM), seq_rows, seq_rows,
                  pl.BlockSpec((nq, HEAD_DIM, tq), lambda bi, i: (bi, 0, 0))],
        out_specs=pl.BlockSpec((tq, A_HEADS * HEAD_DIM), lambda bi, i: (bi * nq + i, 0)),
        scratch_shapes=[pltpu.VMEM((nq, tq, tq), I32), pltpu.VMEM((nq, tq, tq), jnp.int16),
                        pltpu.VMEM((nq, tq, tq), jnp.int16), pltpu.VMEM((tq, tq), F32)],
        compiler_params=_params(("arbitrary", "arbitrary")), name="dsa_attention",
    )(iqt, iwt, qat, ik, kva, vat)


def _mla_kernel(qt_ref, k_ref, vt_ref, o_ref, *, tq):
    i = pl.program_id(1)
    qt = qt_ref[0]
    kpos = lax.broadcasted_iota(I32, (tq, tq), 0)
    qpos = lax.broadcasted_iota(I32, (tq, tq), 1)
    diag_bias = jnp.where(kpos <= qpos, 0.0, NEG_BIG)

    heads = range(B_HEADS)
    qts = [qt[_slab(h), :] for h in heads]

    def step(c, width, carry, bias=None):
        rows = pl.ds(pl.multiple_of(c * tq, tq), width * tq)
        vt = jnp.concatenate([vt_ref[c + j] for j in range(width)], axis=1)
        ss = _flash_logits([k_ref[rows, _slab(h)] for h in heads], qts)
        return _flash_update(ss, [vt[h * B_V:(h + 1) * B_V, :] for h in heads], carry, bias)

    carry = (_flash_init(B_V, tq),) * B_HEADS
    carry = lax.fori_loop(0, i // 2, lambda c2, cr: step(2 * c2, 2, cr), carry)
    carry = lax.fori_loop((i // 2) * 2, i, lambda c, cr: step(c, 1, cr), carry)
    o_ref[...] = _flash_out(step(i, 1, carry, diag_bias))


def _mla(qbt, kb, vbt, b, s, tq=256):
    n = b * s
    nq = s // tq
    assert qbt.shape[2] == tq
    return pl.pallas_call(
        functools.partial(_mla_kernel, tq=tq),
        out_shape=jax.ShapeDtypeStruct((n, B_HEADS * B_V), BF16), grid=(b, nq),
        in_specs=[pl.BlockSpec((1, B_HEADS * LANES, tq), lambda bi, i: (bi * nq + i, 0, 0)),
                  pl.BlockSpec((s, B_HEADS * LANES), lambda bi, i: (bi, 0)),
                  pl.BlockSpec((nq, B_HEADS * B_V, tq), lambda bi, i: (bi, 0, 0))],
        out_specs=pl.BlockSpec((tq, B_HEADS * B_V), lambda bi, i: (bi * nq + i, 0)),
        compiler_params=_params(("arbitrary", "arbitrary")), name="mla_attention",
    )(qbt, kb, vbt)


def _swa_kernel(sink_ref, q_ref, kp_ref, kc_ref, vp_ref, vc_ref, o_ref, *, tq):
    i = pl.program_id(1)
    q = q_ref[...]
    keys = jnp.concatenate([kp_ref[...], kc_ref[...]], axis=0)
    vals = jnp.concatenate([vp_ref[...], vc_ref[...]], axis=0)
    nkeys = WINDOW + tq
    qpos = lax.broadcasted_iota(I32, (tq, nkeys), 0) + i * tq
    kpos = lax.broadcasted_iota(I32, (tq, nkeys), 1) + i * tq - WINDOW
    visible = (kpos > qpos - WINDOW) & (kpos <= qpos) & (kpos >= 0)
    bias = jnp.where(visible, 0.0, NEG_BIG)
    group = C_HEADS // C_KV_HEADS
    heads = range(C_HEADS)
    ks = [keys[:, j * HEAD_DIM:(j + 1) * HEAD_DIM] for j in range(C_KV_HEADS)]
    vs = [vals[:, j * HEAD_DIM:(j + 1) * HEAD_DIM] for j in range(C_KV_HEADS)]
    ss = [_mm_nt(q[:, h * HEAD_DIM:(h + 1) * HEAD_DIM], ks[h // group]) + bias for h in heads]
    ms = [jnp.maximum(jnp.max(s, axis=1, keepdims=True), sink_ref[h]) for h, s in zip(heads, ss)]
    ps = [jnp.exp(s - m) for s, m in zip(ss, ms)]
    ws = [p / (jnp.sum(p, axis=1, keepdims=True) + jnp.exp(sink_ref[h] - m)) for h, p, m in zip(heads, ps, ms)]
    outs = [_mm(w.astype(BF16), vs[h // group]) for h, w in zip(heads, ws)]
    o_ref[...] = jnp.concatenate(outs, axis=1).astype(BF16)


def _swa(sinks, qc, kc, vc, b, s, tq=256):
    n = b * s
    nq = s // tq
    per = tq // WINDOW
    qrow = lambda w: pl.BlockSpec((tq, w), lambda bi, i: (bi * nq + i, 0))
    prev = pl.BlockSpec((WINDOW, LANES), lambda bi, i: (jnp.maximum((bi * nq + i) * per - 1, 0), 0))
    return pl.pallas_call(
        functools.partial(_swa_kernel, tq=tq),
        out_shape=jax.ShapeDtypeStruct((n, C_HEADS * HEAD_DIM), BF16), grid=(b, nq),
        in_specs=[pl.BlockSpec(memory_space=pltpu.SMEM), qrow(512), prev, qrow(LANES), prev, qrow(LANES)],
        out_specs=qrow(512),
        compiler_params=_params(("arbitrary", "arbitrary")), name="swa_attention",
    )(sinks, qc, kc, kc, vc, vc)


def _merge_kernel(ya_ref, yb_ref, yc_ref, gt_ref, x_ref, wa_ref, wb_ref, wc_ref, wo_ref, g_ref,
                  xo_ref, h_ref):
    d = x_ref.shape[1]
    merged = (gt_ref[:, 0:d] * _mm(ya_ref[...], wa_ref[...])
              + gt_ref[:, d:2 * d] * _mm(yb_ref[...], wb_ref[...])
              + gt_ref[:, 2 * d:3 * d] * _mm(yc_ref[...], wc_ref[...]))
    xn = x_ref[...] + _mm(merged.astype(BF16), wo_ref[...])
    xo_ref[...] = xn
    h = xn * lax.rsqrt(jnp.mean(xn * xn, axis=-1, keepdims=True) + EPS) * g_ref[...]
    h_ref[...] = h.astype(h_ref.dtype)


def _merge(ya, yb, yc, gt, xf, wa, wb, wc, wo, g, h_dtype, tm=256):
    n, d = xf.shape
    row = lambda w: pl.BlockSpec((tm, w), lambda i: (i, 0))
    full = lambda a: pl.BlockSpec(a.shape, lambda i: (0,) * a.ndim)
    return pl.pallas_call(
        _merge_kernel,
        out_shape=(jax.ShapeDtypeStruct((n, d), F32), jax.ShapeDtypeStruct((n, d), h_dtype)),
        grid=(n // tm,),
        in_specs=[row(256), row(256), row(512), row(3 * d), row(d), full(wa), full(wb), full(wc), full(wo),
                  full(g)],
        out_specs=(row(d), row(d)),
        compiler_params=_params(("arbitrary",)), name="merge_out_proj",
    )(ya, yb, yc, gt, xf, wa, wb, wc, wo, g)


def _ffn_kernel(h_ref, x_ref, w1_hbm, w3_hbm, w2_hbm, o_ref, w1_ref, w3_ref, w2_ref, sem, *, chunk):
    @pl.when(pl.program_id(0) == 0)
    def _():
        copies = [pltpu.make_async_copy(src, dst, sem.at[j]) for j, (src, dst) in
                  enumerate(((w1_hbm, w1_ref), (w3_hbm, w3_ref), (w2_hbm, w2_ref)))]
        for cp in copies:
            cp.start()
        for cp in copies:
            cp.wait()

    h = h_ref[...]
    acc = x_ref[...]
    for j in range(w1_ref.shape[1] // chunk):
        cols = slice(j * chunk, (j + 1) * chunk)
        a = _mm(h, w1_ref[:, cols])
        mid = a * _sigmoid(a) * _mm(h, w3_ref[:, cols])
        acc = acc + _mm(mid.astype(BF16), w2_ref[cols, :])
    o_ref[...] = acc


def _ffn(h2, xf, w1, w3, w2, tm=256, chunk=1408):
    n, d = xf.shape
    row = lambda w: pl.BlockSpec((tm, w), lambda i: (i, 0))
    hbm = pl.BlockSpec(memory_space=pl.ANY)
    return pl.pallas_call(
        functools.partial(_ffn_kernel, chunk=chunk),
        out_shape=jax.ShapeDtypeStruct((n, d), F32), grid=(n // tm,),
        in_specs=[row(d), row(d), hbm, hbm, hbm], out_specs=row(d),
        scratch_shapes=[pltpu.VMEM(w1.shape, BF16), pltpu.VMEM(w3.shape, BF16), pltpu.VMEM(w2.shape, BF16),
                        pltpu.SemaphoreType.DMA((3,))],
        compiler_params=_params(("arbitrary",)), name="dense_swiglu",
    )(h2, xf, w1, w3, w2)


def _route_kernel(h_ref, wr_ref, tri_ref, route_ref, cnt_ref, carry_ref):
    @pl.when(pl.program_id(0) == 0)
    def _():
        carry_ref[...] = jnp.zeros_like(carry_ref)

    logits = jnp.dot(h_ref[...], wr_ref[...], precision=lax.Precision.HIGHEST, preferred_element_type=F32)
    lane = lax.broadcasted_iota(I32, logits.shape, 1).astype(F32)
    lg = jnp.where(lane < N_EXPERTS, logits, -jnp.inf)
    m1 = jnp.max(lg, axis=1, keepdims=True)
    e1 = jnp.min(jnp.where(lg == m1, lane, float(LANES)), axis=1, keepdims=True)
    lg2 = jnp.where(lane == e1, -jnp.inf, lg)
    m2 = jnp.max(lg2, axis=1, keepdims=True)
    e2 = jnp.min(jnp.where(lg2 == m2, lane, float(LANES)), axis=1, keepdims=True)
    ex = jnp.exp(m2 - m1)
    g1 = 1.0 / (1.0 + ex)
    g2 = ex / (1.0 + ex)
    onehot = jnp.where((lane == e1) | (lane == e2), 1.0, 0.0)
    before = _mm(tri_ref[...], onehot.astype(BF16)) + carry_ref[0:1, :]
    r1 = jnp.sum(jnp.where(lane == e1, before, 0.0), axis=1, keepdims=True)
    r2 = jnp.sum(jnp.where(lane == e2, before, 0.0), axis=1, keepdims=True)
    out = jnp.zeros_like(logits)
    for idx, val in enumerate((e1, e2, g1, g2, r1, r2)):
        out = jnp.where(lane == idx, val, out)
    route_ref[...] = out
    total = carry_ref[0:1, :] + jnp.sum(onehot, axis=0, keepdims=True)
    carry_ref[...] = jnp.broadcast_to(total, carry_ref.shape)
    cnt_ref[...] = jnp.broadcast_to(total, cnt_ref.shape)


def _route(h2, wr, tm=256):
    n, d = h2.shape
    tri = jnp.asarray(np.tril(np.ones((tm, tm), np.float32), -1), BF16)
    full = lambda a: pl.BlockSpec(a.shape, lambda i: (0,) * a.ndim)
    return pl.pallas_call(
        _route_kernel,
        out_shape=(jax.ShapeDtypeStruct((n, LANES), F32), jax.ShapeDtypeStruct((8, LANES), F32)),
        grid=(n // tm,),
        in_specs=[pl.BlockSpec((tm, d), lambda i: (i, 0)), full(wr), full(tri)],
        out_specs=(pl.BlockSpec((tm, LANES), lambda i: (i, 0)), pl.BlockSpec((8, LANES), lambda i: (0, 0))),
        scratch_shapes=[pltpu.VMEM((8, LANES), F32)],
        compiler_params=_params(("arbitrary",)), name="moe_route",
    )(h2, wr, tri)


def _scatter_kernel(dest_ref, h_ref, xin_hbm, xout_hbm, sem, *, tt):
    del xin_hbm

    def copy(r, k):
        d = dest_ref[0, 0, 2 * r + k]
        return pltpu.make_async_copy(h_ref.at[pl.ds(r, 1), :], xout_hbm.at[pl.ds(d, 1), :], sem)

    def start(r, carry):
        copy(r, 0).start()
        copy(r, 1).start()
        return carry

    def wait(r, carry):
        copy(r, 0).wait()
        copy(r, 1).wait()
        return carry

    lax.fori_loop(0, tt, start, 0)
    lax.fori_loop(0, tt, wait, 0)


def _scatter(dest3, h2, rows, tt):
    n, w = h2.shape
    hbm = pl.BlockSpec(memory_space=pl.ANY)
    return pl.pallas_call(
        functools.partial(_scatter_kernel, tt=tt),
        out_shape=jax.ShapeDtypeStruct((rows, w), h2.dtype), grid=(n // tt,),
        in_specs=[pl.BlockSpec((1, 1, 2 * tt), lambda i: (i, 0, 0), memory_space=pltpu.SMEM),
                  pl.BlockSpec((tt, w), lambda i: (i, 0)), hbm],
        out_specs=hbm, scratch_shapes=[pltpu.SemaphoreType.DMA(())],
        input_output_aliases={2: 0},
        compiler_params=_params(("arbitrary",)), name="moe_scatter",
    )(dest3, h2, jnp.zeros((rows, w), h2.dtype))


def _expert_kernel(be_ref, nv_ref, x_ref, w1_hbm, w3_hbm, w2_hbm, y_ref, w1_ref, w3_ref, w2_ref, sem, *, chunk):
    blk = pl.program_id(0)
    e = be_ref[blk]

    @pl.when((blk == 0) | (e != be_ref[jnp.maximum(blk - 1, 0)]))
    def _():
        copies = [pltpu.make_async_copy(src.at[e], dst, sem.at[j]) for j, (src, dst) in
                  enumerate(((w1_hbm, w1_ref), (w3_hbm, w3_ref), (w2_hbm, w2_ref)))]
        for cp in copies:
            cp.start()
        for cp in copies:
            cp.wait()

    @pl.when(blk < nv_ref[0])
    def _():
        x = x_ref[...].astype(BF16)
        acc = jnp.zeros(y_ref.shape, F32)
        for j in range(w1_ref.shape[1] // chunk):
            cols = slice(j * chunk, (j + 1) * chunk)
            a = _mm(x, w1_ref[:, cols])
            mid = a * _sigmoid(a) * _mm(x, w3_ref[:, cols])
            acc = acc + _mm(mid.astype(BF16), w2_ref[cols, :])
        y_ref[...] = acc

    @pl.when(blk >= nv_ref[0])
    def _():
        y_ref[...] = jnp.zeros(y_ref.shape, F32)


def _experts(block_e, nvalid, xbuf, w1, w3, w2, chunk=512):
    rows, d = xbuf.shape
    hbm = pl.BlockSpec(memory_space=pl.ANY)
    blk = pl.BlockSpec((MOE_BLOCK, d), lambda i, be, nv: (i, 0))
    return pl.pallas_call(
        functools.partial(_expert_kernel, chunk=chunk),
        out_shape=jax.ShapeDtypeStruct((rows, d), F32),
        grid_spec=pltpu.PrefetchScalarGridSpec(
            num_scalar_prefetch=2, grid=(rows // MOE_BLOCK,),
            in_specs=[blk, hbm, hbm, hbm], out_specs=blk,
            scratch_shapes=[pltpu.VMEM(w1.shape[1:], BF16), pltpu.VMEM(w3.shape[1:], BF16),
                            pltpu.VMEM(w2.shape[1:], BF16), pltpu.SemaphoreType.DMA((3,))]),
        compiler_params=_params(("arbitrary",)), name="moe_experts",
    )(block_e, nvalid, xbuf, w1, w3, w2)


def _combine_kernel(dest_ref, y_hbm, x_ref, route_ref, o_ref, buf_ref, sem, *, tt):
    def copy(r, k):
        d = dest_ref[0, 0, 2 * r + k]
        return pltpu.make_async_copy(y_hbm.at[pl.ds(d, 1), :], buf_ref.at[k, pl.ds(r, 1), :], sem)

    def start(r, carry):
        copy(r, 0).start()
        copy(r, 1).start()
        return carry

    def wait(r, carry):
        copy(r, 0).wait()
        copy(r, 1).wait()
        return carry

    lax.fori_loop(0, tt, start, 0)
    lax.fori_loop(0, tt, wait, 0)
    rt = route_ref[...]
    o_ref[...] = x_ref[...] + (buf_ref[0] * rt[:, 2:3] + buf_ref[1] * rt[:, 3:4])


def _combine(dest3, ybuf, xf, route, tt):
    n, d = xf.shape
    row = lambda w: pl.BlockSpec((tt, w), lambda i: (i, 0))
    return pl.pallas_call(
        functools.partial(_combine_kernel, tt=tt),
        out_shape=jax.ShapeDtypeStruct((n, d), F32), grid=(n // tt,),
        in_specs=[pl.BlockSpec((1, 1, 2 * tt), lambda i: (i, 0, 0), memory_space=pltpu.SMEM),
                  pl.BlockSpec(memory_space=pl.ANY), row(d), row(LANES)],
        out_specs=row(d),
        scratch_shapes=[pltpu.VMEM((2, tt, d), F32), pltpu.SemaphoreType.DMA(())],
        compiler_params=_params(("arbitrary",)), name="moe_combine",
    )(dest3, ybuf, xf, route)


def _moe(xf, h2, w_router, w1, w3, w2, tt=256):
    n, d = xf.shape
    wr = jnp.pad(w_router, ((0, 0), (0, LANES - N_EXPERTS)))
    route, cnt = _route(h2, wr)
    counts = cnt[0, :N_EXPERTS].astype(I32)
    padded = (counts + MOE_BLOCK - 1) // MOE_BLOCK * MOE_BLOCK
    pend = jnp.cumsum(padded)
    pstart = pend - padded
    dest = pstart[route[:, 0:2].astype(I32)] + route[:, 4:6].astype(I32)
    dest3 = dest.reshape(n // tt, 1, 2 * tt)
    n_blocks = n * 2 // MOE_BLOCK + N_EXPERTS
    rows = n_blocks * MOE_BLOCK
    block_e = jnp.minimum(jnp.searchsorted(pend, jnp.arange(n_blocks) * MOE_BLOCK, side='right'),
                          N_EXPERTS - 1).astype(I32)
    nvalid = (pend[-1:] // MOE_BLOCK).astype(I32)
    xbuf = _scatter(dest3, h2, rows, tt)
    ybuf = _experts(block_e, nvalid, xbuf, w1, w3, w2)
    return _combine(dest3, ybuf, xf, route, tt)


def _tile_row(v, width=LANES):
    v = v.astype(F32)
    return jnp.tile(v, width // v.shape[0])


def _layer_params(l, attn_norm_g, w_in, a_q_norm_g, a_k_norm_g, b_cq_norm_g, b_ckv_norm_g, b_w_uq, b_w_ukv,
                  b_qn_g, b_qr_g, b_kn_g, b_kr_g, c_q_norm_g, c_k_norm_g):
    d = w_in.shape[1]
    sizes = (A_HEADS * HEAD_DIM, HEAD_DIM, HEAD_DIM, IDX_HEADS * IDX_DIM, IDX_DIM, IDX_HEADS,
             b_w_uq.shape[1], b_w_ukv.shape[1], B_ROPE, C_HEADS * HEAD_DIM, C_KV_HEADS * HEAD_DIM,
             C_KV_HEADS * HEAD_DIM, 3 * d)
    qa, ka, va, iq, ik, iw, cq, ckv, kr, qc, kc, vc, gates = jnp.split(w_in[l], np.cumsum(sizes)[:-1].tolist(), axis=1)
    z = lambda k: jnp.zeros((d, k), F32)
    w_r = jnp.concatenate([qa, ka, va, iq, ik, iw, z(LANES - IDX_DIM - IDX_HEADS), cq, ckv,
                           z(B_NOPE), kr, z(LANES - B_NOPE - B_ROPE), qc, kc, vc, gates], axis=1).astype(BF16)
    uq = b_w_uq[l].reshape(-1, B_HEADS, B_NOPE + B_ROPE)
    uq = jnp.pad(uq, ((0, 0), (0, 0), (0, LANES - B_NOPE - B_ROPE))).reshape(-1, B_HEADS * LANES).astype(BF16)
    ukv = b_w_ukv[l].reshape(-1, B_HEADS, B_NOPE + B_V)
    uk = jnp.pad(ukv[:, :, :B_NOPE], ((0, 0), (0, 0), (0, LANES - B_NOPE))).reshape(-1, B_HEADS * LANES)
    uv = ukv[:, :, B_NOPE:].reshape(-1, B_HEADS * B_V)
    wukv = jnp.concatenate([uk, uv], axis=1).astype(BF16)
    zeros = lambda k: jnp.zeros((k,), F32)
    grows = jnp.stack([
        _tile_row(a_q_norm_g[l]),
        jnp.concatenate([a_k_norm_g[l], jnp.ones((HEAD_DIM,), F32)]),
        jnp.concatenate([b_qn_g[l], b_qr_g[l], zeros(LANES - B_NOPE - B_ROPE)]),
        jnp.concatenate([b_kn_g[l], zeros(LANES - B_NOPE)]),
        jnp.concatenate([zeros(B_NOPE), b_kr_g[l], zeros(LANES - B_NOPE - B_ROPE)]),
        _tile_row(c_q_norm_g[l]),
        _tile_row(c_k_norm_g[l]),
        b_ckv_norm_g[l],
    ]).astype(F32)
    return (attn_norm_g[l][None, :], w_r, uq, wukv, b_cq_norm_g[l][None, :], grows)


def _seg_matrices():
    lane = np.arange(LANES)
    g64 = lane // 64
    m64 = (g64[:, None] == g64[None, :]) / 64.0
    gqb = np.where(lane < B_NOPE, 0, np.where(lane < B_NOPE + B_ROPE, 1, 2))
    size = np.where(lane < B_NOPE, B_NOPE, B_ROPE)
    mqb = (gqb[:, None] == gqb[None, :]) / size[None, :]
    return jnp.asarray(np.stack([m64, mqb]), BF16)


def kernel(x, positions, attn_norm_g, w_in, a_q_norm_g, a_k_norm_g, b_cq_norm_g, b_ckv_norm_g, b_w_uq, b_w_ukv,
           b_qn_g, b_qr_g, b_kn_g, b_kr_g, c_q_norm_g, c_k_norm_g, c_sinks, w_a_out, w_b_out, w_c_out, w_o,
           ffn_norm_g, ffn_w1, ffn_w3, ffn_w2, router_w, moe_w1, moe_w3, moe_w2):
    b, s, d = x.shape
    n = b * s
    depth = w_in.shape[0]
    xf = x.reshape(n, d)
    tabs = _rope_tables(positions.reshape(n, 1).astype(F32))
    mseg = _seg_matrices()
    for l in range(depth):
        g, w_r, uq, wukv, gcq, grows = _layer_params(
            l, attn_norm_g, w_in, a_q_norm_g, a_k_norm_g, b_cq_norm_g, b_ckv_norm_g, b_w_uq, b_w_ukv,
            b_qn_g, b_qr_g, b_kn_g, b_kr_g, c_q_norm_g, c_k_norm_g)
        qat, kva, vat, iqt, ik, iwt, qbt, kb, vbt, qc, kc, vc, gt = _proj(xf, g, w_r, uq, wukv, gcq, grows, mseg,
                                                                         tabs)
        ya = _dsa(iqt, iwt, qat, ik, kva, vat, b, s)
        yb = _mla(qbt, kb, vbt, b, s)
        yc = _swa(c_sinks[l], qc, kc, vc, b, s)
        dense = l % 2 == 0
        xf, h2 = _merge(ya, yb, yc, gt, xf, w_a_out[l].astype(BF16), w_b_out[l].astype(BF16),
                        w_c_out[l].astype(BF16), w_o[l].astype(BF16), ffn_norm_g[l][None, :],
                        BF16 if dense else F32)
        if dense:
            xf = _ffn(h2, xf, ffn_w1[l // 2].astype(BF16), ffn_w3[l // 2].astype(BF16), ffn_w2[l // 2].astype(BF16))
        else:
            xf = _moe(xf, h2, router_w[l // 2], moe_w1[l // 2].astype(BF16),
                      moe_w3[l // 2].astype(BF16), moe_w2[l // 2].astype(BF16))
    return xf.reshape(b, s, d)
```

```python
import functools

import numpy as np
import jax
import jax.numpy as jnp
from jax import lax
from jax.experimental import pallas as pl
from jax.experimental.pallas import tpu as pltpu

F32, BF16, I32 = jnp.float32, jnp.bfloat16, jnp.int32

EPS = 1e-6
ROPE_THETA = 10000.0
HEAD_DIM = 64
A_HEADS = 4
IDX_HEADS = 8
IDX_DIM = 32
TOPK_MAX = 256
B_HEADS = 4
B_NOPE = 64
B_ROPE = 32
B_V = 64
C_HEADS = 8
C_KV_HEADS = 2
WINDOW = 128
N_EXPERTS = 8
MOE_BLOCK = 256

LANES = 128
SUB = 8
VMEM_LIMIT = 56 * 1024 * 1024
INT_MIN = -2 ** 31
NEG_BIG = -1e30
LOG2E = 1.4426950408889634

C_QA, C_KVA, C_IQ, C_IKW, C_CQ, C_CKV, C_KR, C_QC, C_KC, C_VC, C_G = (
    0, 256, 384, 640, 768, 1024, 1152, 1280, 1792, 1920, 2048)


def _params(sem):
    return pltpu.CompilerParams(dimension_semantics=sem, vmem_limit_bytes=VMEM_LIMIT)


def _mm(a, b):
    return jnp.dot(a, b, preferred_element_type=F32)


def _mm_nt(a, b):
    return lax.dot_general(a, b, (((1,), (1,)), ((), ())), preferred_element_type=F32)


def _sigmoid(v):
    return 1.0 / (1.0 + jnp.exp(-v))


def _rope_table_kernel(pos_ref, f_ref, sg_ref, c64_ref, s64_ref, c32_ref, s32_ref):
    pos = pos_ref[...]
    a = pos * f_ref[0:1, :]
    c64_ref[...] = jnp.cos(a)
    s64_ref[...] = jnp.sin(a) * sg_ref[0:1, :]
    a = pos * f_ref[1:2, :]
    c32_ref[...] = jnp.cos(a)
    s32_ref[...] = jnp.sin(a) * sg_ref[1:2, :]


def _rope_tables(pos_f, tm=1024):
    n = pos_f.shape[0]
    lane = np.arange(LANES)
    f32 = ROPE_THETA ** (-jnp.arange(32, dtype=F32) / 32)
    f16 = ROPE_THETA ** (-jnp.arange(16, dtype=F32) / 16)
    freqs = jnp.stack([f32[lane % 32], f16[lane % 16]])
    signs = jnp.asarray(np.stack([np.where(lane % 64 < 32, -1.0, 1.0),
                                  np.where(lane % 32 < 16, -1.0, 1.0)]), F32)
    tab = jax.ShapeDtypeStruct((n, LANES), F32)
    row = pl.BlockSpec((tm, LANES), lambda i: (i, 0))
    par = pl.BlockSpec((2, LANES), lambda i: (0, 0))
    return pl.pallas_call(
        _rope_table_kernel, out_shape=(tab,) * 4, grid=(n // tm,),
        in_specs=[pl.BlockSpec((tm, 1), lambda i: (i, 0)), par, par],
        out_specs=(row,) * 4, compiler_params=_params(("arbitrary",)), name="rope_tables",
    )(pos_f, freqs, signs)


def _swap_half(y, half):
    lane = lax.broadcasted_iota(I32, y.shape, 1)
    return jnp.where((lane & half) == 0, pltpu.roll(y, LANES - half, 1), pltpu.roll(y, half, 1))


def _rope(y, cos, sin_signed, half):
    return y * cos + _swap_half(y, half) * sin_signed


def _seg_mean_sq(y, mseg):
    sq = y * y
    hi = sq.astype(BF16)
    lo = (sq - hi.astype(F32)).astype(BF16)
    return _mm(hi, mseg) + _mm(lo, mseg)


def _seg_norm(y, mseg, gain):
    return y * lax.rsqrt(_seg_mean_sq(y, mseg) + EPS) * gain


def _slab(s):
    return slice(s * LANES, (s + 1) * LANES)


def _proj_kernel(x_ref, g_ref, w_ref, wuq_ref, wukv_ref, gcq_ref, grows_ref, mseg_ref,
                 c64_ref, s64_ref, c32_ref, s32_ref,
                 qat_ref, kva_ref, vat_ref, iqt_ref, ik_ref, iwt_ref, qbt_ref, kb_ref, vbt_ref,
                 qc_ref, kc_ref, vc_ref, gt_ref):
    x = x_ref[...]
    h = (x * lax.rsqrt(jnp.mean(x * x, axis=-1, keepdims=True) + EPS) * g_ref[...]).astype(BF16)
    c64, s64, c32, s32 = c64_ref[...], s64_ref[...], c32_ref[...], s32_ref[...]
    m64, mqb = mseg_ref[0], mseg_ref[1]
    lane = lax.broadcasted_iota(I32, c64.shape, 1)
    in_rope = (lane >= B_NOPE) & (lane < B_NOPE + B_ROPE)
    cb = jnp.where(in_rope, c32, 1.0)
    sb = jnp.where(in_rope, s32, 0.0)

    p = _mm(h, w_ref[:, C_QA:C_CQ])
    for s in range(2):
        y = _rope(_seg_norm(p[:, _slab(s)], m64, grows_ref[0:1, :]), c64, s64, 32)
        qat_ref[0, _slab(s), :] = (y * (LOG2E * HEAD_DIM ** -0.5)).T.astype(BF16)
    y = p[:, _slab(2)]
    yr = _rope(_seg_norm(y, m64, grows_ref[1:2, :]), c64, s64, 32)
    kva_ref[...] = jnp.where(lane < HEAD_DIM, yr, y).astype(BF16)
    vat_ref[0] = y.T[HEAD_DIM:2 * HEAD_DIM, :].astype(BF16)
    for s in range(2):
        iqt_ref[0, _slab(s), :] = _rope(p[:, _slab(3 + s)], c32, s32, 16).T.astype(BF16)
    y = p[:, _slab(5)]
    ik_ref[...] = jnp.where(lane < IDX_DIM, _rope(y, c32, s32, 16), 0.0).astype(BF16)
    iwt_ref[0] = (y * (IDX_HEADS * IDX_DIM) ** -0.5).T[IDX_DIM:IDX_DIM + IDX_HEADS, :]

    p = _mm(h, w_ref[:, C_CQ:C_QC])
    cq = p[:, 0:256]
    cqn = cq * lax.rsqrt(jnp.mean(cq * cq, axis=-1, keepdims=True) + EPS) * gcq_ref[...]
    qb = _mm(cqn.astype(BF16), wuq_ref[...])
    for s in range(B_HEADS):
        y = _rope(_seg_norm(qb[:, _slab(s)], mqb, grows_ref[2:3, :]), cb, sb, 16)
        qbt_ref[0, _slab(s), :] = (y * (LOG2E * (B_NOPE + B_ROPE) ** -0.5)).T.astype(BF16)
    ckv = p[:, 256:384]
    ckvn = ckv * lax.rsqrt(jnp.mean(ckv * ckv, axis=-1, keepdims=True) + EPS) * grows_ref[7:8, :]
    kvb = _mm(ckvn.astype(BF16), wukv_ref[...])
    krs = p[:, 384:512]
    kr = krs * lax.rsqrt(jnp.sum(krs * krs, axis=-1, keepdims=True) * (1.0 / B_ROPE) + EPS) * grows_ref[4:5, :]
    kr = _rope(kr, cb, sb, 16)
    for s in range(B_HEADS):
        kb_ref[:, _slab(s)] = (_seg_norm(kvb[:, _slab(s)], m64, grows_ref[3:4, :]) + kr).astype(BF16)
    for s in range(2):
        vbt_ref[0, _slab(s), :] = kvb[:, 512 + s * LANES:512 + (s + 1) * LANES].T.astype(BF16)

    p = _mm(h, w_ref[:, C_QC:C_G])
    for s in range(4):
        y = _rope(_seg_norm(p[:, _slab(s)], m64, grows_ref[5:6, :]), c64, s64, 32)
        qc_ref[:, _slab(s)] = (y * HEAD_DIM ** -0.5).astype(BF16)
    y = _rope(_seg_norm(p[:, _slab(4)], m64, grows_ref[6:7, :]), c64, s64, 32)
    kc_ref[...] = y.astype(BF16)
    vc_ref[...] = p[:, _slab(5)].astype(BF16)

    for c in range(3):
        lo = C_G + c * 1024
        gt_ref[:, c * 1024:(c + 1) * 1024] = _sigmoid(_mm(h, w_ref[:, lo:lo + 1024]))


def _proj(xf, g, w_r, wuq, wukv, gcq, grows, mseg, tabs, tm=256):
    n, d = xf.shape
    row = lambda w: pl.BlockSpec((tm, w), lambda i: (i, 0))
    full = lambda a: pl.BlockSpec(a.shape, lambda i: (0,) * a.ndim)
    outs = [("t", 256, BF16), ("r", 128, BF16), ("t", HEAD_DIM, BF16), ("t", 256, BF16), ("r", 128, BF16),
            ("t", IDX_HEADS, F32), ("t", 512, BF16), ("r", 512, BF16), ("t", 256, BF16),
            ("r", 512, BF16), ("r", 128, BF16), ("r", 128, BF16), ("r", 3072, F32)]
    shape = lambda k, w, dt: jax.ShapeDtypeStruct((n // tm, w, tm) if k == "t" else (n, w), dt)
    spec = lambda k, w: pl.BlockSpec((1, w, tm), lambda i: (i, 0, 0)) if k == "t" else row(w)
    return pl.pallas_call(
        _proj_kernel,
        out_shape=tuple(shape(*o) for o in outs),
        grid=(n // tm,),
        in_specs=[row(d), full(g), full(w_r), full(wuq), full(wukv), full(gcq), full(grows), full(mseg)]
        + [row(LANES)] * 4,
        out_specs=tuple(spec(k, w) for k, w, _ in outs),
        compiler_params=_params(("arbitrary",)), name="in_proj",
    )(xf, g, w_r, wuq, wukv, gcq, grows, mseg, *tabs)


def _flash_logits(ks, qts):
    return tuple(_mm(k, qt) for k, qt in zip(ks, qts))


def _flash_update(ss, vts, carries, bias=None):
    if bias is not None:
        ss = [s + bias for s in ss]
    m_new = [jnp.maximum(c[0], jnp.max(s, axis=0, keepdims=True)) for c, s in zip(carries, ss)]
    ps = [jnp.exp2(s - m) for s, m in zip(ss, m_new)]
    pvs = [_mm(vt, p.astype(BF16)) for vt, p in zip(vts, ps)]
    out = []
    for (m, l, acc), mn, p, pv in zip(carries, m_new, ps, pvs):
        alpha = jnp.exp2(m - mn)
        out.append((mn, alpha * l + jnp.sum(p, axis=0, keepdims=True), alpha * acc + pv))
    return tuple(out)


def _walk_chunks(n, step, carry, widths=(4, 2, 1)):
    start = 0
    for w in widths:
        trips = (n - start) // w
        carry = lax.fori_loop(0, trips, lambda j, cr, start=start, w=w: step(start + j * w, w, cr), carry)
        start = start + trips * w
    return carry


def _flash_init(dv, tq):
    return jnp.full((1, tq), NEG_BIG, F32), jnp.zeros((1, tq), F32), jnp.zeros((dv, tq), F32)


def _flash_out(results):
    return jnp.concatenate([acc / l for _, l, acc in results], axis=0).T.astype(BF16)


def _dsa_kernel(iqt_ref, iwt_ref, qat_ref, ik_ref, kv_ref, vat_ref, o_ref, keys_ref, hi_ref, lo_ref, gmax_ref, *,
                tq, nsel, seq):
    ck = tq
    i = pl.program_id(1)
    nk = i + 1
    iqt = iqt_ref[0]
    iwt = iwt_ref[0]
    half = ck // 2
    qpos_h = lax.broadcasted_iota(I32, (half, tq), 1) + i * tq
    krow_h = lax.broadcasted_iota(I32, (half, tq), 0)
    krow = lax.broadcasted_iota(I32, (SUB, tq), 0)
    gmax_ref[...] = jnp.full(gmax_ref.shape, -jnp.inf, F32)

    def to_key(score):
        bits = lax.bitcast_convert_type(score, I32)
        return bits ^ ((bits >> 31) & 0x7FFFFFFF)

    def score_body(c, carry):
        for r in range(ck // half):
            rows = pl.ds(pl.multiple_of(c * ck + r * half, half), half)
            ikc = ik_ref[rows, :][:, 0:IDX_DIM]
            acc = jnp.zeros((half, tq), F32)
            for h in range(IDX_HEADS):
                sc = _mm(ikc, iqt[h * IDX_DIM:(h + 1) * IDX_DIM, :])
                acc = acc + jnp.maximum(sc, 0.0) * iwt[h:h + 1, :]
            causal = krow_h + (c * ck + r * half) <= qpos_h
            key = jnp.where(causal, to_key(acc), INT_MIN)
            keys_ref[c, r * half:(r + 1) * half, :] = key
            hi_ref[c, r * half:(r + 1) * half, :] = (key >> 16).astype(jnp.int16)
            lo_ref[c, r * half:(r + 1) * half, :] = (((key ^ 0x8000) << 16) >> 16).astype(jnp.int16)
            gmax_ref[r * half:(r + 1) * half, :] = jnp.maximum(gmax_ref[r * half:(r + 1) * half, :],
                                                               jnp.where(causal, acc, -jnp.inf))
        return carry

    lax.fori_loop(0, nk, score_body, 0)

    def rep(row):
        return jnp.broadcast_to(row, (SUB, tq))

    def tile(v):
        return jnp.concatenate([v] * (ck // SUB), axis=0)

    def count(preds, *ops):
        nacc = 4 // len(preds)

        def body(c, accs):
            accs = [list(a) for a in accs]
            for g in range(ck // SUB):
                k = keys_ref[c, g * SUB:(g + 1) * SUB, :]
                for a, pred in zip(accs, preds):
                    a[g % nacc] = a[g % nacc] + jnp.where(pred(k, *ops), 1.0, 0.0)
            return tuple(tuple(a) for a in accs)
        accs = lax.fori_loop(0, nk, body, ((jnp.zeros((SUB, tq), F32),) * nacc,) * len(preds))
        return [rep(jnp.sum(sum(a[1:], a[0]), axis=0, keepdims=True)) for a in accs]

    gm = gmax_ref[...]
    smin = rep(jnp.min(gm, axis=0, keepdims=True))
    lo0 = jnp.where(smin == -jnp.inf, INT_MIN, to_key(smin))
    hi0 = to_key(rep(jnp.max(gm, axis=0, keepdims=True)))
    n_pos, n_nonneg = count([lambda k: k > 0, lambda k: k >= 0])
    at_zero = (n_pos < nsel) & (n_nonneg >= nsel)
    lo0 = jnp.where(at_zero, 0, jnp.where(n_pos >= nsel, jnp.maximum(lo0, 1), lo0))
    hi0 = jnp.where(at_zero, 0, jnp.where(n_nonneg < nsel, jnp.minimum(hi0, -1), hi0))
    unknown = float(2 * seq)
    low16 = -2 ** 15

    def pack16(v):
        return jnp.concatenate([v, v], axis=0).astype(jnp.int16)

    def count16(ref, t):
        t16 = pack16(t)
        nacc, rows = 4, 2 * SUB

        def body(c, accs):
            accs = list(accs)
            for g in range(ck // rows):
                hit = ref[c, g * rows:(g + 1) * rows, :] >= t16
                accs[g % nacc] = accs[g % nacc] + jnp.where(hit, jnp.int16(1), jnp.int16(0))
            return tuple(accs)
        accs = lax.fori_loop(0, nk, body, (jnp.zeros((rows, tq), jnp.int16),) * nacc)
        total = sum(a.astype(F32) for a in accs)
        return rep(jnp.sum(total, axis=0, keepdims=True))

    def bisect16(ref, lo, hi, above, want, inert):
        def cond(state):
            return (state[0] < 18) & (state[6] < 0.5)

        def body(state):
            it, lo, hi, n_lo, n_above, exact, _ = state
            all_done = jnp.min(jnp.where(lo == hi, 1.0, 0.0))
            mid = lo + ((hi - lo + 1) >> 1)
            cnt = count16(ref, mid)
            up = cnt >= want
            hit = (cnt == want) & (inert < 0.5)
            lo = jnp.where(up, mid, lo)
            n_lo = jnp.where(up, cnt, n_lo)
            n_above = jnp.where(up, n_above, cnt)
            hi = jnp.where(hit, mid, jnp.where(up, hi, mid - 1))
            exact = jnp.where(hit, 1.0, exact)
            return it + 1, lo, hi, n_lo, n_above, exact, all_done

        zero = jnp.zeros((SUB, tq), F32)
        out = lax.while_loop(cond, body, (jnp.int32(0), lo, hi, zero + unknown, above, zero, jnp.float32(0.0)))
        return out[1], out[3], out[4], out[5]

    zeros = jnp.zeros((SUB, tq), F32)
    top, _, n_above, exact1 = bisect16(hi_ref, lo0 >> 16, hi0 >> 16, jnp.where(n_nonneg < nsel, n_nonneg, 0.0),
                                       float(nsel), zeros)
    done1 = at_zero | (exact1 > 0.5)
    top16 = pack16(top)

    def mask_body(c, carry):
        for g in range(ck // (2 * SUB)):
            rows = slice(g * 2 * SUB, (g + 1) * 2 * SUB)
            lo_ref[c, rows, :] = jnp.where(hi_ref[c, rows, :] == top16, lo_ref[c, rows, :], jnp.int16(low16))
        return carry

    lax.fori_loop(0, nk, mask_body, 0)
    full = jnp.full((SUB, tq), low16, I32)
    bottom, n_bottom, _, _ = bisect16(lo_ref, full, jnp.where(done1, low16, -low16 - 1), zeros, nsel - n_above,
                                      jnp.where(done1, 1.0, 0.0))
    thr = jnp.where(at_zero, 0, jnp.where(done1, top << 16, (top << 16) + (bottom - low16)))
    n_thr = jnp.where(at_zero, n_nonneg, jnp.where(done1, float(nsel), n_above + n_bottom))

    tied = (n_thr != nsel) & (thr != INT_MIN)

    @pl.when(jnp.max(jnp.where(tied, 1.0, 0.0)) > 0.0)
    def _():
        n_above, = count([lambda k, t: k > t], thr)
        keep_t = tile(nsel - n_above)
        thr_t = tile(thr)
        tied_t = tile(tied)
        tri = jnp.where(lax.broadcasted_iota(I32, (ck, ck), 1) <= lax.broadcasted_iota(I32, (ck, ck), 0),
                        1.0, 0.0).astype(BF16)

        def strike(c, seen):
            key = keys_ref[c]
            tie = (key == thr_t) & tied_t
            rank = _mm(tri, jnp.where(tie, 1.0, 0.0).astype(BF16)) + tile(seen)
            keys_ref[c] = jnp.where(tie & (rank > keep_t), INT_MIN, key)
            return rep(rank[ck - 1:ck, :])
        lax.fori_loop(0, nk, strike, jnp.zeros((SUB, tq), F32))

    qat = qat_ref[0]
    sel_t = tile(jnp.where(thr == INT_MIN, INT_MIN + 1, thr))
    qts = [qat[h * HEAD_DIM:(h + 1) * HEAD_DIM, :] for h in range(A_HEADS)]

    def step(c, width, carry):
        k = kv_ref[pl.ds(pl.multiple_of(c * ck, ck), width * ck), :][:, 0:HEAD_DIM]
        vt = jnp.concatenate([vat_ref[c + j] for j in range(width)], axis=1)
        bias = jnp.concatenate([jnp.where(keys_ref[c + j] >= sel_t, 0.0, NEG_BIG) for j in range(width)], axis=0)
        ss = _flash_logits([k] * A_HEADS, qts)
        return _flash_update(ss, [vt] * A_HEADS, carry, bias)

    carry = (_flash_init(HEAD_DIM, tq),) * A_HEADS
    o_ref[...] = _flash_out(_walk_chunks(nk, step, carry))


def _dsa(iqt, iwt, qat, ik, kva, vat, b, s, tq=256):
    n = b * s
    nq = s // tq
    nsel = min(TOPK_MAX, s // 4)
    assert nsel <= tq and iqt.shape[2] == tq
    qtile = lambda w: pl.BlockSpec((1, w, tq), lambda bi, i: (bi * nq + i, 0, 0))
    seq_rows = pl.BlockSpec((s, LANES), lambda bi, i: (bi, 0))
    return pl.pallas_call(
        functools.partial(_dsa_kernel, tq=tq, nsel=nsel, seq=s),
        out_shape=jax.ShapeDtypeStruct((n, A_HEADS * HEAD_DIM), BF16), grid=(b, nq),
        in_specs=[qtile(IDX_HEADS * IDX_DIM), qtile(IDX_HEADS), qtile(A_HEADS * HEAD_DIM), seq_rows, seq_rows,
                  pl.BlockSpec((nq, HEAD_DIM, tq), lambda bi, i: (bi, 0, 0))],
        out_specs=pl.BlockSpec((tq, A_HEADS * HEAD_DIM), lambda bi, i: (bi * nq + i, 0)),
        scratch_shapes=[pltpu.VMEM((nq, tq, tq), I32), pltpu.VMEM((nq, tq, tq), jnp.int16),
                        pltpu.VMEM((nq, tq, tq), jnp.int16), pltpu.VMEM((tq, tq), F32)],
        compiler_params=_params(("arbitrary", "arbitrary")), name="dsa_attention",
    )(iqt, iwt, qat, ik, kva, vat)


def _mla_kernel(qt_ref, k_ref, vt_ref, o_ref, *, tq):
    i = pl.program_id(1)
    qt = qt_ref[0]
    kpos = lax.broadcasted_iota(I32, (tq, tq), 0)
    qpos = lax.broadcasted_iota(I32, (tq, tq), 1)
    diag_bias = jnp.where(kpos <= qpos, 0.0, NEG_BIG)

    heads = range(B_HEADS)
    qts = [qt[_slab(h), :] for h in heads]

    def step(c, width, carry, bias=None):
        rows = pl.ds(pl.multiple_of(c * tq, tq), width * tq)
        vt = jnp.concatenate([vt_ref[c + j] for j in range(width)], axis=1)
        ss = _flash_logits([k_ref[rows, _slab(h)] for h in heads], qts)
        return _flash_update(ss, [vt[h * B_V:(h + 1) * B_V, :] for h in heads], carry, bias)

    carry = (_flash_init(B_V, tq),) * B_HEADS
    o_ref[...] = _flash_out(step(i, 1, _walk_chunks(i, step, carry), diag_bias))


def _mla(qbt, kb, vbt, b, s, tq=256):
    n = b * s
    nq = s // tq
    assert qbt.shape[2] == tq
    return pl.pallas_call(
        functools.partial(_mla_kernel, tq=tq),
        out_shape=jax.ShapeDtypeStruct((n, B_HEADS * B_V), BF16), grid=(b, nq),
        in_specs=[pl.BlockSpec((1, B_HEADS * LANES, tq), lambda bi, i: (bi * nq + i, 0, 0)),
                  pl.BlockSpec((s, B_HEADS * LANES), lambda bi, i: (bi, 0)),
                  pl.BlockSpec((nq, B_HEADS * B_V, tq), lambda bi, i: (bi, 0, 0))],
        out_specs=pl.BlockSpec((tq, B_HEADS * B_V), lambda bi, i: (bi * nq + i, 0)),
        compiler_params=_params(("arbitrary", "arbitrary")), name="mla_attention",
    )(qbt, kb, vbt)


def _swa_kernel(sink_ref, q_ref, kp_ref, kc_ref, vp_ref, vc_ref, o_ref, *, tq):
    i = pl.program_id(1)
    q = q_ref[...]
    keys = jnp.concatenate([kp_ref[...], kc_ref[...]], axis=0)
    vals = jnp.concatenate([vp_ref[...], vc_ref[...]], axis=0)
    nkeys = WINDOW + tq
    qpos = lax.broadcasted_iota(I32, (tq, nkeys), 0) + i * tq
    kpos = lax.broadcasted_iota(I32, (tq, nkeys), 1) + i * tq - WINDOW
    visible = (kpos > qpos - WINDOW) & (kpos <= qpos) & (kpos >= 0)
    bias = jnp.where(visible, 0.0, NEG_BIG)
    group = C_HEADS // C_KV_HEADS
    heads = range(C_HEADS)
    ks = [keys[:, j * HEAD_DIM:(j + 1) * HEAD_DIM] for j in range(C_KV_HEADS)]
    vs = [vals[:, j * HEAD_DIM:(j + 1) * HEAD_DIM] for j in range(C_KV_HEADS)]
    ss = [_mm_nt(q[:, h * HEAD_DIM:(h + 1) * HEAD_DIM], ks[h // group]) + bias for h in heads]
    ms = [jnp.maximum(jnp.max(s, axis=1, keepdims=True), sink_ref[h]) for h, s in zip(heads, ss)]
    ps = [jnp.exp(s - m) for s, m in zip(ss, ms)]
    ws = [p / (jnp.sum(p, axis=1, keepdims=True) + jnp.exp(sink_ref[h] - m)) for h, p, m in zip(heads, ps, ms)]
    outs = [_mm(w.astype(BF16), vs[h // group]) for h, w in zip(heads, ws)]
    o_ref[...] = jnp.concatenate(outs, axis=1).astype(BF16)


def _swa(sinks, qc, kc, vc, b, s, tq=256):
    n = b * s
    nq = s // tq
    per = tq // WINDOW
    qrow = lambda w: pl.BlockSpec((tq, w), lambda bi, i: (bi * nq + i, 0))
    prev = pl.BlockSpec((WINDOW, LANES), lambda bi, i: (jnp.maximum((bi * nq + i) * per - 1, 0), 0))
    return pl.pallas_call(
        functools.partial(_swa_kernel, tq=tq),
        out_shape=jax.ShapeDtypeStruct((n, C_HEADS * HEAD_DIM), BF16), grid=(b, nq),
        in_specs=[pl.BlockSpec(memory_space=pltpu.SMEM), qrow(512), prev, qrow(LANES), prev, qrow(LANES)],
        out_specs=qrow(512),
        compiler_params=_params(("arbitrary", "arbitrary")), name="swa_attention",
    )(sinks, qc, kc, kc, vc, vc)


def _merge_kernel(ya_ref, yb_ref, yc_ref, gt_ref, x_ref, wa_ref, wb_ref, wc_ref, wo_ref, g_ref,
                  xo_ref, h_ref):
    d = x_ref.shape[1]
    merged = (gt_ref[:, 0:d] * _mm(ya_ref[...], wa_ref[...])
              + gt_ref[:, d:2 * d] * _mm(yb_ref[...], wb_ref[...])
              + gt_ref[:, 2 * d:3 * d] * _mm(yc_ref[...], wc_ref[...]))
    xn = x_ref[...] + _mm(merged.astype(BF16), wo_ref[...])
    xo_ref[...] = xn
    h = xn * lax.rsqrt(jnp.mean(xn * xn, axis=-1, keepdims=True) + EPS) * g_ref[...]
    h_ref[...] = h.astype(h_ref.dtype)


def _merge(ya, yb, yc, gt, xf, wa, wb, wc, wo, g, h_dtype, tm=256):
    n, d = xf.shape
    row = lambda w: pl.BlockSpec((tm, w), lambda i: (i, 0))
    full = lambda a: pl.BlockSpec(a.shape, lambda i: (0,) * a.ndim)
    return pl.pallas_call(
        _merge_kernel,
        out_shape=(jax.ShapeDtypeStruct((n, d), F32), jax.ShapeDtypeStruct((n, d), h_dtype)),
        grid=(n // tm,),
        in_specs=[row(256), row(256), row(512), row(3 * d), row(d), full(wa), full(wb), full(wc), full(wo),
                  full(g)],
        out_specs=(row(d), row(d)),
        compiler_params=_params(("arbitrary",)), name="merge_out_proj",
    )(ya, yb, yc, gt, xf, wa, wb, wc, wo, g)


def _ffn_kernel(h_ref, x_ref, w1_hbm, w3_hbm, w2_hbm, o_ref, w1_ref, w3_ref, w2_ref, sem, *, chunk):
    @pl.when(pl.program_id(0) == 0)
    def _():
        copies = [pltpu.make_async_copy(src, dst, sem.at[j]) for j, (src, dst) in
                  enumerate(((w1_hbm, w1_ref), (w3_hbm, w3_ref), (w2_hbm, w2_ref)))]
        for cp in copies:
            cp.start()
        for cp in copies:
            cp.wait()

    h = h_ref[...]
    acc = x_ref[...]
    for j in range(w1_ref.shape[1] // chunk):
        cols = slice(j * chunk, (j + 1) * chunk)
        a = _mm(h, w1_ref[:, cols])
        mid = a * _sigmoid(a) * _mm(h, w3_ref[:, cols])
        acc = acc + _mm(mid.astype(BF16), w2_ref[cols, :])
    o_ref[...] = acc


def _ffn(h2, xf, w1, w3, w2, tm=256, chunk=1408):
    n, d = xf.shape
    row = lambda w: pl.BlockSpec((tm, w), lambda i: (i, 0))
    hbm = pl.BlockSpec(memory_space=pl.ANY)
    return pl.pallas_call(
        functools.partial(_ffn_kernel, chunk=chunk),
        out_shape=jax.ShapeDtypeStruct((n, d), F32), grid=(n // tm,),
        in_specs=[row(d), row(d), hbm, hbm, hbm], out_specs=row(d),
        scratch_shapes=[pltpu.VMEM(w1.shape, BF16), pltpu.VMEM(w3.shape, BF16), pltpu.VMEM(w2.shape, BF16),
                        pltpu.SemaphoreType.DMA((3,))],
        compiler_params=_params(("arbitrary",)), name="dense_swiglu",
    )(h2, xf, w1, w3, w2)


def _route_kernel(h_ref, wr_ref, tri_ref, route_ref, cnt_ref, carry_ref):
    @pl.when(pl.program_id(0) == 0)
    def _():
        carry_ref[...] = jnp.zeros_like(carry_ref)

    logits = jnp.dot(h_ref[...], wr_ref[...], precision=lax.Precision.HIGHEST, preferred_element_type=F32)
    lane = lax.broadcasted_iota(I32, logits.shape, 1).astype(F32)
    lg = jnp.where(lane < N_EXPERTS, logits, -jnp.inf)
    m1 = jnp.max(lg, axis=1, keepdims=True)
    e1 = jnp.min(jnp.where(lg == m1, lane, float(LANES)), axis=1, keepdims=True)
    lg2 = jnp.where(lane == e1, -jnp.inf, lg)
    m2 = jnp.max(lg2, axis=1, keepdims=True)
    e2 = jnp.min(jnp.where(lg2 == m2, lane, float(LANES)), axis=1, keepdims=True)
    ex = jnp.exp(m2 - m1)
    g1 = 1.0 / (1.0 + ex)
    g2 = ex / (1.0 + ex)
    onehot = jnp.where((lane == e1) | (lane == e2), 1.0, 0.0)
    before = _mm(tri_ref[...], onehot.astype(BF16)) + carry_ref[0:1, :]
    r1 = jnp.sum(jnp.where(lane == e1, before, 0.0), axis=1, keepdims=True)
    r2 = jnp.sum(jnp.where(lane == e2, before, 0.0), axis=1, keepdims=True)
    out = jnp.zeros_like(logits)
    for idx, val in enumerate((e1, e2, g1, g2, r1, r2)):
        out = jnp.where(lane == idx, val, out)
    route_ref[...] = out
    total = carry_ref[0:1, :] + jnp.sum(onehot, axis=0, keepdims=True)
    carry_ref[...] = jnp.broadcast_to(total, carry_ref.shape)
    cnt_ref[...] = jnp.broadcast_to(total, cnt_ref.shape)


def _route(h2, wr, tm=256):
    n, d = h2.shape
    tri = jnp.asarray(np.tril(np.ones((tm, tm), np.float32), -1), BF16)
    full = lambda a: pl.BlockSpec(a.shape, lambda i: (0,) * a.ndim)
    return pl.pallas_call(
        _route_kernel,
        out_shape=(jax.ShapeDtypeStruct((n, LANES), F32), jax.ShapeDtypeStruct((8, LANES), F32)),
        grid=(n // tm,),
        in_specs=[pl.BlockSpec((tm, d), lambda i: (i, 0)), full(wr), full(tri)],
        out_specs=(pl.BlockSpec((tm, LANES), lambda i: (i, 0)), pl.BlockSpec((8, LANES), lambda i: (0, 0))),
        scratch_shapes=[pltpu.VMEM((8, LANES), F32)],
        compiler_params=_params(("arbitrary",)), name="moe_route",
    )(h2, wr, tri)


def _scatter_kernel(dest_ref, h_ref, xin_hbm, xout_hbm, sem, *, tt):
    del xin_hbm

    def copy(r, k):
        d = dest_ref[0, 0, 2 * r + k]
        return pltpu.make_async_copy(h_ref.at[pl.ds(r, 1), :], xout_hbm.at[pl.ds(d, 1), :], sem)

    def start(r, carry):
        copy(r, 0).start()
        copy(r, 1).start()
        return carry

    def wait(r, carry):
        copy(r, 0).wait()
        copy(r, 1).wait()
        return carry

    lax.fori_loop(0, tt, start, 0)
    lax.fori_loop(0, tt, wait, 0)


def _scatter(dest3, h2, rows, tt):
    n, w = h2.shape
    hbm = pl.BlockSpec(memory_space=pl.ANY)
    return pl.pallas_call(
        functools.partial(_scatter_kernel, tt=tt),
        out_shape=jax.ShapeDtypeStruct((rows, w), h2.dtype), grid=(n // tt,),
        in_specs=[pl.BlockSpec((1, 1, 2 * tt), lambda i: (i, 0, 0), memory_space=pltpu.SMEM),
                  pl.BlockSpec((tt, w), lambda i: (i, 0)), hbm],
        out_specs=hbm, scratch_shapes=[pltpu.SemaphoreType.DMA(())],
        input_output_aliases={2: 0},
        compiler_params=_params(("arbitrary",)), name="moe_scatter",
    )(dest3, h2, jnp.zeros((rows, w), h2.dtype))


def _expert_kernel(be_ref, nv_ref, x_ref, w1_hbm, w3_hbm, w2_hbm, y_ref, w1_ref, w3_ref, w2_ref, sem, *, chunk):
    blk = pl.program_id(0)
    e = be_ref[blk]

    @pl.when((blk == 0) | (e != be_ref[jnp.maximum(blk - 1, 0)]))
    def _():
        copies = [pltpu.make_async_copy(src.at[e], dst, sem.at[j]) for j, (src, dst) in
                  enumerate(((w1_hbm, w1_ref), (w3_hbm, w3_ref), (w2_hbm, w2_ref)))]
        for cp in copies:
            cp.start()
        for cp in copies:
            cp.wait()

    @pl.when(blk < nv_ref[0])
    def _():
        x = x_ref[...].astype(BF16)
        acc = jnp.zeros(y_ref.shape, F32)
        for j in range(w1_ref.shape[1] // chunk):
            cols = slice(j * chunk, (j + 1) * chunk)
            a = _mm(x, w1_ref[:, cols])
            mid = a * _sigmoid(a) * _mm(x, w3_ref[:, cols])
            acc = acc + _mm(mid.astype(BF16), w2_ref[cols, :])
        y_ref[...] = acc

    @pl.when(blk >= nv_ref[0])
    def _():
        y_ref[...] = jnp.zeros(y_ref.shape, F32)


def _experts(block_e, nvalid, xbuf, w1, w3, w2, chunk=512):
    rows, d = xbuf.shape
    hbm = pl.BlockSpec(memory_space=pl.ANY)
    blk = pl.BlockSpec((MOE_BLOCK, d), lambda i, be, nv: (i, 0))
    return pl.pallas_call(
        functools.partial(_expert_kernel, chunk=chunk),
        out_shape=jax.ShapeDtypeStruct((rows, d), F32),
        grid_spec=pltpu.PrefetchScalarGridSpec(
            num_scalar_prefetch=2, grid=(rows // MOE_BLOCK,),
            in_specs=[blk, hbm, hbm, hbm], out_specs=blk,
            scratch_shapes=[pltpu.VMEM(w1.shape[1:], BF16), pltpu.VMEM(w3.shape[1:], BF16),
                            pltpu.VMEM(w2.shape[1:], BF16), pltpu.SemaphoreType.DMA((3,))]),
        compiler_params=_params(("arbitrary",)), name="moe_experts",
    )(block_e, nvalid, xbuf, w1, w3, w2)


def _combine_kernel(dest_ref, dest_next_ref, y_hbm, x_ref, route_ref, o_ref, buf_ref, sem, *, tt):
    i = pl.program_id(0)
    slot = i % 2

    def copies(dref, s, fn):
        def body(r, carry):
            for k in range(2):
                d = dref[0, 0, 2 * r + k]
                fn(pltpu.make_async_copy(y_hbm.at[pl.ds(d, 1), :], buf_ref.at[s, k, pl.ds(r, 1), :], sem.at[s]))
            return carry
        lax.fori_loop(0, tt, body, 0)

    @pl.when(i == 0)
    def _():
        copies(dest_ref, slot, lambda cp: cp.start())

    @pl.when(i + 1 < pl.num_programs(0))
    def _():
        copies(dest_next_ref, 1 - slot, lambda cp: cp.start())

    copies(dest_ref, slot, lambda cp: cp.wait())
    rt = route_ref[...]
    o_ref[...] = x_ref[...] + (buf_ref[slot, 0] * rt[:, 2:3] + buf_ref[slot, 1] * rt[:, 3:4])


def _combine(dest3, ybuf, xf, route, tt):
    n, d = xf.shape
    nt = n // tt
    row = lambda w: pl.BlockSpec((tt, w), lambda i: (i, 0))
    idx = lambda f: pl.BlockSpec((1, 1, 2 * tt), lambda i: (f(i), 0, 0), memory_space=pltpu.SMEM)
    return pl.pallas_call(
        functools.partial(_combine_kernel, tt=tt),
        out_shape=jax.ShapeDtypeStruct((n, d), F32), grid=(nt,),
        in_specs=[idx(lambda i: i), idx(lambda i: jnp.minimum(i + 1, nt - 1)),
                  pl.BlockSpec(memory_space=pl.ANY), row(d), row(LANES)],
        out_specs=row(d),
        scratch_shapes=[pltpu.VMEM((2, 2, tt, d), F32), pltpu.SemaphoreType.DMA((2,))],
        compiler_params=_params(("arbitrary",)), name="moe_combine",
    )(dest3, dest3, ybuf, xf, route)


def _moe(xf, h2, w_router, w1, w3, w2, tt=256):
    n, d = xf.shape
    wr = jnp.pad(w_router, ((0, 0), (0, LANES - N_EXPERTS)))
    route, cnt = _route(h2, wr)
    counts = cnt[0, :N_EXPERTS].astype(I32)
    padded = (counts + MOE_BLOCK - 1) // MOE_BLOCK * MOE_BLOCK
    pend = jnp.cumsum(padded)
    pstart = pend - padded
    dest = pstart[route[:, 0:2].astype(I32)] + route[:, 4:6].astype(I32)
    dest3 = dest.reshape(n // tt, 1, 2 * tt)
    n_blocks = n * 2 // MOE_BLOCK + N_EXPERTS
    rows = n_blocks * MOE_BLOCK
    block_e = jnp.minimum(jnp.searchsorted(pend, jnp.arange(n_blocks) * MOE_BLOCK, side='right'),
                          N_EXPERTS - 1).astype(I32)
    nvalid = (pend[-1:] // MOE_BLOCK).astype(I32)
    xbuf = _scatter(dest3, h2, rows, tt)
    ybuf = _experts(block_e, nvalid, xbuf, w1, w3, w2)
    return _combine(dest3, ybuf, xf, route, tt)


def _tile_row(v, width=LANES):
    v = v.astype(F32)
    return jnp.tile(v, width // v.shape[0])


def _layer_params(l, attn_norm_g, w_in, a_q_norm_g, a_k_norm_g, b_cq_norm_g, b_ckv_norm_g, b_w_uq, b_w_ukv,
                  b_qn_g, b_qr_g, b_kn_g, b_kr_g, c_q_norm_g, c_k_norm_g):
    d = w_in.shape[1]
    sizes = (A_HEADS * HEAD_DIM, HEAD_DIM, HEAD_DIM, IDX_HEADS * IDX_DIM, IDX_DIM, IDX_HEADS,
             b_w_uq.shape[1], b_w_ukv.shape[1], B_ROPE, C_HEADS * HEAD_DIM, C_KV_HEADS * HEAD_DIM,
             C_KV_HEADS * HEAD_DIM, 3 * d)
    qa, ka, va, iq, ik, iw, cq, ckv, kr, qc, kc, vc, gates = jnp.split(w_in[l], np.cumsum(sizes)[:-1].tolist(), axis=1)
    z = lambda k: jnp.zeros((d, k), F32)
    w_r = jnp.concatenate([qa, ka, va, iq, ik, iw, z(LANES - IDX_DIM - IDX_HEADS), cq, ckv,
                           z(B_NOPE), kr, z(LANES - B_NOPE - B_ROPE), qc, kc, vc, gates], axis=1).astype(BF16)
    uq = b_w_uq[l].reshape(-1, B_HEADS, B_NOPE + B_ROPE)
    uq = jnp.pad(uq, ((0, 0), (0, 0), (0, LANES - B_NOPE - B_ROPE))).reshape(-1, B_HEADS * LANES).astype(BF16)
    ukv = b_w_ukv[l].reshape(-1, B_HEADS, B_NOPE + B_V)
    uk = jnp.pad(ukv[:, :, :B_NOPE], ((0, 0), (0, 0), (0, LANES - B_NOPE))).reshape(-1, B_HEADS * LANES)
    uv = ukv[:, :, B_NOPE:].reshape(-1, B_HEADS * B_V)
    wukv = jnp.concatenate([uk, uv], axis=1).astype(BF16)
    zeros = lambda k: jnp.zeros((k,), F32)
    grows = jnp.stack([
        _tile_row(a_q_norm_g[l]),
        jnp.concatenate([a_k_norm_g[l], jnp.ones((HEAD_DIM,), F32)]),
        jnp.concatenate([b_qn_g[l], b_qr_g[l], zeros(LANES - B_NOPE - B_ROPE)]),
        jnp.concatenate([b_kn_g[l], zeros(LANES - B_NOPE)]),
        jnp.concatenate([zeros(B_NOPE), b_kr_g[l], zeros(LANES - B_NOPE - B_ROPE)]),
        _tile_row(c_q_norm_g[l]),
        _tile_row(c_k_norm_g[l]),
        b_ckv_norm_g[l],
    ]).astype(F32)
    return (attn_norm_g[l][None, :], w_r, uq, wukv, b_cq_norm_g[l][None, :], grows)


def _seg_matrices():
    lane = np.arange(LANES)
    g64 = lane // 64
    m64 = (g64[:, None] == g64[None, :]) / 64.0
    gqb = np.where(lane < B_NOPE, 0, np.where(lane < B_NOPE + B_ROPE, 1, 2))
    size = np.where(lane < B_NOPE, B_NOPE, B_ROPE)
    mqb = (gqb[:, None] == gqb[None, :]) / size[None, :]
    return jnp.asarray(np.stack([m64, mqb]), BF16)


def kernel(x, positions, attn_norm_g, w_in, a_q_norm_g, a_k_norm_g, b_cq_norm_g, b_ckv_norm_g, b_w_uq, b_w_ukv,
           b_qn_g, b_qr_g, b_kn_g, b_kr_g, c_q_norm_g, c_k_norm_g, c_sinks, w_a_out, w_b_out, w_c_out, w_o,
           ffn_norm_g, ffn_w1, ffn_w3, ffn_w2, router_w, moe_w1, moe_w3, moe_w2):
    b, s, d = x.shape
    n = b * s
    depth = w_in.shape[0]
    xf = x.reshape(n, d)
    tabs = _rope_tables(positions.reshape(n, 1).astype(F32))
    mseg = _seg_matrices()
    for l in range(depth):
        g, w_r, uq, wukv, gcq, grows = _layer_params(
            l, attn_norm_g, w_in, a_q_norm_g, a_k_norm_g, b_cq_norm_g, b_ckv_norm_g, b_w_uq, b_w_ukv,
            b_qn_g, b_qr_g, b_kn_g, b_kr_g, c_q_norm_g, c_k_norm_g)
        qat, kva, vat, iqt, ik, iwt, qbt, kb, vbt, qc, kc, vc, gt = _proj(xf, g, w_r, uq, wukv, gcq, grows, mseg,
                                                                         tabs)
        ya = _dsa(iqt, iwt, qat, ik, kva, vat, b, s)
        yb = _mla(qbt, kb, vbt, b, s)
        yc = _swa(c_sinks[l], qc, kc, vc, b, s)
        dense = l % 2 == 0
        xf, h2 = _merge(ya, yb, yc, gt, xf, w_a_out[l].astype(BF16), w_b_out[l].astype(BF16),
                        w_c_out[l].astype(BF16), w_o[l].astype(BF16), ffn_norm_g[l][None, :],
                        BF16 if dense else F32)
        if dense:
            xf = _ffn(h2, xf, ffn_w1[l // 2].astype(BF16), ffn_w3[l // 2].astype(BF16), ffn_w2[l // 2].astype(BF16))
        else:
            xf = _moe(xf, h2, router_w[l // 2], moe_w1[l // 2].astype(BF16),
                      moe_w3[l // 2].astype(BF16), moe_w2[l // 2].astype(BF16))
    return xf.reshape(b, s, d)
```

```python
import functools

import numpy as np
import jax
import jax.numpy as jnp
from jax import lax
from jax.experimental import pallas as pl
from jax.experimental.pallas import tpu as pltpu

F32, BF16, I32 = jnp.float32, jnp.bfloat16, jnp.int32

EPS = 1e-6
ROPE_THETA = 10000.0
HEAD_DIM = 64
A_HEADS = 4
IDX_HEADS = 8
IDX_DIM = 32
TOPK_MAX = 256
B_HEADS = 4
B_NOPE = 64
B_ROPE = 32
B_V = 64
C_HEADS = 8
C_KV_HEADS = 2
WINDOW = 128
N_EXPERTS = 8
MOE_BLOCK = 256

LANES = 128
SUB = 8
VMEM_LIMIT = 56 * 1024 * 1024
INT_MIN = -2 ** 31
NEG_BIG = -1e30
LOG2E = 1.4426950408889634

C_QA, C_KVA, C_IQ, C_IKW, C_CQ, C_CKV, C_KR, C_QC, C_KC, C_VC, C_G = (
    0, 256, 384, 640, 768, 1024, 1152, 1280, 1792, 1920, 2048)


def _params(sem):
    return pltpu.CompilerParams(dimension_semantics=sem, vmem_limit_bytes=VMEM_LIMIT)


def _mm(a, b):
    return jnp.dot(a, b, preferred_element_type=F32)


def _mm_nt(a, b):
    return lax.dot_general(a, b, (((1,), (1,)), ((), ())), preferred_element_type=F32)


def _sigmoid(v):
    return 1.0 / (1.0 + jnp.exp(-v))


def _rope_table_kernel(pos_ref, f_ref, sg_ref, c64_ref, s64_ref, c32_ref, s32_ref):
    pos = pos_ref[...]
    a = pos * f_ref[0:1, :]
    c64_ref[...] = jnp.cos(a)
    s64_ref[...] = jnp.sin(a) * sg_ref[0:1, :]
    a = pos * f_ref[1:2, :]
    c32_ref[...] = jnp.cos(a)
    s32_ref[...] = jnp.sin(a) * sg_ref[1:2, :]


def _rope_tables(pos_f, tm=1024):
    n = pos_f.shape[0]
    lane = np.arange(LANES)
    f32 = ROPE_THETA ** (-jnp.arange(32, dtype=F32) / 32)
    f16 = ROPE_THETA ** (-jnp.arange(16, dtype=F32) / 16)
    freqs = jnp.stack([f32[lane % 32], f16[lane % 16]])
    signs = jnp.asarray(np.stack([np.where(lane % 64 < 32, -1.0, 1.0),
                                  np.where(lane % 32 < 16, -1.0, 1.0)]), F32)
    tab = jax.ShapeDtypeStruct((n, LANES), F32)
    row = pl.BlockSpec((tm, LANES), lambda i: (i, 0))
    par = pl.BlockSpec((2, LANES), lambda i: (0, 0))
    return pl.pallas_call(
        _rope_table_kernel, out_shape=(tab,) * 4, grid=(n // tm,),
        in_specs=[pl.BlockSpec((tm, 1), lambda i: (i, 0)), par, par],
        out_specs=(row,) * 4, compiler_params=_params(("arbitrary",)), name="rope_tables",
    )(pos_f, freqs, signs)


def _swap_half(y, half):
    lane = lax.broadcasted_iota(I32, y.shape, 1)
    return jnp.where((lane & half) == 0, pltpu.roll(y, LANES - half, 1), pltpu.roll(y, half, 1))


def _rope(y, cos, sin_signed, half):
    return y * cos + _swap_half(y, half) * sin_signed


def _seg_mean_sq(y, mseg):
    sq = y * y
    hi = sq.astype(BF16)
    lo = (sq - hi.astype(F32)).astype(BF16)
    return _mm(hi, mseg) + _mm(lo, mseg)


def _seg_norm(y, mseg, gain):
    return y * lax.rsqrt(_seg_mean_sq(y, mseg) + EPS) * gain


def _slab(s):
    return slice(s * LANES, (s + 1) * LANES)


def _proj_kernel(x_ref, g_ref, w_ref, wuq_ref, wukv_ref, gcq_ref, grows_ref, mseg_ref,
                 c64_ref, s64_ref, c32_ref, s32_ref,
                 qat_ref, kva_ref, vat_ref, iqt_ref, ik_ref, iwt_ref, qbt_ref, kb_ref, vbt_ref,
                 qc_ref, kc_ref, vc_ref, gt_ref):
    x = x_ref[...]
    h = (x * lax.rsqrt(jnp.mean(x * x, axis=-1, keepdims=True) + EPS) * g_ref[...]).astype(BF16)
    c64, s64, c32, s32 = c64_ref[...], s64_ref[...], c32_ref[...], s32_ref[...]
    m64, mqb = mseg_ref[0], mseg_ref[1]
    lane = lax.broadcasted_iota(I32, c64.shape, 1)
    in_rope = (lane >= B_NOPE) & (lane < B_NOPE + B_ROPE)
    cb = jnp.where(in_rope, c32, 1.0)
    sb = jnp.where(in_rope, s32, 0.0)

    p = _mm(h, w_ref[:, C_QA:C_CQ])
    for s in range(2):
        y = _rope(_seg_norm(p[:, _slab(s)], m64, grows_ref[0:1, :]), c64, s64, 32)
        qat_ref[0, _slab(s), :] = (y * (LOG2E * HEAD_DIM ** -0.5)).T.astype(BF16)
    y = p[:, _slab(2)]
    yr = _rope(_seg_norm(y, m64, grows_ref[1:2, :]), c64, s64, 32)
    kva_ref[...] = jnp.where(lane < HEAD_DIM, yr, y).astype(BF16)
    vat_ref[0] = y.T[HEAD_DIM:2 * HEAD_DIM, :].astype(BF16)
    for s in range(2):
        iqt_ref[0, _slab(s), :] = _rope(p[:, _slab(3 + s)], c32, s32, 16).T.astype(BF16)
    y = p[:, _slab(5)]
    ik_ref[...] = jnp.where(lane < IDX_DIM, _rope(y, c32, s32, 16), 0.0).astype(BF16)
    iwt_ref[0] = (y * (IDX_HEADS * IDX_DIM) ** -0.5).T[IDX_DIM:IDX_DIM + IDX_HEADS, :]

    p = _mm(h, w_ref[:, C_CQ:C_QC])
    cq = p[:, 0:256]
    cqn = cq * lax.rsqrt(jnp.mean(cq * cq, axis=-1, keepdims=True) + EPS) * gcq_ref[...]
    qb = _mm(cqn.astype(BF16), wuq_ref[...])
    for s in range(B_HEADS):
        y = _rope(_seg_norm(qb[:, _slab(s)], mqb, grows_ref[2:3, :]), cb, sb, 16)
        qbt_ref[0, _slab(s), :] = (y * (LOG2E * (B_NOPE + B_ROPE) ** -0.5)).T.astype(BF16)
    ckv = p[:, 256:384]
    ckvn = ckv * lax.rsqrt(jnp.mean(ckv * ckv, axis=-1, keepdims=True) + EPS) * grows_ref[7:8, :]
    kvb = _mm(ckvn.astype(BF16), wukv_ref[...])
    krs = p[:, 384:512]
    kr = krs * lax.rsqrt(jnp.sum(krs * krs, axis=-1, keepdims=True) * (1.0 / B_ROPE) + EPS) * grows_ref[4:5, :]
    kr = _rope(kr, cb, sb, 16)
    for s in range(B_HEADS):
        kb_ref[:, _slab(s)] = (_seg_norm(kvb[:, _slab(s)], m64, grows_ref[3:4, :]) + kr).astype(BF16)
    for s in range(2):
        vbt_ref[0, _slab(s), :] = kvb[:, 512 + s * LANES:512 + (s + 1) * LANES].T.astype(BF16)

    p = _mm(h, w_ref[:, C_QC:C_G])
    for s in range(4):
        y = _rope(_seg_norm(p[:, _slab(s)], m64, grows_ref[5:6, :]), c64, s64, 32)
        qc_ref[:, _slab(s)] = (y * HEAD_DIM ** -0.5).astype(BF16)
    y = _rope(_seg_norm(p[:, _slab(4)], m64, grows_ref[6:7, :]), c64, s64, 32)
    kc_ref[...] = y.astype(BF16)
    vc_ref[...] = p[:, _slab(5)].astype(BF16)

    for c in range(3):
        lo = C_G + c * 1024
        gt_ref[:, c * 1024:(c + 1) * 1024] = _sigmoid(_mm(h, w_ref[:, lo:lo + 1024])).astype(gt_ref.dtype)


def _proj(xf, g, w_r, wuq, wukv, gcq, grows, mseg, tabs, tm=256):
    n, d = xf.shape
    row = lambda w: pl.BlockSpec((tm, w), lambda i: (i, 0))
    full = lambda a: pl.BlockSpec(a.shape, lambda i: (0,) * a.ndim)
    outs = [("t", 256, BF16), ("r", 128, BF16), ("t", HEAD_DIM, BF16), ("t", 256, BF16), ("r", 128, BF16),
            ("t", IDX_HEADS, F32), ("t", 512, BF16), ("r", 512, BF16), ("t", 256, BF16),
            ("r", 512, BF16), ("r", 128, BF16), ("r", 128, BF16), ("r", 3072, BF16)]
    shape = lambda k, w, dt: jax.ShapeDtypeStruct((n // tm, w, tm) if k == "t" else (n, w), dt)
    spec = lambda k, w: pl.BlockSpec((1, w, tm), lambda i: (i, 0, 0)) if k == "t" else row(w)
    return pl.pallas_call(
        _proj_kernel,
        out_shape=tuple(shape(*o) for o in outs),
        grid=(n // tm,),
        in_specs=[row(d), full(g), full(w_r), full(wuq), full(wukv), full(gcq), full(grows), full(mseg)]
        + [row(LANES)] * 4,
        out_specs=tuple(spec(k, w) for k, w, _ in outs),
        compiler_params=_params(("arbitrary",)), name="in_proj",
    )(xf, g, w_r, wuq, wukv, gcq, grows, mseg, *tabs)


def _flash_logits(ks, qts):
    return tuple(_mm(k, qt) for k, qt in zip(ks, qts))


def _flash_update(ss, vts, carries, bias=None):
    if bias is not None:
        ss = [s + bias for s in ss]
    m_new = [jnp.maximum(c[0], jnp.max(s, axis=0, keepdims=True)) for c, s in zip(carries, ss)]
    ps = [jnp.exp2(s - m) for s, m in zip(ss, m_new)]
    pvs = [_mm(vt, p.astype(BF16)) for vt, p in zip(vts, ps)]
    out = []
    for (m, l, acc), mn, p, pv in zip(carries, m_new, ps, pvs):
        alpha = jnp.exp2(m - mn)
        out.append((mn, alpha * l + jnp.sum(p, axis=0, keepdims=True), alpha * acc + pv))
    return tuple(out)


def _walk_chunks(n, step, carry, widths=(4, 2, 1)):
    start = 0
    for w in widths:
        trips = (n - start) // w
        carry = lax.fori_loop(0, trips, lambda j, cr, start=start, w=w: step(start + j * w, w, cr), carry)
        start = start + trips * w
    return carry


def _flash_init(dv, tq):
    return jnp.full((1, tq), NEG_BIG, F32), jnp.zeros((1, tq), F32), jnp.zeros((dv, tq), F32)


def _flash_out(results):
    return jnp.concatenate([acc / l for _, l, acc in results], axis=0).T.astype(BF16)


def _dsa_kernel(iqt_ref, iwt_ref, qat_ref, ik_ref, kv_ref, vat_ref, o_ref, keys_ref, hi_ref, lo_ref, gmax_ref, *,
                tq, nsel, seq):
    per, _, ck = iqt_ref.shape
    i = pl.program_id(1)
    nk = (i + 1) * per

    def lanes(ref):
        return jnp.concatenate([ref[j] for j in range(per)], axis=1)

    iqt = lanes(iqt_ref)
    iwt = lanes(iwt_ref)
    half = ck // (2 * per)
    qpos_h = lax.broadcasted_iota(I32, (half, tq), 1) + i * tq
    krow_h = lax.broadcasted_iota(I32, (half, tq), 0)
    gmax_ref[...] = jnp.full(gmax_ref.shape, -jnp.inf, F32)

    def to_key(score):
        bits = lax.bitcast_convert_type(score, I32)
        return bits ^ ((bits >> 31) & 0x7FFFFFFF)

    def score_body(c, carry):
        for r in range(ck // half):
            rows = pl.ds(pl.multiple_of(c * ck + r * half, half), half)
            ikc = ik_ref[rows, :][:, 0:IDX_DIM]
            acc = jnp.zeros((half, tq), F32)
            for h in range(IDX_HEADS):
                sc = _mm(ikc, iqt[h * IDX_DIM:(h + 1) * IDX_DIM, :])
                acc = acc + jnp.maximum(sc, 0.0) * iwt[h:h + 1, :]
            causal = krow_h + (c * ck + r * half) <= qpos_h
            key = jnp.where(causal, to_key(acc), INT_MIN)
            keys_ref[c, r * half:(r + 1) * half, :] = key
            hi_ref[c, r * half:(r + 1) * half, :] = (key >> 16).astype(jnp.int16)
            lo_ref[c, r * half:(r + 1) * half, :] = (((key ^ 0x8000) << 16) >> 16).astype(jnp.int16)
            gmax_ref[r * half:(r + 1) * half, :] = jnp.maximum(gmax_ref[r * half:(r + 1) * half, :],
                                                               jnp.where(causal, acc, -jnp.inf))
        return carry

    lax.fori_loop(0, nk, score_body, 0)

    def rep(row):
        return jnp.broadcast_to(row, (SUB, tq))

    def tile(v):
        return jnp.concatenate([v] * (ck // SUB), axis=0)

    def count(preds, *ops):
        nacc = 4 // len(preds)

        def body(c, accs):
            accs = [list(a) for a in accs]
            for g in range(ck // SUB):
                k = keys_ref[c, g * SUB:(g + 1) * SUB, :]
                for a, pred in zip(accs, preds):
                    a[g % nacc] = a[g % nacc] + jnp.where(pred(k, *ops), 1.0, 0.0)
            return tuple(tuple(a) for a in accs)
        accs = lax.fori_loop(0, nk, body, ((jnp.zeros((SUB, tq), F32),) * nacc,) * len(preds))
        return [rep(jnp.sum(sum(a[1:], a[0]), axis=0, keepdims=True)) for a in accs]

    gm = gmax_ref[...]
    smin = rep(jnp.min(gm, axis=0, keepdims=True))
    lo0 = jnp.where(smin == -jnp.inf, INT_MIN, to_key(smin))
    hi0 = to_key(rep(jnp.max(gm, axis=0, keepdims=True)))
    n_pos, n_nonneg = count([lambda k: k > 0, lambda k: k >= 0])
    at_zero = (n_pos < nsel) & (n_nonneg >= nsel)
    lo0 = jnp.where(at_zero, 0, jnp.where(n_pos >= nsel, jnp.maximum(lo0, 1), lo0))
    hi0 = jnp.where(at_zero, 0, jnp.where(n_nonneg < nsel, jnp.minimum(hi0, -1), hi0))
    unknown = float(2 * seq)
    low16 = -2 ** 15

    def pack16(v):
        return jnp.concatenate([v, v], axis=0).astype(jnp.int16)

    def count16(ref, t):
        t16 = pack16(t)
        nacc, rows = 4, 2 * SUB

        def body(c, accs):
            accs = list(accs)
            for g in range(ck // rows):
                hit = ref[c, g * rows:(g + 1) * rows, :] >= t16
                accs[g % nacc] = accs[g % nacc] + jnp.where(hit, jnp.int16(1), jnp.int16(0))
            return tuple(accs)
        accs = lax.fori_loop(0, nk, body, (jnp.zeros((rows, tq), jnp.int16),) * nacc)
        total = sum(a.astype(F32) for a in accs)
        return rep(jnp.sum(total, axis=0, keepdims=True))

    def bisect16(ref, lo, hi, above, want, inert):
        def cond(state):
            return (state[0] < 18) & (state[6] < 0.5)

        def body(state):
            it, lo, hi, n_lo, n_above, exact, _ = state
            all_done = jnp.min(jnp.where(lo == hi, 1.0, 0.0))
            mid = lo + ((hi - lo + 1) >> 1)
            cnt = count16(ref, mid)
            up = cnt >= want
            hit = (cnt == want) & (inert < 0.5)
            lo = jnp.where(up, mid, lo)
            n_lo = jnp.where(up, cnt, n_lo)
            n_above = jnp.where(up, n_above, cnt)
            hi = jnp.where(hit, mid, jnp.where(up, hi, mid - 1))
            exact = jnp.where(hit, 1.0, exact)
            return it + 1, lo, hi, n_lo, n_above, exact, all_done

        zero = jnp.zeros((SUB, tq), F32)
        out = lax.while_loop(cond, body, (jnp.int32(0), lo, hi, zero + unknown, above, zero, jnp.float32(0.0)))
        return out[1], out[3], out[4], out[5]

    zeros = jnp.zeros((SUB, tq), F32)
    top, _, n_above, exact1 = bisect16(hi_ref, lo0 >> 16, hi0 >> 16, jnp.where(n_nonneg < nsel, n_nonneg, 0.0),
                                       float(nsel), zeros)
    done1 = at_zero | (exact1 > 0.5)
    top16 = pack16(top)

    def mask_body(c, carry):
        for g in range(ck // (2 * SUB)):
            rows = slice(g * 2 * SUB, (g + 1) * 2 * SUB)
            lo_ref[c, rows, :] = jnp.where(hi_ref[c, rows, :] == top16, lo_ref[c, rows, :], jnp.int16(low16))
        return carry

    lax.fori_loop(0, nk, mask_body, 0)
    full = jnp.full((SUB, tq), low16, I32)
    bottom, n_bottom, _, _ = bisect16(lo_ref, full, jnp.where(done1, low16, -low16 - 1), zeros, nsel - n_above,
                                      jnp.where(done1, 1.0, 0.0))
    thr = jnp.where(at_zero, 0, jnp.where(done1, top << 16, (top << 16) + (bottom - low16)))
    n_thr = jnp.where(at_zero, n_nonneg, jnp.where(done1, float(nsel), n_above + n_bottom))

    tied = (n_thr != nsel) & (thr != INT_MIN)

    @pl.when(jnp.max(jnp.where(tied, 1.0, 0.0)) > 0.0)
    def _():
        n_above, = count([lambda k, t: k > t], thr)
        keep_t = tile(nsel - n_above)
        thr_t = tile(thr)
        tied_t = tile(tied)
        tri = jnp.where(lax.broadcasted_iota(I32, (ck, ck), 1) <= lax.broadcasted_iota(I32, (ck, ck), 0),
                        1.0, 0.0).astype(BF16)

        def strike(c, seen):
            key = keys_ref[c]
            tie = (key == thr_t) & tied_t
            rank = _mm(tri, jnp.where(tie, 1.0, 0.0).astype(BF16)) + tile(seen)
            keys_ref[c] = jnp.where(tie & (rank > keep_t), INT_MIN, key)
            return rep(rank[ck - 1:ck, :])
        lax.fori_loop(0, nk, strike, jnp.zeros((SUB, tq), F32))

    qat = lanes(qat_ref)
    sel_t = tile(jnp.where(thr == INT_MIN, INT_MIN + 1, thr))
    qts = [qat[h * HEAD_DIM:(h + 1) * HEAD_DIM, :] for h in range(A_HEADS)]

    def step(c, width, carry):
        k = kv_ref[pl.ds(pl.multiple_of(c * ck, ck), width * ck), :][:, 0:HEAD_DIM]
        vt = jnp.concatenate([vat_ref[c + j] for j in range(width)], axis=1)
        bias = jnp.concatenate([jnp.where(keys_ref[c + j] >= sel_t, 0.0, NEG_BIG) for j in range(width)], axis=0)
        ss = _flash_logits([k] * A_HEADS, qts)
        return _flash_update(ss, [vt] * A_HEADS, carry, bias)

    carry = (_flash_init(HEAD_DIM, tq),) * A_HEADS
    o_ref[...] = _flash_out(_walk_chunks(nk, step, carry, (4, 2, 1) if per == 1 else (2, 1)))


def _dsa(iqt, iwt, qat, ik, kva, vat, b, s, tq=512):
    n = b * s
    ck = iqt.shape[2]
    tq = min(tq, s)
    nq, nc, per = s // tq, s // ck, tq // ck
    nsel = min(TOPK_MAX, s // 4)
    assert nsel <= ck and tq % ck == 0
    qtile = lambda w: pl.BlockSpec((per, w, ck), lambda bi, i: (bi * nq + i, 0, 0))
    seq_rows = pl.BlockSpec((s, LANES), lambda bi, i: (bi, 0))
    return pl.pallas_call(
        functools.partial(_dsa_kernel, tq=tq, nsel=nsel, seq=s),
        out_shape=jax.ShapeDtypeStruct((n, A_HEADS * HEAD_DIM), BF16), grid=(b, nq),
        in_specs=[qtile(IDX_HEADS * IDX_DIM), qtile(IDX_HEADS), qtile(A_HEADS * HEAD_DIM), seq_rows, seq_rows,
                  pl.BlockSpec((nc, HEAD_DIM, ck), lambda bi, i: (bi, 0, 0))],
        out_specs=pl.BlockSpec((tq, A_HEADS * HEAD_DIM), lambda bi, i: (bi * nq + i, 0)),
        scratch_shapes=[pltpu.VMEM((nc, ck, tq), I32), pltpu.VMEM((nc, ck, tq), jnp.int16),
                        pltpu.VMEM((nc, ck, tq), jnp.int16), pltpu.VMEM((ck, tq), F32)],
        compiler_params=_params(("arbitrary", "arbitrary")), name="dsa_attention",
    )(iqt, iwt, qat, ik, kva, vat)


def _mla_kernel(qt_ref, k_ref, vt_ref, o_ref, *, tq):
    i = pl.program_id(1)
    qt = qt_ref[0]
    kpos = lax.broadcasted_iota(I32, (tq, tq), 0)
    qpos = lax.broadcasted_iota(I32, (tq, tq), 1)
    diag_bias = jnp.where(kpos <= qpos, 0.0, NEG_BIG)

    heads = range(B_HEADS)
    qts = [qt[_slab(h), :] for h in heads]

    def step(c, width, carry, bias=None):
        rows = pl.ds(pl.multiple_of(c * tq, tq), width * tq)
        vt = jnp.concatenate([vt_ref[c + j] for j in range(width)], axis=1)
        ss = _flash_logits([k_ref[rows, _slab(h)] for h in heads], qts)
        return _flash_update(ss, [vt[h * B_V:(h + 1) * B_V, :] for h in heads], carry, bias)

    carry = (_flash_init(B_V, tq),) * B_HEADS
    o_ref[...] = _flash_out(step(i, 1, _walk_chunks(i, step, carry), diag_bias))


def _mla(qbt, kb, vbt, b, s, tq=256):
    n = b * s
    nq = s // tq
    assert qbt.shape[2] == tq
    return pl.pallas_call(
        functools.partial(_mla_kernel, tq=tq),
        out_shape=jax.ShapeDtypeStruct((n, B_HEADS * B_V), BF16), grid=(b, nq),
        in_specs=[pl.BlockSpec((1, B_HEADS * LANES, tq), lambda bi, i: (bi * nq + i, 0, 0)),
                  pl.BlockSpec((s, B_HEADS * LANES), lambda bi, i: (bi, 0)),
                  pl.BlockSpec((nq, B_HEADS * B_V, tq), lambda bi, i: (bi, 0, 0))],
        out_specs=pl.BlockSpec((tq, B_HEADS * B_V), lambda bi, i: (bi * nq + i, 0)),
        compiler_params=_params(("arbitrary", "arbitrary")), name="mla_attention",
    )(qbt, kb, vbt)


def _swa_kernel(sink_ref, q_ref, kp_ref, kc_ref, vp_ref, vc_ref, o_ref, *, tq):
    i = pl.program_id(1)
    q = q_ref[...]
    keys = jnp.concatenate([kp_ref[...], kc_ref[...]], axis=0)
    vals = jnp.concatenate([vp_ref[...], vc_ref[...]], axis=0)
    nkeys = WINDOW + tq
    qpos = lax.broadcasted_iota(I32, (tq, nkeys), 0) + i * tq
    kpos = lax.broadcasted_iota(I32, (tq, nkeys), 1) + i * tq - WINDOW
    visible = (kpos > qpos - WINDOW) & (kpos <= qpos) & (kpos >= 0)
    bias = jnp.where(visible, 0.0, NEG_BIG)
    group = C_HEADS // C_KV_HEADS
    heads = range(C_HEADS)
    ks = [keys[:, j * HEAD_DIM:(j + 1) * HEAD_DIM] for j in range(C_KV_HEADS)]
    vs = [vals[:, j * HEAD_DIM:(j + 1) * HEAD_DIM] for j in range(C_KV_HEADS)]
    ss = [_mm_nt(q[:, h * HEAD_DIM:(h + 1) * HEAD_DIM], ks[h // group]) + bias for h in heads]
    ms = [jnp.maximum(jnp.max(s, axis=1, keepdims=True), sink_ref[h]) for h, s in zip(heads, ss)]
    ps = [jnp.exp(s - m) for s, m in zip(ss, ms)]
    ws = [p / (jnp.sum(p, axis=1, keepdims=True) + jnp.exp(sink_ref[h] - m)) for h, p, m in zip(heads, ps, ms)]
    outs = [_mm(w.astype(BF16), vs[h // group]) for h, w in zip(heads, ws)]
    o_ref[...] = jnp.concatenate(outs, axis=1).astype(BF16)


def _swa(sinks, qc, kc, vc, b, s, tq=256):
    n = b * s
    nq = s // tq
    per = tq // WINDOW
    qrow = lambda w: pl.BlockSpec((tq, w), lambda bi, i: (bi * nq + i, 0))
    prev = pl.BlockSpec((WINDOW, LANES), lambda bi, i: (jnp.maximum((bi * nq + i) * per - 1, 0), 0))
    return pl.pallas_call(
        functools.partial(_swa_kernel, tq=tq),
        out_shape=jax.ShapeDtypeStruct((n, C_HEADS * HEAD_DIM), BF16), grid=(b, nq),
        in_specs=[pl.BlockSpec(memory_space=pltpu.SMEM), qrow(512), prev, qrow(LANES), prev, qrow(LANES)],
        out_specs=qrow(512),
        compiler_params=_params(("arbitrary", "arbitrary")), name="swa_attention",
    )(sinks, qc, kc, kc, vc, vc)


def _merge_kernel(ya_ref, yb_ref, yc_ref, gt_ref, x_ref, wa_ref, wb_ref, wc_ref, wo_ref, g_ref,
                  xo_ref, h_ref):
    d = x_ref.shape[1]
    merged = (gt_ref[:, 0:d] * _mm(ya_ref[...], wa_ref[...])
              + gt_ref[:, d:2 * d] * _mm(yb_ref[...], wb_ref[...])
              + gt_ref[:, 2 * d:3 * d] * _mm(yc_ref[...], wc_ref[...]))
    xn = x_ref[...] + _mm(merged.astype(BF16), wo_ref[...])
    xo_ref[...] = xn
    h = xn * lax.rsqrt(jnp.mean(xn * xn, axis=-1, keepdims=True) + EPS) * g_ref[...]
    h_ref[...] = h.astype(h_ref.dtype)


def _merge(ya, yb, yc, gt, xf, wa, wb, wc, wo, g, h_dtype, tm=256):
    n, d = xf.shape
    row = lambda w: pl.BlockSpec((tm, w), lambda i: (i, 0))
    full = lambda a: pl.BlockSpec(a.shape, lambda i: (0,) * a.ndim)
    return pl.pallas_call(
        _merge_kernel,
        out_shape=(jax.ShapeDtypeStruct((n, d), F32), jax.ShapeDtypeStruct((n, d), h_dtype)),
        grid=(n // tm,),
        in_specs=[row(256), row(256), row(512), row(3 * d), row(d), full(wa), full(wb), full(wc), full(wo),
                  full(g)],
        out_specs=(row(d), row(d)),
        compiler_params=_params(("arbitrary",)), name="merge_out_proj",
    )(ya, yb, yc, gt, xf, wa, wb, wc, wo, g)


def _ffn_kernel(h_ref, x_ref, w1_hbm, w3_hbm, w2_hbm, o_ref, w1_ref, w3_ref, w2_ref, sem, *, chunk):
    @pl.when(pl.program_id(0) == 0)
    def _():
        copies = [pltpu.make_async_copy(src, dst, sem.at[j]) for j, (src, dst) in
                  enumerate(((w1_hbm, w1_ref), (w3_hbm, w3_ref), (w2_hbm, w2_ref)))]
        for cp in copies:
            cp.start()
        for cp in copies:
            cp.wait()

    h = h_ref[...]
    acc = x_ref[...]
    for j in range(w1_ref.shape[1] // chunk):
        cols = slice(j * chunk, (j + 1) * chunk)
        a = _mm(h, w1_ref[:, cols])
        mid = a * _sigmoid(a) * _mm(h, w3_ref[:, cols])
        acc = acc + _mm(mid.astype(BF16), w2_ref[cols, :])
    o_ref[...] = acc


def _ffn(h2, xf, w1, w3, w2, tm=256, chunk=1408):
    n, d = xf.shape
    row = lambda w: pl.BlockSpec((tm, w), lambda i: (i, 0))
    hbm = pl.BlockSpec(memory_space=pl.ANY)
    return pl.pallas_call(
        functools.partial(_ffn_kernel, chunk=chunk),
        out_shape=jax.ShapeDtypeStruct((n, d), F32), grid=(n // tm,),
        in_specs=[row(d), row(d), hbm, hbm, hbm], out_specs=row(d),
        scratch_shapes=[pltpu.VMEM(w1.shape, BF16), pltpu.VMEM(w3.shape, BF16), pltpu.VMEM(w2.shape, BF16),
                        pltpu.SemaphoreType.DMA((3,))],
        compiler_params=_params(("arbitrary",)), name="dense_swiglu",
    )(h2, xf, w1, w3, w2)


def _route_kernel(h_ref, wr_ref, tri_ref, route_ref, cnt_ref, carry_ref):
    @pl.when(pl.program_id(0) == 0)
    def _():
        carry_ref[...] = jnp.zeros_like(carry_ref)

    logits = jnp.dot(h_ref[...], wr_ref[...], precision=lax.Precision.HIGHEST, preferred_element_type=F32)
    lane = lax.broadcasted_iota(I32, logits.shape, 1).astype(F32)
    lg = jnp.where(lane < N_EXPERTS, logits, -jnp.inf)
    m1 = jnp.max(lg, axis=1, keepdims=True)
    e1 = jnp.min(jnp.where(lg == m1, lane, float(LANES)), axis=1, keepdims=True)
    lg2 = jnp.where(lane == e1, -jnp.inf, lg)
    m2 = jnp.max(lg2, axis=1, keepdims=True)
    e2 = jnp.min(jnp.where(lg2 == m2, lane, float(LANES)), axis=1, keepdims=True)
    ex = jnp.exp(m2 - m1)
    g1 = 1.0 / (1.0 + ex)
    g2 = ex / (1.0 + ex)
    onehot = jnp.where((lane == e1) | (lane == e2), 1.0, 0.0)
    before = _mm(tri_ref[...], onehot.astype(BF16)) + carry_ref[0:1, :]
    r1 = jnp.sum(jnp.where(lane == e1, before, 0.0), axis=1, keepdims=True)
    r2 = jnp.sum(jnp.where(lane == e2, before, 0.0), axis=1, keepdims=True)
    out = jnp.zeros_like(logits)
    for idx, val in enumerate((e1, e2, g1, g2, r1, r2)):
        out = jnp.where(lane == idx, val, out)
    route_ref[...] = out
    total = carry_ref[0:1, :] + jnp.sum(onehot, axis=0, keepdims=True)
    carry_ref[...] = jnp.broadcast_to(total, carry_ref.shape)
    cnt_ref[...] = jnp.broadcast_to(total, cnt_ref.shape)


def _route(h2, wr, tm=256):
    n, d = h2.shape
    tri = jnp.asarray(np.tril(np.ones((tm, tm), np.float32), -1), BF16)
    full = lambda a: pl.BlockSpec(a.shape, lambda i: (0,) * a.ndim)
    return pl.pallas_call(
        _route_kernel,
        out_shape=(jax.ShapeDtypeStruct((n, LANES), F32), jax.ShapeDtypeStruct((8, LANES), F32)),
        grid=(n // tm,),
        in_specs=[pl.BlockSpec((tm, d), lambda i: (i, 0)), full(wr), full(tri)],
        out_specs=(pl.BlockSpec((tm, LANES), lambda i: (i, 0)), pl.BlockSpec((8, LANES), lambda i: (0, 0))),
        scratch_shapes=[pltpu.VMEM((8, LANES), F32)],
        compiler_params=_params(("arbitrary",)), name="moe_route",
    )(h2, wr, tri)


def _scatter_kernel(dest_ref, h_ref, xin_hbm, xout_hbm, sem, *, tt):
    del xin_hbm

    def copy(r, k):
        d = dest_ref[0, 0, 2 * r + k]
        return pltpu.make_async_copy(h_ref.at[pl.ds(r, 1), :], xout_hbm.at[pl.ds(d, 1), :], sem)

    def start(r, carry):
        copy(r, 0).start()
        copy(r, 1).start()
        return carry

    def wait(r, carry):
        copy(r, 0).wait()
        copy(r, 1).wait()
        return carry

    lax.fori_loop(0, tt, start, 0)
    lax.fori_loop(0, tt, wait, 0)


def _scatter(dest3, h2, rows, tt):
    n, w = h2.shape
    hbm = pl.BlockSpec(memory_space=pl.ANY)
    return pl.pallas_call(
        functools.partial(_scatter_kernel, tt=tt),
        out_shape=jax.ShapeDtypeStruct((rows, w), h2.dtype), grid=(n // tt,),
        in_specs=[pl.BlockSpec((1, 1, 2 * tt), lambda i: (i, 0, 0), memory_space=pltpu.SMEM),
                  pl.BlockSpec((tt, w), lambda i: (i, 0)), hbm],
        out_specs=hbm, scratch_shapes=[pltpu.SemaphoreType.DMA(())],
        input_output_aliases={2: 0},
        compiler_params=_params(("arbitrary",)), name="moe_scatter",
    )(dest3, h2, jnp.zeros((rows, w), h2.dtype))


def _expert_kernel(be_ref, nv_ref, x_ref, w1_hbm, w3_hbm, w2_hbm, y_ref, w1_ref, w3_ref, w2_ref, sem, *, chunk):
    blk = pl.program_id(0)
    e = be_ref[blk]

    @pl.when((blk == 0) | (e != be_ref[jnp.maximum(blk - 1, 0)]))
    def _():
        copies = [pltpu.make_async_copy(src.at[e], dst, sem.at[j]) for j, (src, dst) in
                  enumerate(((w1_hbm, w1_ref), (w3_hbm, w3_ref), (w2_hbm, w2_ref)))]
        for cp in copies:
            cp.start()
        for cp in copies:
            cp.wait()

    @pl.when(blk < nv_ref[0])
    def _():
        x = x_ref[...].astype(BF16)
        acc = jnp.zeros(y_ref.shape, F32)
        for j in range(w1_ref.shape[1] // chunk):
            cols = slice(j * chunk, (j + 1) * chunk)
            a = _mm(x, w1_ref[:, cols])
            mid = a * _sigmoid(a) * _mm(x, w3_ref[:, cols])
            acc = acc + _mm(mid.astype(BF16), w2_ref[cols, :])
        y_ref[...] = acc

    @pl.when(blk >= nv_ref[0])
    def _():
        y_ref[...] = jnp.zeros(y_ref.shape, F32)


def _experts(block_e, nvalid, xbuf, w1, w3, w2, chunk=512):
    rows, d = xbuf.shape
    hbm = pl.BlockSpec(memory_space=pl.ANY)
    blk = pl.BlockSpec((MOE_BLOCK, d), lambda i, be, nv: (i, 0))
    return pl.pallas_call(
        functools.partial(_expert_kernel, chunk=chunk),
        out_shape=jax.ShapeDtypeStruct((rows, d), F32),
        grid_spec=pltpu.PrefetchScalarGridSpec(
            num_scalar_prefetch=2, grid=(rows // MOE_BLOCK,),
            in_specs=[blk, hbm, hbm, hbm], out_specs=blk,
            scratch_shapes=[pltpu.VMEM(w1.shape[1:], BF16), pltpu.VMEM(w3.shape[1:], BF16),
                            pltpu.VMEM(w2.shape[1:], BF16), pltpu.SemaphoreType.DMA((3,))]),
        compiler_params=_params(("arbitrary",)), name="moe_experts",
    )(block_e, nvalid, xbuf, w1, w3, w2)


def _combine_kernel(dest_ref, y_hbm, x_ref, route_ref, o_ref, buf_ref, sem, *, tt):
    def copy(r, k):
        d = dest_ref[0, 0, 2 * r + k]
        return pltpu.make_async_copy(y_hbm.at[pl.ds(d, 1), :], buf_ref.at[k, pl.ds(r, 1), :], sem)

    def start(r, carry):
        copy(r, 0).start()
        copy(r, 1).start()
        return carry

    def wait(r, carry):
        copy(r, 0).wait()
        copy(r, 1).wait()
        return carry

    lax.fori_loop(0, tt, start, 0)
    lax.fori_loop(0, tt, wait, 0)
    rt = route_ref[...]
    o_ref[...] = x_ref[...] + (buf_ref[0] * rt[:, 2:3] + buf_ref[1] * rt[:, 3:4])


def _combine(dest3, ybuf, xf, route, tt):
    n, d = xf.shape
    row = lambda w: pl.BlockSpec((tt, w), lambda i: (i, 0))
    return pl.pallas_call(
        functools.partial(_combine_kernel, tt=tt),
        out_shape=jax.ShapeDtypeStruct((n, d), F32), grid=(n // tt,),
        in_specs=[pl.BlockSpec((1, 1, 2 * tt), lambda i: (i, 0, 0), memory_space=pltpu.SMEM),
                  pl.BlockSpec(memory_space=pl.ANY), row(d), row(LANES)],
        out_specs=row(d),
        scratch_shapes=[pltpu.VMEM((2, tt, d), F32), pltpu.SemaphoreType.DMA(())],
        compiler_params=_params(("arbitrary",)), name="moe_combine",
    )(dest3, ybuf, xf, route)


def _moe(xf, h2, w_router, w1, w3, w2, tt=256):
    n, d = xf.shape
    wr = jnp.pad(w_router, ((0, 0), (0, LANES - N_EXPERTS)))
    route, cnt = _route(h2, wr)
    counts = cnt[0, :N_EXPERTS].astype(I32)
    padded = (counts + MOE_BLOCK - 1) // MOE_BLOCK * MOE_BLOCK
    pend = jnp.cumsum(padded)
    pstart = pend - padded
    dest = pstart[route[:, 0:2].astype(I32)] + route[:, 4:6].astype(I32)
    dest3 = dest.reshape(n // tt, 1, 2 * tt)
    n_blocks = n * 2 // MOE_BLOCK + N_EXPERTS
    rows = n_blocks * MOE_BLOCK
    block_e = jnp.minimum(jnp.searchsorted(pend, jnp.arange(n_blocks) * MOE_BLOCK, side='right'),
                          N_EXPERTS - 1).astype(I32)
    nvalid = (pend[-1:] // MOE_BLOCK).astype(I32)
    xbuf = _scatter(dest3, h2, rows, tt)
    ybuf = _experts(block_e, nvalid, xbuf, w1, w3, w2)
    return _combine(dest3, ybuf, xf, route, tt)


def _tile_row(v, width=LANES):
    v = v.astype(F32)
    return jnp.tile(v, width // v.shape[0])


def _layer_params(l, attn_norm_g, w_in, a_q_norm_g, a_k_norm_g, b_cq_norm_g, b_ckv_norm_g, b_w_uq, b_w_ukv,
                  b_qn_g, b_qr_g, b_kn_g, b_kr_g, c_q_norm_g, c_k_norm_g):
    d = w_in.shape[1]
    sizes = (A_HEADS * HEAD_DIM, HEAD_DIM, HEAD_DIM, IDX_HEADS * IDX_DIM, IDX_DIM, IDX_HEADS,
             b_w_uq.shape[1], b_w_ukv.shape[1], B_ROPE, C_HEADS * HEAD_DIM, C_KV_HEADS * HEAD_DIM,
             C_KV_HEADS * HEAD_DIM, 3 * d)
    qa, ka, va, iq, ik, iw, cq, ckv, kr, qc, kc, vc, gates = jnp.split(w_in[l], np.cumsum(sizes)[:-1].tolist(), axis=1)
    z = lambda k: jnp.zeros((d, k), F32)
    w_r = jnp.concatenate([qa, ka, va, iq, ik, iw, z(LANES - IDX_DIM - IDX_HEADS), cq, ckv,
                           z(B_NOPE), kr, z(LANES - B_NOPE - B_ROPE), qc, kc, vc, gates], axis=1).astype(BF16)
    uq = b_w_uq[l].reshape(-1, B_HEADS, B_NOPE + B_ROPE)
    uq = jnp.pad(uq, ((0, 0), (0, 0), (0, LANES - B_NOPE - B_ROPE))).reshape(-1, B_HEADS * LANES).astype(BF16)
    ukv = b_w_ukv[l].reshape(-1, B_HEADS, B_NOPE + B_V)
    uk = jnp.pad(ukv[:, :, :B_NOPE], ((0, 0), (0, 0), (0, LANES - B_NOPE))).reshape(-1, B_HEADS * LANES)
    uv = ukv[:, :, B_NOPE:].reshape(-1, B_HEADS * B_V)
    wukv = jnp.concatenate([uk, uv], axis=1).astype(BF16)
    zeros = lambda k: jnp.zeros((k,), F32)
    grows = jnp.stack([
        _tile_row(a_q_norm_g[l]),
        jnp.concatenate([a_k_norm_g[l], jnp.ones((HEAD_DIM,), F32)]),
        jnp.concatenate([b_qn_g[l], b_qr_g[l], zeros(LANES - B_NOPE - B_ROPE)]),
        jnp.concatenate([b_kn_g[l], zeros(LANES - B_NOPE)]),
        jnp.concatenate([zeros(B_NOPE), b_kr_g[l], zeros(LANES - B_NOPE - B_ROPE)]),
        _tile_row(c_q_norm_g[l]),
        _tile_row(c_k_norm_g[l]),
        b_ckv_norm_g[l],
    ]).astype(F32)
    return (attn_norm_g[l][None, :], w_r, uq, wukv, b_cq_norm_g[l][None, :], grows)


def _seg_matrices():
    lane = np.arange(LANES)
    g64 = lane // 64
    m64 = (g64[:, None] == g64[None, :]) / 64.0
    gqb = np.where(lane < B_NOPE, 0, np.where(lane < B_NOPE + B_ROPE, 1, 2))
    size = np.where(lane < B_NOPE, B_NOPE, B_ROPE)
    mqb = (gqb[:, None] == gqb[None, :]) / size[None, :]
    return jnp.asarray(np.stack([m64, mqb]), BF16)


def kernel(x, positions, attn_norm_g, w_in, a_q_norm_g, a_k_norm_g, b_cq_norm_g, b_ckv_norm_g, b_w_uq, b_w_ukv,
           b_qn_g, b_qr_g, b_kn_g, b_kr_g, c_q_norm_g, c_k_norm_g, c_sinks, w_a_out, w_b_out, w_c_out, w_o,
           ffn_norm_g, ffn_w1, ffn_w3, ffn_w2, router_w, moe_w1, moe_w3, moe_w2):
    b, s, d = x.shape
    n = b * s
    depth = w_in.shape[0]
    xf = x.reshape(n, d)
    tabs = _rope_tables(positions.reshape(n, 1).astype(F32))
    mseg = _seg_matrices()
    for l in range(depth):
        g, w_r, uq, wukv, gcq, grows = _layer_params(
            l, attn_norm_g, w_in, a_q_norm_g, a_k_norm_g, b_cq_norm_g, b_ckv_norm_g, b_w_uq, b_w_ukv,
            b_qn_g, b_qr_g, b_kn_g, b_kr_g, c_q_norm_g, c_k_norm_g)
        qat, kva, vat, iqt, ik, iwt, qbt, kb, vbt, qc, kc, vc, gt = _proj(xf, g, w_r, uq, wukv, gcq, grows, mseg,
                                                                         tabs)
        ya = _dsa(iqt, iwt, qat, ik, kva, vat, b, s)
        yb = _mla(qbt, kb, vbt, b, s)
        yc = _swa(c_sinks[l], qc, kc, vc, b, s)
        dense = l % 2 == 0
        xf, h2 = _merge(ya, yb, yc, gt, xf, w_a_out[l].astype(BF16), w_b_out[l].astype(BF16),
                        w_c_out[l].astype(BF16), w_o[l].astype(BF16), ffn_norm_g[l][None, :],
                        BF16 if dense else F32)
        if dense:
            xf = _ffn(h2, xf, ffn_w1[l // 2].astype(BF16), ffn_w3[l // 2].astype(BF16), ffn_w2[l // 2].astype(BF16))
        else:
            xf = _moe(xf, h2, router_w[l // 2], moe_w1[l // 2].astype(BF16),
                      moe_w3[l // 2].astype(BF16), moe_w2[l // 2].astype(BF16))
    return xf.reshape(b, s, d)
```

```python
import functools

import numpy as np
import jax
import jax.numpy as jnp
from jax import lax
from jax.experimental import pallas as pl
from jax.experimental.pallas import tpu as pltpu

F32, BF16, I32 = jnp.float32, jnp.bfloat16, jnp.int32

EPS = 1e-6
ROPE_THETA = 10000.0
HEAD_DIM = 64
A_HEADS = 4
IDX_HEADS = 8
IDX_DIM = 32
TOPK_MAX = 256
B_HEADS = 4
B_NOPE = 64
B_ROPE = 32
B_V = 64
C_HEADS = 8
C_KV_HEADS = 2
WINDOW = 128
N_EXPERTS = 8
MOE_BLOCK = 256

LANES = 128
SUB = 8
VMEM_LIMIT = 56 * 1024 * 1024
INT_MIN = -2 ** 31
NEG_BIG = -1e30
LOG2E = 1.4426950408889634

C_QA, C_KVA, C_IQ, C_IKW, C_CQ, C_CKV, C_KR, C_QC, C_KC, C_VC, C_G = (
    0, 256, 384, 640, 768, 1024, 1152, 1280, 1792, 1920, 2048)


def _params(sem):
    return pltpu.CompilerParams(dimension_semantics=sem, vmem_limit_bytes=VMEM_LIMIT)


def _mm(a, b):
    return jnp.dot(a, b, preferred_element_type=F32)


def _mm_nt(a, b):
    return lax.dot_general(a, b, (((1,), (1,)), ((), ())), preferred_element_type=F32)


def _sigmoid(v):
    return 1.0 / (1.0 + jnp.exp(-v))


def _rope_table_kernel(pos_ref, f_ref, sg_ref, c64_ref, s64_ref, c32_ref, s32_ref):
    pos = pos_ref[...]
    a = pos * f_ref[0:1, :]
    c64_ref[...] = jnp.cos(a)
    s64_ref[...] = jnp.sin(a) * sg_ref[0:1, :]
    a = pos * f_ref[1:2, :]
    c32_ref[...] = jnp.cos(a)
    s32_ref[...] = jnp.sin(a) * sg_ref[1:2, :]


def _rope_tables(pos_f, tm=1024):
    n = pos_f.shape[0]
    lane = np.arange(LANES)
    f32 = ROPE_THETA ** (-jnp.arange(32, dtype=F32) / 32)
    f16 = ROPE_THETA ** (-jnp.arange(16, dtype=F32) / 16)
    freqs = jnp.stack([f32[lane % 32], f16[lane % 16]])
    signs = jnp.asarray(np.stack([np.where(lane % 64 < 32, -1.0, 1.0),
                                  np.where(lane % 32 < 16, -1.0, 1.0)]), F32)
    tab = jax.ShapeDtypeStruct((n, LANES), F32)
    row = pl.BlockSpec((tm, LANES), lambda i: (i, 0))
    par = pl.BlockSpec((2, LANES), lambda i: (0, 0))
    return pl.pallas_call(
        _rope_table_kernel, out_shape=(tab,) * 4, grid=(n // tm,),
        in_specs=[pl.BlockSpec((tm, 1), lambda i: (i, 0)), par, par],
        out_specs=(row,) * 4, compiler_params=_params(("arbitrary",)), name="rope_tables",
    )(pos_f, freqs, signs)


def _swap_half(y, half):
    lane = lax.broadcasted_iota(I32, y.shape, 1)
    return jnp.where((lane & half) == 0, pltpu.roll(y, LANES - half, 1), pltpu.roll(y, half, 1))


def _rope(y, cos, sin_signed, half):
    return y * cos + _swap_half(y, half) * sin_signed


def _seg_mean_sq(y, mseg):
    sq = y * y
    hi = sq.astype(BF16)
    lo = (sq - hi.astype(F32)).astype(BF16)
    return _mm(hi, mseg) + _mm(lo, mseg)


def _seg_norm(y, mseg, gain):
    return y * lax.rsqrt(_seg_mean_sq(y, mseg) + EPS) * gain


def _slab(s):
    return slice(s * LANES, (s + 1) * LANES)


def _proj_kernel(x_ref, g_ref, w_ref, wuq_ref, wukv_ref, gcq_ref, grows_ref, mseg_ref,
                 c64_ref, s64_ref, c32_ref, s32_ref,
                 qat_ref, kva_ref, vat_ref, iqt_ref, ik_ref, iwt_ref, qbt_ref, kb_ref, vbt_ref,
                 qc_ref, kc_ref, vc_ref, gt_ref):
    x = x_ref[...]
    h = (x * lax.rsqrt(jnp.mean(x * x, axis=-1, keepdims=True) + EPS) * g_ref[...]).astype(BF16)
    c64, s64, c32, s32 = c64_ref[...], s64_ref[...], c32_ref[...], s32_ref[...]
    m64, mqb = mseg_ref[0], mseg_ref[1]
    lane = lax.broadcasted_iota(I32, c64.shape, 1)
    in_rope = (lane >= B_NOPE) & (lane < B_NOPE + B_ROPE)
    cb = jnp.where(in_rope, c32, 1.0)
    sb = jnp.where(in_rope, s32, 0.0)

    p = _mm(h, w_ref[:, C_QA:C_CQ])
    for s in range(2):
        y = _rope(_seg_norm(p[:, _slab(s)], m64, grows_ref[0:1, :]), c64, s64, 32)
        qat_ref[0, _slab(s), :] = (y * (LOG2E * HEAD_DIM ** -0.5)).T.astype(BF16)
    y = p[:, _slab(2)]
    yr = _rope(_seg_norm(y, m64, grows_ref[1:2, :]), c64, s64, 32)
    kva_ref[...] = jnp.where(lane < HEAD_DIM, yr, y).astype(BF16)
    vat_ref[0] = y.T[HEAD_DIM:2 * HEAD_DIM, :].astype(BF16)
    for s in range(2):
        iqt_ref[0, _slab(s), :] = _rope(p[:, _slab(3 + s)], c32, s32, 16).T.astype(BF16)
    y = p[:, _slab(5)]
    ik_ref[...] = jnp.where(lane < IDX_DIM, _rope(y, c32, s32, 16), 0.0).astype(BF16)
    iwt_ref[0] = (y * (IDX_HEADS * IDX_DIM) ** -0.5).T[IDX_DIM:IDX_DIM + IDX_HEADS, :]

    p = _mm(h, w_ref[:, C_CQ:C_QC])
    cq = p[:, 0:256]
    cqn = cq * lax.rsqrt(jnp.mean(cq * cq, axis=-1, keepdims=True) + EPS) * gcq_ref[...]
    qb = _mm(cqn.astype(BF16), wuq_ref[...])
    for s in range(B_HEADS):
        y = _rope(_seg_norm(qb[:, _slab(s)], mqb, grows_ref[2:3, :]), cb, sb, 16)
        qbt_ref[0, _slab(s), :] = (y * (LOG2E * (B_NOPE + B_ROPE) ** -0.5)).T.astype(BF16)
    ckv = p[:, 256:384]
    ckvn = ckv * lax.rsqrt(jnp.mean(ckv * ckv, axis=-1, keepdims=True) + EPS) * grows_ref[7:8, :]
    kvb = _mm(ckvn.astype(BF16), wukv_ref[...])
    krs = p[:, 384:512]
    kr = krs * lax.rsqrt(jnp.sum(krs * krs, axis=-1, keepdims=True) * (1.0 / B_ROPE) + EPS) * grows_ref[4:5, :]
    kr = _rope(kr, cb, sb, 16)
    for s in range(B_HEADS):
        kb_ref[:, _slab(s)] = (_seg_norm(kvb[:, _slab(s)], m64, grows_ref[3:4, :]) + kr).astype(BF16)
    for s in range(2):
        vbt_ref[0, _slab(s), :] = kvb[:, 512 + s * LANES:512 + (s + 1) * LANES].T.astype(BF16)

    p = _mm(h, w_ref[:, C_QC:C_G])
    for s in range(4):
        y = _rope(_seg_norm(p[:, _slab(s)], m64, grows_ref[5:6, :]), c64, s64, 32)
        qc_ref[:, _slab(s)] = (y * HEAD_DIM ** -0.5).astype(BF16)
    y = _rope(_seg_norm(p[:, _slab(4)], m64, grows_ref[6:7, :]), c64, s64, 32)
    kc_ref[...] = y.astype(BF16)
    vc_ref[...] = p[:, _slab(5)].astype(BF16)

    for c in range(3):
        lo = C_G + c * 1024
        gt_ref[:, c * 1024:(c + 1) * 1024] = _sigmoid(_mm(h, w_ref[:, lo:lo + 1024])).astype(gt_ref.dtype)


def _proj(xf, g, w_r, wuq, wukv, gcq, grows, mseg, tabs, tm=256):
    n, d = xf.shape
    row = lambda w: pl.BlockSpec((tm, w), lambda i: (i, 0))
    full = lambda a: pl.BlockSpec(a.shape, lambda i: (0,) * a.ndim)
    outs = [("t", 256, BF16), ("r", 128, BF16), ("t", HEAD_DIM, BF16), ("t", 256, BF16), ("r", 128, BF16),
            ("t", IDX_HEADS, F32), ("t", 512, BF16), ("r", 512, BF16), ("t", 256, BF16),
            ("r", 512, BF16), ("r", 128, BF16), ("r", 128, BF16), ("r", 3072, BF16)]
    shape = lambda k, w, dt: jax.ShapeDtypeStruct((n // tm, w, tm) if k == "t" else (n, w), dt)
    spec = lambda k, w: pl.BlockSpec((1, w, tm), lambda i: (i, 0, 0)) if k == "t" else row(w)
    return pl.pallas_call(
        _proj_kernel,
        out_shape=tuple(shape(*o) for o in outs),
        grid=(n // tm,),
        in_specs=[row(d), full(g), full(w_r), full(wuq), full(wukv), full(gcq), full(grows), full(mseg)]
        + [row(LANES)] * 4,
        out_specs=tuple(spec(k, w) for k, w, _ in outs),
        compiler_params=_params(("arbitrary",)), name="in_proj",
    )(xf, g, w_r, wuq, wukv, gcq, grows, mseg, *tabs)


def _flash_logits(ks, qts):
    return tuple(_mm(k, qt) for k, qt in zip(ks, qts))


def _flash_update(ss, vts, carries, bias=None):
    if bias is not None:
        ss = [s + bias for s in ss]
    m_new = [jnp.maximum(c[0], jnp.max(s, axis=0, keepdims=True)) for c, s in zip(carries, ss)]
    ps = [jnp.exp2(s - m) for s, m in zip(ss, m_new)]
    pvs = [_mm(vt, p.astype(BF16)) for vt, p in zip(vts, ps)]
    out = []
    for (m, l, acc), mn, p, pv in zip(carries, m_new, ps, pvs):
        alpha = jnp.exp2(m - mn)
        out.append((mn, alpha * l + jnp.sum(p, axis=0, keepdims=True), alpha * acc + pv))
    return tuple(out)


def _walk_chunks(n, step, carry, widths=(4, 2, 1)):
    start = 0
    for w in widths:
        trips = (n - start) // w
        carry = lax.fori_loop(0, trips, lambda j, cr, start=start, w=w: step(start + j * w, w, cr), carry)
        start = start + trips * w
    return carry


def _flash_init(dv, tq):
    return jnp.full((1, tq), NEG_BIG, F32), jnp.zeros((1, tq), F32), jnp.zeros((dv, tq), F32)


def _flash_out(results):
    return jnp.concatenate([acc / l for _, l, acc in results], axis=0).T.astype(BF16)


def _dsa_kernel(iqt_ref, iwt_ref, qat_ref, ik_ref, kv_ref, vat_ref, o_ref, keys_ref, hi_ref, lo_ref, gmax_ref, *,
                tq, nsel, seq):
    per, _, ck = iqt_ref.shape
    i = pl.program_id(1)
    nk = (i + 1) * per

    def lanes(ref):
        return jnp.concatenate([ref[j] for j in range(per)], axis=1)

    iqt = lanes(iqt_ref)
    iwt = lanes(iwt_ref)
    half = ck // (2 * per)
    qpos_h = lax.broadcasted_iota(I32, (half, tq), 1) + i * tq
    krow_h = lax.broadcasted_iota(I32, (half, tq), 0)
    gmax_ref[...] = jnp.full(gmax_ref.shape, -jnp.inf, F32)

    def to_key(score):
        bits = lax.bitcast_convert_type(score, I32)
        return bits ^ ((bits >> 31) & 0x7FFFFFFF)

    def score_body(c, carry):
        for r in range(ck // half):
            rows = pl.ds(pl.multiple_of(c * ck + r * half, half), half)
            ikc = ik_ref[rows, :][:, 0:IDX_DIM]
            acc = jnp.zeros((half, tq), F32)
            for h in range(IDX_HEADS):
                sc = _mm(ikc, iqt[h * IDX_DIM:(h + 1) * IDX_DIM, :])
                acc = acc + jnp.maximum(sc, 0.0) * iwt[h:h + 1, :]
            causal = krow_h + (c * ck + r * half) <= qpos_h
            key = jnp.where(causal, to_key(acc), INT_MIN)
            keys_ref[c, r * half:(r + 1) * half, :] = key
            hi_ref[c, r * half:(r + 1) * half, :] = (key >> 16).astype(jnp.int16)
            lo_ref[c, r * half:(r + 1) * half, :] = (((key ^ 0x8000) << 16) >> 16).astype(jnp.int16)
            gmax_ref[r * half:(r + 1) * half, :] = jnp.maximum(gmax_ref[r * half:(r + 1) * half, :],
                                                               jnp.where(causal, acc, -jnp.inf))
        return carry

    lax.fori_loop(0, nk, score_body, 0)

    def rep(row):
        return jnp.broadcast_to(row, (SUB, tq))

    def tile(v):
        return jnp.concatenate([v] * (ck // SUB), axis=0)

    def count(preds, *ops):
        nacc = 4 // len(preds)

        def body(c, accs):
            accs = [list(a) for a in accs]
            for g in range(ck // SUB):
                k = keys_ref[c, g * SUB:(g + 1) * SUB, :]
                for a, pred in zip(accs, preds):
                    a[g % nacc] = a[g % nacc] + jnp.where(pred(k, *ops), 1.0, 0.0)
            return tuple(tuple(a) for a in accs)
        accs = lax.fori_loop(0, nk, body, ((jnp.zeros((SUB, tq), F32),) * nacc,) * len(preds))
        return [rep(jnp.sum(sum(a[1:], a[0]), axis=0, keepdims=True)) for a in accs]

    gm = gmax_ref[...]
    smin = rep(jnp.min(gm, axis=0, keepdims=True))
    lo0 = jnp.where(smin == -jnp.inf, INT_MIN, to_key(smin))
    hi0 = to_key(rep(jnp.max(gm, axis=0, keepdims=True)))
    n_pos, n_nonneg = count([lambda k: k > 0, lambda k: k >= 0])
    at_zero = (n_pos < nsel) & (n_nonneg >= nsel)
    lo0 = jnp.where(at_zero, 0, jnp.where(n_pos >= nsel, jnp.maximum(lo0, 1), lo0))
    hi0 = jnp.where(at_zero, 0, jnp.where(n_nonneg < nsel, jnp.minimum(hi0, -1), hi0))
    unknown = float(2 * seq)
    low16 = -2 ** 15

    def pack16(v):
        return jnp.concatenate([v, v], axis=0).astype(jnp.int16)

    def count16(ref, t):
        t16 = pack16(t)
        nacc, rows = 4, 2 * SUB

        def body(c, accs):
            accs = list(accs)
            for g in range(ck // rows):
                hit = ref[c, g * rows:(g + 1) * rows, :] >= t16
                accs[g % nacc] = accs[g % nacc] + jnp.where(hit, jnp.int16(1), jnp.int16(0))
            return tuple(accs)
        accs = lax.fori_loop(0, nk, body, (jnp.zeros((rows, tq), jnp.int16),) * nacc)
        total = sum(a.astype(F32) for a in accs)
        return rep(jnp.sum(total, axis=0, keepdims=True))

    def bisect16(ref, lo, hi, above, want, inert):
        def cond(state):
            return (state[0] < 18) & (state[6] < 0.5)

        def body(state):
            it, lo, hi, n_lo, n_above, exact, _ = state
            all_done = jnp.min(jnp.where(lo == hi, 1.0, 0.0))
            mid = lo + ((hi - lo + 1) >> 1)
            cnt = count16(ref, mid)
            up = cnt >= want
            hit = (cnt == want) & (inert < 0.5)
            lo = jnp.where(up, mid, lo)
            n_lo = jnp.where(up, cnt, n_lo)
            n_above = jnp.where(up, n_above, cnt)
            hi = jnp.where(hit, mid, jnp.where(up, hi, mid - 1))
            exact = jnp.where(hit, 1.0, exact)
            return it + 1, lo, hi, n_lo, n_above, exact, all_done

        zero = jnp.zeros((SUB, tq), F32)
        out = lax.while_loop(cond, body, (jnp.int32(0), lo, hi, zero + unknown, above, zero, jnp.float32(0.0)))
        return out[1], out[3], out[4], out[5]

    zeros = jnp.zeros((SUB, tq), F32)
    top, _, n_above, exact1 = bisect16(hi_ref, lo0 >> 16, hi0 >> 16, jnp.where(n_nonneg < nsel, n_nonneg, 0.0),
                                       float(nsel), zeros)
    done1 = at_zero | (exact1 > 0.5)
    top16 = pack16(top)

    def mask_body(c, carry):
        for g in range(ck // (2 * SUB)):
            rows = slice(g * 2 * SUB, (g + 1) * 2 * SUB)
            lo_ref[c, rows, :] = jnp.where(hi_ref[c, rows, :] == top16, lo_ref[c, rows, :], jnp.int16(low16))
        return carry

    lax.fori_loop(0, nk, mask_body, 0)
    full = jnp.full((SUB, tq), low16, I32)
    bottom, n_bottom, _, _ = bisect16(lo_ref, full, jnp.where(done1, low16, -low16 - 1), zeros, nsel - n_above,
                                      jnp.where(done1, 1.0, 0.0))
    thr = jnp.where(at_zero, 0, jnp.where(done1, top << 16, (top << 16) + (bottom - low16)))
    n_thr = jnp.where(at_zero, n_nonneg, jnp.where(done1, float(nsel), n_above + n_bottom))

    tied = (n_thr != nsel) & (thr != INT_MIN)

    @pl.when(jnp.max(jnp.where(tied, 1.0, 0.0)) > 0.0)
    def _():
        n_above, = count([lambda k, t: k > t], thr)
        keep_t = tile(nsel - n_above)
        thr_t = tile(thr)
        tied_t = tile(tied)
        tri = jnp.where(lax.broadcasted_iota(I32, (ck, ck), 1) <= lax.broadcasted_iota(I32, (ck, ck), 0),
                        1.0, 0.0).astype(BF16)

        def strike(c, seen):
            key = keys_ref[c]
            tie = (key == thr_t) & tied_t
            rank = _mm(tri, jnp.where(tie, 1.0, 0.0).astype(BF16)) + tile(seen)
            keys_ref[c] = jnp.where(tie & (rank > keep_t), INT_MIN, key)
            return rep(rank[ck - 1:ck, :])
        lax.fori_loop(0, nk, strike, jnp.zeros((SUB, tq), F32))

    qat = lanes(qat_ref)
    sel_t = tile(jnp.where(thr == INT_MIN, INT_MIN + 1, thr))
    qts = [qat[h * HEAD_DIM:(h + 1) * HEAD_DIM, :] for h in range(A_HEADS)]

    def step(c, width, carry):
        k = kv_ref[pl.ds(pl.multiple_of(c * ck, ck), width * ck), :][:, 0:HEAD_DIM]
        vt = jnp.concatenate([vat_ref[c + j] for j in range(width)], axis=1)
        bias = jnp.concatenate([jnp.where(keys_ref[c + j] >= sel_t, 0.0, NEG_BIG) for j in range(width)], axis=0)
        ss = _flash_logits([k] * A_HEADS, qts)
        return _flash_update(ss, [vt] * A_HEADS, carry, bias)

    carry = (_flash_init(HEAD_DIM, tq),) * A_HEADS
    o_ref[...] = _flash_out(_walk_chunks(nk, step, carry, (4, 2, 1) if per == 1 else (2, 1)))


def _dsa(iqt, iwt, qat, ik, kva, vat, b, s, tq=512):
    n = b * s
    ck = iqt.shape[2]
    tq = min(tq, s)
    nq, nc, per = s // tq, s // ck, tq // ck
    nsel = min(TOPK_MAX, s // 4)
    assert nsel <= ck and tq % ck == 0
    qtile = lambda w: pl.BlockSpec((per, w, ck), lambda bi, i: (bi * nq + i, 0, 0))
    seq_rows = pl.BlockSpec((s, LANES), lambda bi, i: (bi, 0))
    return pl.pallas_call(
        functools.partial(_dsa_kernel, tq=tq, nsel=nsel, seq=s),
        out_shape=jax.ShapeDtypeStruct((n, A_HEADS * HEAD_DIM), BF16), grid=(b, nq),
        in_specs=[qtile(IDX_HEADS * IDX_DIM), qtile(IDX_HEADS), qtile(A_HEADS * HEAD_DIM), seq_rows, seq_rows,
                  pl.BlockSpec((nc, HEAD_DIM, ck), lambda bi, i: (bi, 0, 0))],
        out_specs=pl.BlockSpec((tq, A_HEADS * HEAD_DIM), lambda bi, i: (bi * nq + i, 0)),
        scratch_shapes=[pltpu.VMEM((nc, ck, tq), I32), pltpu.VMEM((nc, ck, tq), jnp.int16),
                        pltpu.VMEM((nc, ck, tq), jnp.int16), pltpu.VMEM((ck, tq), F32)],
        compiler_params=_params(("arbitrary", "arbitrary")), name="dsa_attention",
    )(iqt, iwt, qat, ik, kva, vat)


def _mla_kernel(qt_ref, k_ref, vt_ref, o_ref, *, tq):
    per, _, ck = qt_ref.shape
    i = pl.program_id(1)
    qt = jnp.concatenate([qt_ref[j] for j in range(per)], axis=1)
    kpos = lax.broadcasted_iota(I32, (per * ck, tq), 0)
    qpos = lax.broadcasted_iota(I32, (per * ck, tq), 1)
    diag_bias = jnp.where(kpos <= qpos, 0.0, NEG_BIG)

    heads = range(B_HEADS)
    qts = [qt[_slab(h), :] for h in heads]

    def step(c, width, carry, bias=None):
        rows = pl.ds(pl.multiple_of(c * ck, ck), width * ck)
        vt = jnp.concatenate([vt_ref[c + j] for j in range(width)], axis=1)
        ss = _flash_logits([k_ref[rows, _slab(h)] for h in heads], qts)
        return _flash_update(ss, [vt[h * B_V:(h + 1) * B_V, :] for h in heads], carry, bias)

    carry = (_flash_init(B_V, tq),) * B_HEADS
    carry = _walk_chunks(i * per, step, carry, (4, 2, 1) if per == 1 else (2, 1))
    o_ref[...] = _flash_out(step(i * per, per, carry, diag_bias))


def _mla(qbt, kb, vbt, b, s, tq=512):
    n = b * s
    ck = qbt.shape[2]
    tq = min(tq, s)
    nq, nc, per = s // tq, s // ck, tq // ck
    assert tq % ck == 0
    return pl.pallas_call(
        functools.partial(_mla_kernel, tq=tq),
        out_shape=jax.ShapeDtypeStruct((n, B_HEADS * B_V), BF16), grid=(b, nq),
        in_specs=[pl.BlockSpec((per, B_HEADS * LANES, ck), lambda bi, i: (bi * nq + i, 0, 0)),
                  pl.BlockSpec((s, B_HEADS * LANES), lambda bi, i: (bi, 0)),
                  pl.BlockSpec((nc, B_HEADS * B_V, ck), lambda bi, i: (bi, 0, 0))],
        out_specs=pl.BlockSpec((tq, B_HEADS * B_V), lambda bi, i: (bi * nq + i, 0)),
        compiler_params=_params(("arbitrary", "arbitrary")), name="mla_attention",
    )(qbt, kb, vbt)


def _swa_kernel(sink_ref, q_ref, kp_ref, kc_ref, vp_ref, vc_ref, o_ref, *, tq):
    i = pl.program_id(1)
    q = q_ref[...]
    keys = jnp.concatenate([kp_ref[...], kc_ref[...]], axis=0)
    vals = jnp.concatenate([vp_ref[...], vc_ref[...]], axis=0)
    nkeys = WINDOW + tq
    qpos = lax.broadcasted_iota(I32, (tq, nkeys), 0) + i * tq
    kpos = lax.broadcasted_iota(I32, (tq, nkeys), 1) + i * tq - WINDOW
    visible = (kpos > qpos - WINDOW) & (kpos <= qpos) & (kpos >= 0)
    bias = jnp.where(visible, 0.0, NEG_BIG)
    group = C_HEADS // C_KV_HEADS
    heads = range(C_HEADS)
    ks = [keys[:, j * HEAD_DIM:(j + 1) * HEAD_DIM] for j in range(C_KV_HEADS)]
    vs = [vals[:, j * HEAD_DIM:(j + 1) * HEAD_DIM] for j in range(C_KV_HEADS)]
    ss = [_mm_nt(q[:, h * HEAD_DIM:(h + 1) * HEAD_DIM], ks[h // group]) + bias for h in heads]
    ms = [jnp.maximum(jnp.max(s, axis=1, keepdims=True), sink_ref[h]) for h, s in zip(heads, ss)]
    ps = [jnp.exp(s - m) for s, m in zip(ss, ms)]
    ws = [p / (jnp.sum(p, axis=1, keepdims=True) + jnp.exp(sink_ref[h] - m)) for h, p, m in zip(heads, ps, ms)]
    outs = [_mm(w.astype(BF16), vs[h // group]) for h, w in zip(heads, ws)]
    o_ref[...] = jnp.concatenate(outs, axis=1).astype(BF16)


def _swa(sinks, qc, kc, vc, b, s, tq=256):
    n = b * s
    nq = s // tq
    per = tq // WINDOW
    qrow = lambda w: pl.BlockSpec((tq, w), lambda bi, i: (bi * nq + i, 0))
    prev = pl.BlockSpec((WINDOW, LANES), lambda bi, i: (jnp.maximum((bi * nq + i) * per - 1, 0), 0))
    return pl.pallas_call(
        functools.partial(_swa_kernel, tq=tq),
        out_shape=jax.ShapeDtypeStruct((n, C_HEADS * HEAD_DIM), BF16), grid=(b, nq),
        in_specs=[pl.BlockSpec(memory_space=pltpu.SMEM), qrow(512), prev, qrow(LANES), prev, qrow(LANES)],
        out_specs=qrow(512),
        compiler_params=_params(("arbitrary", "arbitrary")), name="swa_attention",
    )(sinks, qc, kc, kc, vc, vc)


def _merge_kernel(ya_ref, yb_ref, yc_ref, gt_ref, x_ref, wa_ref, wb_ref, wc_ref, wo_ref, g_ref,
                  xo_ref, h_ref):
    d = x_ref.shape[1]
    merged = (gt_ref[:, 0:d] * _mm(ya_ref[...], wa_ref[...])
              + gt_ref[:, d:2 * d] * _mm(yb_ref[...], wb_ref[...])
              + gt_ref[:, 2 * d:3 * d] * _mm(yc_ref[...], wc_ref[...]))
    xn = x_ref[...] + _mm(merged.astype(BF16), wo_ref[...])
    xo_ref[...] = xn
    h = xn * lax.rsqrt(jnp.mean(xn * xn, axis=-1, keepdims=True) + EPS) * g_ref[...]
    h_ref[...] = h.astype(h_ref.dtype)


def _merge(ya, yb, yc, gt, xf, wa, wb, wc, wo, g, h_dtype, tm=256):
    n, d = xf.shape
    row = lambda w: pl.BlockSpec((tm, w), lambda i: (i, 0))
    full = lambda a: pl.BlockSpec(a.shape, lambda i: (0,) * a.ndim)
    return pl.pallas_call(
        _merge_kernel,
        out_shape=(jax.ShapeDtypeStruct((n, d), F32), jax.ShapeDtypeStruct((n, d), h_dtype)),
        grid=(n // tm,),
        in_specs=[row(256), row(256), row(512), row(3 * d), row(d), full(wa), full(wb), full(wc), full(wo),
                  full(g)],
        out_specs=(row(d), row(d)),
        compiler_params=_params(("arbitrary",)), name="merge_out_proj",
    )(ya, yb, yc, gt, xf, wa, wb, wc, wo, g)


def _ffn_kernel(h_ref, x_ref, w1_hbm, w3_hbm, w2_hbm, o_ref, w1_ref, w3_ref, w2_ref, sem, *, chunk):
    @pl.when(pl.program_id(0) == 0)
    def _():
        copies = [pltpu.make_async_copy(src, dst, sem.at[j]) for j, (src, dst) in
                  enumerate(((w1_hbm, w1_ref), (w3_hbm, w3_ref), (w2_hbm, w2_ref)))]
        for cp in copies:
            cp.start()
        for cp in copies:
            cp.wait()

    h = h_ref[...]
    acc = x_ref[...]
    for j in range(w1_ref.shape[1] // chunk):
        cols = slice(j * chunk, (j + 1) * chunk)
        a = _mm(h, w1_ref[:, cols])
        mid = a * _sigmoid(a) * _mm(h, w3_ref[:, cols])
        acc = acc + _mm(mid.astype(BF16), w2_ref[cols, :])
    o_ref[...] = acc


def _ffn(h2, xf, w1, w3, w2, tm=256, chunk=1408):
    n, d = xf.shape
    row = lambda w: pl.BlockSpec((tm, w), lambda i: (i, 0))
    hbm = pl.BlockSpec(memory_space=pl.ANY)
    return pl.pallas_call(
        functools.partial(_ffn_kernel, chunk=chunk),
        out_shape=jax.ShapeDtypeStruct((n, d), F32), grid=(n // tm,),
        in_specs=[row(d), row(d), hbm, hbm, hbm], out_specs=row(d),
        scratch_shapes=[pltpu.VMEM(w1.shape, BF16), pltpu.VMEM(w3.shape, BF16), pltpu.VMEM(w2.shape, BF16),
                        pltpu.SemaphoreType.DMA((3,))],
        compiler_params=_params(("arbitrary",)), name="dense_swiglu",
    )(h2, xf, w1, w3, w2)


def _route_kernel(h_ref, wr_ref, tri_ref, route_ref, cnt_ref, carry_ref):
    @pl.when(pl.program_id(0) == 0)
    def _():
        carry_ref[...] = jnp.zeros_like(carry_ref)

    logits = jnp.dot(h_ref[...], wr_ref[...], precision=lax.Precision.HIGHEST, preferred_element_type=F32)
    lane = lax.broadcasted_iota(I32, logits.shape, 1).astype(F32)
    lg = jnp.where(lane < N_EXPERTS, logits, -jnp.inf)
    m1 = jnp.max(lg, axis=1, keepdims=True)
    e1 = jnp.min(jnp.where(lg == m1, lane, float(LANES)), axis=1, keepdims=True)
    lg2 = jnp.where(lane == e1, -jnp.inf, lg)
    m2 = jnp.max(lg2, axis=1, keepdims=True)
    e2 = jnp.min(jnp.where(lg2 == m2, lane, float(LANES)), axis=1, keepdims=True)
    ex = jnp.exp(m2 - m1)
    g1 = 1.0 / (1.0 + ex)
    g2 = ex / (1.0 + ex)
    onehot = jnp.where((lane == e1) | (lane == e2), 1.0, 0.0)
    before = _mm(tri_ref[...], onehot.astype(BF16)) + carry_ref[0:1, :]
    r1 = jnp.sum(jnp.where(lane == e1, before, 0.0), axis=1, keepdims=True)
    r2 = jnp.sum(jnp.where(lane == e2, before, 0.0), axis=1, keepdims=True)
    out = jnp.zeros_like(logits)
    for idx, val in enumerate((e1, e2, g1, g2, r1, r2)):
        out = jnp.where(lane == idx, val, out)
    route_ref[...] = out
    total = carry_ref[0:1, :] + jnp.sum(onehot, axis=0, keepdims=True)
    carry_ref[...] = jnp.broadcast_to(total, carry_ref.shape)
    cnt_ref[...] = jnp.broadcast_to(total, cnt_ref.shape)


def _route(h2, wr, tm=256):
    n, d = h2.shape
    tri = jnp.asarray(np.tril(np.ones((tm, tm), np.float32), -1), BF16)
    full = lambda a: pl.BlockSpec(a.shape, lambda i: (0,) * a.ndim)
    return pl.pallas_call(
        _route_kernel,
        out_shape=(jax.ShapeDtypeStruct((n, LANES), F32), jax.ShapeDtypeStruct((8, LANES), F32)),
        grid=(n // tm,),
        in_specs=[pl.BlockSpec((tm, d), lambda i: (i, 0)), full(wr), full(tri)],
        out_specs=(pl.BlockSpec((tm, LANES), lambda i: (i, 0)), pl.BlockSpec((8, LANES), lambda i: (0, 0))),
        scratch_shapes=[pltpu.VMEM((8, LANES), F32)],
        compiler_params=_params(("arbitrary",)), name="moe_route",
    )(h2, wr, tri)


def _scatter_kernel(dest_ref, h_ref, xin_hbm, xout_hbm, sem, *, tt):
    del xin_hbm

    def copy(r, k):
        d = dest_ref[0, 0, 2 * r + k]
        return pltpu.make_async_copy(h_ref.at[pl.ds(r, 1), :], xout_hbm.at[pl.ds(d, 1), :], sem)

    def start(r, carry):
        copy(r, 0).start()
        copy(r, 1).start()
        return carry

    def wait(r, carry):
        copy(r, 0).wait()
        copy(r, 1).wait()
        return carry

    lax.fori_loop(0, tt, start, 0)
    lax.fori_loop(0, tt, wait, 0)


def _scatter(dest3, h2, rows, tt):
    n, w = h2.shape
    hbm = pl.BlockSpec(memory_space=pl.ANY)
    return pl.pallas_call(
        functools.partial(_scatter_kernel, tt=tt),
        out_shape=jax.ShapeDtypeStruct((rows, w), h2.dtype), grid=(n // tt,),
        in_specs=[pl.BlockSpec((1, 1, 2 * tt), lambda i: (i, 0, 0), memory_space=pltpu.SMEM),
                  pl.BlockSpec((tt, w), lambda i: (i, 0)), hbm],
        out_specs=hbm, scratch_shapes=[pltpu.SemaphoreType.DMA(())],
        input_output_aliases={2: 0},
        compiler_params=_params(("arbitrary",)), name="moe_scatter",
    )(dest3, h2, jnp.zeros((rows, w), h2.dtype))


def _expert_kernel(be_ref, nv_ref, x_ref, w1_hbm, w3_hbm, w2_hbm, y_ref, w1_ref, w3_ref, w2_ref, sem, *, chunk):
    blk = pl.program_id(0)
    e = be_ref[blk]

    @pl.when((blk == 0) | (e != be_ref[jnp.maximum(blk - 1, 0)]))
    def _():
        copies = [pltpu.make_async_copy(src.at[e], dst, sem.at[j]) for j, (src, dst) in
                  enumerate(((w1_hbm, w1_ref), (w3_hbm, w3_ref), (w2_hbm, w2_ref)))]
        for cp in copies:
            cp.start()
        for cp in copies:
            cp.wait()

    @pl.when(blk < nv_ref[0])
    def _():
        x = x_ref[...].astype(BF16)
        acc = jnp.zeros(y_ref.shape, F32)
        for j in range(w1_ref.shape[1] // chunk):
            cols = slice(j * chunk, (j + 1) * chunk)
            a = _mm(x, w1_ref[:, cols])
            mid = a * _sigmoid(a) * _mm(x, w3_ref[:, cols])
            acc = acc + _mm(mid.astype(BF16), w2_ref[cols, :])
        y_ref[...] = acc

    @pl.when(blk >= nv_ref[0])
    def _():
        y_ref[...] = jnp.zeros(y_ref.shape, F32)


def _experts(block_e, nvalid, xbuf, w1, w3, w2, chunk=512):
    rows, d = xbuf.shape
    hbm = pl.BlockSpec(memory_space=pl.ANY)
    blk = pl.BlockSpec((MOE_BLOCK, d), lambda i, be, nv: (i, 0))
    return pl.pallas_call(
        functools.partial(_expert_kernel, chunk=chunk),
        out_shape=jax.ShapeDtypeStruct((rows, d), F32),
        grid_spec=pltpu.PrefetchScalarGridSpec(
            num_scalar_prefetch=2, grid=(rows // MOE_BLOCK,),
            in_specs=[blk, hbm, hbm, hbm], out_specs=blk,
            scratch_shapes=[pltpu.VMEM(w1.shape[1:], BF16), pltpu.VMEM(w3.shape[1:], BF16),
                            pltpu.VMEM(w2.shape[1:], BF16), pltpu.SemaphoreType.DMA((3,))]),
        compiler_params=_params(("arbitrary",)), name="moe_experts",
    )(block_e, nvalid, xbuf, w1, w3, w2)


def _combine_kernel(dest_ref, y_hbm, x_ref, route_ref, o_ref, buf_ref, sem, *, tt):
    def copy(r, k):
        d = dest_ref[0, 0, 2 * r + k]
        return pltpu.make_async_copy(y_hbm.at[pl.ds(d, 1), :], buf_ref.at[k, pl.ds(r, 1), :], sem)

    def start(r, carry):
        copy(r, 0).start()
        copy(r, 1).start()
        return carry

    def wait(r, carry):
        copy(r, 0).wait()
        copy(r, 1).wait()
        return carry

    lax.fori_loop(0, tt, start, 0)
    lax.fori_loop(0, tt, wait, 0)
    rt = route_ref[...]
    o_ref[...] = x_ref[...] + (buf_ref[0] * rt[:, 2:3] + buf_ref[1] * rt[:, 3:4])


def _combine(dest3, ybuf, xf, route, tt):
    n, d = xf.shape
    row = lambda w: pl.BlockSpec((tt, w), lambda i: (i, 0))
    return pl.pallas_call(
        functools.partial(_combine_kernel, tt=tt),
        out_shape=jax.ShapeDtypeStruct((n, d), F32), grid=(n // tt,),
        in_specs=[pl.BlockSpec((1, 1, 2 * tt), lambda i: (i, 0, 0), memory_space=pltpu.SMEM),
                  pl.BlockSpec(memory_space=pl.ANY), row(d), row(LANES)],
        out_specs=row(d),
        scratch_shapes=[pltpu.VMEM((2, tt, d), F32), pltpu.SemaphoreType.DMA(())],
        compiler_params=_params(("arbitrary",)), name="moe_combine",
    )(dest3, ybuf, xf, route)


def _moe(xf, h2, w_router, w1, w3, w2, tt=256):
    n, d = xf.shape
    wr = jnp.pad(w_router, ((0, 0), (0, LANES - N_EXPERTS)))
    route, cnt = _route(h2, wr)
    counts = cnt[0, :N_EXPERTS].astype(I32)
    padded = (counts + MOE_BLOCK - 1) // MOE_BLOCK * MOE_BLOCK
    pend = jnp.cumsum(padded)
    pstart = pend - padded
    dest = pstart[route[:, 0:2].astype(I32)] + route[:, 4:6].astype(I32)
    dest3 = dest.reshape(n // tt, 1, 2 * tt)
    n_blocks = n * 2 // MOE_BLOCK + N_EXPERTS
    rows = n_blocks * MOE_BLOCK
    block_e = jnp.minimum(jnp.searchsorted(pend, jnp.arange(n_blocks) * MOE_BLOCK, side='right'),
                          N_EXPERTS - 1).astype(I32)
    nvalid = (pend[-1:] // MOE_BLOCK).astype(I32)
    xbuf = _scatter(dest3, h2, rows, tt)
    ybuf = _experts(block_e, nvalid, xbuf, w1, w3, w2)
    return _combine(dest3, ybuf, xf, route, tt)


def _tile_row(v, width=LANES):
    v = v.astype(F32)
    return jnp.tile(v, width // v.shape[0])


def _layer_params(l, attn_norm_g, w_in, a_q_norm_g, a_k_norm_g, b_cq_norm_g, b_ckv_norm_g, b_w_uq, b_w_ukv,
                  b_qn_g, b_qr_g, b_kn_g, b_kr_g, c_q_norm_g, c_k_norm_g):
    d = w_in.shape[1]
    sizes = (A_HEADS * HEAD_DIM, HEAD_DIM, HEAD_DIM, IDX_HEADS * IDX_DIM, IDX_DIM, IDX_HEADS,
             b_w_uq.shape[1], b_w_ukv.shape[1], B_ROPE, C_HEADS * HEAD_DIM, C_KV_HEADS * HEAD_DIM,
             C_KV_HEADS * HEAD_DIM, 3 * d)
    qa, ka, va, iq, ik, iw, cq, ckv, kr, qc, kc, vc, gates = jnp.split(w_in[l], np.cumsum(sizes)[:-1].tolist(), axis=1)
    z = lambda k: jnp.zeros((d, k), F32)
    w_r = jnp.concatenate([qa, ka, va, iq, ik, iw, z(LANES - IDX_DIM - IDX_HEADS), cq, ckv,
                           z(B_NOPE), kr, z(LANES - B_NOPE - B_ROPE), qc, kc, vc, gates], axis=1).astype(BF16)
    uq = b_w_uq[l].reshape(-1, B_HEADS, B_NOPE + B_ROPE)
    uq = jnp.pad(uq, ((0, 0), (0, 0), (0, LANES - B_NOPE - B_ROPE))).reshape(-1, B_HEADS * LANES).astype(BF16)
    ukv = b_w_ukv[l].reshape(-1, B_HEADS, B_NOPE + B_V)
    uk = jnp.pad(ukv[:, :, :B_NOPE], ((0, 0), (0, 0), (0, LANES - B_NOPE))).reshape(-1, B_HEADS * LANES)
    uv = ukv[:, :, B_NOPE:].reshape(-1, B_HEADS * B_V)
    wukv = jnp.concatenate([uk, uv], axis=1).astype(BF16)
    zeros = lambda k: jnp.zeros((k,), F32)
    grows = jnp.stack([
        _tile_row(a_q_norm_g[l]),
        jnp.concatenate([a_k_norm_g[l], jnp.ones((HEAD_DIM,), F32)]),
        jnp.concatenate([b_qn_g[l], b_qr_g[l], zeros(LANES - B_NOPE - B_ROPE)]),
        jnp.concatenate([b_kn_g[l], zeros(LANES - B_NOPE)]),
        jnp.concatenate([zeros(B_NOPE), b_kr_g[l], zeros(LANES - B_NOPE - B_ROPE)]),
        _tile_row(c_q_norm_g[l]),
        _tile_row(c_k_norm_g[l]),
        b_ckv_norm_g[l],
    ]).astype(F32)
    return (attn_norm_g[l][None, :], w_r, uq, wukv, b_cq_norm_g[l][None, :], grows)


def _seg_matrices():
    lane = np.arange(LANES)
    g64 = lane // 64
    m64 = (g64[:, None] == g64[None, :]) / 64.0
    gqb = np.where(lane < B_NOPE, 0, np.where(lane < B_NOPE + B_ROPE, 1, 2))
    size = np.where(lane < B_NOPE, B_NOPE, B_ROPE)
    mqb = (gqb[:, None] == gqb[None, :]) / size[None, :]
    return jnp.asarray(np.stack([m64, mqb]), BF16)


def kernel(x, positions, attn_norm_g, w_in, a_q_norm_g, a_k_norm_g, b_cq_norm_g, b_ckv_norm_g, b_w_uq, b_w_ukv,
           b_qn_g, b_qr_g, b_kn_g, b_kr_g, c_q_norm_g, c_k_norm_g, c_sinks, w_a_out, w_b_out, w_c_out, w_o,
           ffn_norm_g, ffn_w1, ffn_w3, ffn_w2, router_w, moe_w1, moe_w3, moe_w2):
    b, s, d = x.shape
    n = b * s
    depth = w_in.shape[0]
    xf = x.reshape(n, d)
    tabs = _rope_tables(positions.reshape(n, 1).astype(F32))
    mseg = _seg_matrices()
    for l in range(depth):
        g, w_r, uq, wukv, gcq, grows = _layer_params(
            l, attn_norm_g, w_in, a_q_norm_g, a_k_norm_g, b_cq_norm_g, b_ckv_norm_g, b_w_uq, b_w_ukv,
            b_qn_g, b_qr_g, b_kn_g, b_kr_g, c_q_norm_g, c_k_norm_g)
        qat, kva, vat, iqt, ik, iwt, qbt, kb, vbt, qc, kc, vc, gt = _proj(xf, g, w_r, uq, wukv, gcq, grows, mseg,
                                                                         tabs)
        ya = _dsa(iqt, iwt, qat, ik, kva, vat, b, s)
        yb = _mla(qbt, kb, vbt, b, s)
        yc = _swa(c_sinks[l], qc, kc, vc, b, s)
        dense = l % 2 == 0
        xf, h2 = _merge(ya, yb, yc, gt, xf, w_a_out[l].astype(BF16), w_b_out[l].astype(BF16),
                        w_c_out[l].astype(BF16), w_o[l].astype(BF16), ffn_norm_g[l][None, :],
                        BF16 if dense else F32)
        if dense:
            xf = _ffn(h2, xf, ffn_w1[l // 2].astype(BF16), ffn_w3[l // 2].astype(BF16), ffn_w2[l // 2].astype(BF16))
        else:
            xf = _moe(xf, h2, router_w[l // 2], moe_w1[l // 2].astype(BF16),
                      moe_w3[l // 2].astype(BF16), moe_w2[l // 2].astype(BF16))
    return xf.reshape(b, s, d)
```

```python
import functools

import numpy as np
import jax
import jax.numpy as jnp
from jax import lax
from jax.experimental import pallas as pl
from jax.experimental.pallas import tpu as pltpu

F32, BF16, I32 = jnp.float32, jnp.bfloat16, jnp.int32

EPS = 1e-6
ROPE_THETA = 10000.0
HEAD_DIM = 64
A_HEADS = 4
IDX_HEADS = 8
IDX_DIM = 32
TOPK_MAX = 256
B_HEADS = 4
B_NOPE = 64
B_ROPE = 32
B_V = 64
C_HEADS = 8
C_KV_HEADS = 2
WINDOW = 128
N_EXPERTS = 8
MOE_BLOCK = 256

LANES = 128
SUB = 8
VMEM_LIMIT = 56 * 1024 * 1024
INT_MIN = -2 ** 31
NEG_BIG = -1e30
LOG2E = 1.4426950408889634

C_QA, C_KVA, C_IQ, C_IKW, C_CQ, C_CKV, C_KR, C_QC, C_KC, C_VC, C_G = (
    0, 256, 384, 640, 768, 1024, 1152, 1280, 1792, 1920, 2048)


def _params(sem):
    return pltpu.CompilerParams(dimension_semantics=sem, vmem_limit_bytes=VMEM_LIMIT)


def _mm(a, b):
    return jnp.dot(a, b, preferred_element_type=F32)


def _mm_nt(a, b):
    return lax.dot_general(a, b, (((1,), (1,)), ((), ())), preferred_element_type=F32)


def _sigmoid(v):
    return 1.0 / (1.0 + jnp.exp(-v))


def _rope_table_kernel(pos_ref, f_ref, sg_ref, c64_ref, s64_ref, c32_ref, s32_ref):
    pos = pos_ref[...]
    a = pos * f_ref[0:1, :]
    c64_ref[...] = jnp.cos(a)
    s64_ref[...] = jnp.sin(a) * sg_ref[0:1, :]
    a = pos * f_ref[1:2, :]
    c32_ref[...] = jnp.cos(a)
    s32_ref[...] = jnp.sin(a) * sg_ref[1:2, :]


def _rope_tables(pos_f, tm=1024):
    n = pos_f.shape[0]
    lane = np.arange(LANES)
    f32 = ROPE_THETA ** (-jnp.arange(32, dtype=F32) / 32)
    f16 = ROPE_THETA ** (-jnp.arange(16, dtype=F32) / 16)
    freqs = jnp.stack([f32[lane % 32], f16[lane % 16]])
    signs = jnp.asarray(np.stack([np.where(lane % 64 < 32, -1.0, 1.0),
                                  np.where(lane % 32 < 16, -1.0, 1.0)]), F32)
    tab = jax.ShapeDtypeStruct((n, LANES), F32)
    row = pl.BlockSpec((tm, LANES), lambda i: (i, 0))
    par = pl.BlockSpec((2, LANES), lambda i: (0, 0))
    return pl.pallas_call(
        _rope_table_kernel, out_shape=(tab,) * 4, grid=(n // tm,),
        in_specs=[pl.BlockSpec((tm, 1), lambda i: (i, 0)), par, par],
        out_specs=(row,) * 4, compiler_params=_params(("arbitrary",)), name="rope_tables",
    )(pos_f, freqs, signs)


def _swap_half(y, half):
    lane = lax.broadcasted_iota(I32, y.shape, 1)
    return jnp.where((lane & half) == 0, pltpu.roll(y, LANES - half, 1), pltpu.roll(y, half, 1))


def _rope(y, cos, sin_signed, half):
    return y * cos + _swap_half(y, half) * sin_signed


def _seg_mean_sq(y, mseg):
    sq = y * y
    hi = sq.astype(BF16)
    lo = (sq - hi.astype(F32)).astype(BF16)
    return _mm(hi, mseg) + _mm(lo, mseg)


def _seg_norm(y, mseg, gain):
    return y * lax.rsqrt(_seg_mean_sq(y, mseg) + EPS) * gain


def _slab(s):
    return slice(s * LANES, (s + 1) * LANES)


def _proj_kernel(x_ref, g_ref, w_ref, wuq_ref, wukv_ref, gcq_ref, grows_ref, mseg_ref,
                 c64_ref, s64_ref, c32_ref, s32_ref,
                 qat_ref, kva_ref, vat_ref, iqt_ref, ik_ref, iwt_ref, qbt_ref, kb_ref, vbt_ref,
                 qc_ref, kc_ref, vc_ref, gt_ref):
    x = x_ref[...]
    h = (x * lax.rsqrt(jnp.mean(x * x, axis=-1, keepdims=True) + EPS) * g_ref[...]).astype(BF16)
    c64, s64, c32, s32 = c64_ref[...], s64_ref[...], c32_ref[...], s32_ref[...]
    m64, mqb = mseg_ref[0], mseg_ref[1]
    lane = lax.broadcasted_iota(I32, c64.shape, 1)
    in_rope = (lane >= B_NOPE) & (lane < B_NOPE + B_ROPE)
    cb = jnp.where(in_rope, c32, 1.0)
    sb = jnp.where(in_rope, s32, 0.0)

    p = _mm(h, w_ref[:, C_QA:C_CQ])
    for s in range(2):
        y = _rope(_seg_norm(p[:, _slab(s)], m64, grows_ref[0:1, :]), c64, s64, 32)
        qat_ref[0, _slab(s), :] = (y * (LOG2E * HEAD_DIM ** -0.5)).T.astype(BF16)
    y = p[:, _slab(2)]
    yr = _rope(_seg_norm(y, m64, grows_ref[1:2, :]), c64, s64, 32)
    kva_ref[...] = jnp.where(lane < HEAD_DIM, yr, y).astype(BF16)
    vat_ref[0] = y.T[HEAD_DIM:2 * HEAD_DIM, :].astype(BF16)
    for s in range(2):
        iqt_ref[0, _slab(s), :] = _rope(p[:, _slab(3 + s)], c32, s32, 16).T.astype(BF16)
    y = p[:, _slab(5)]
    ik_ref[...] = jnp.where(lane < IDX_DIM, _rope(y, c32, s32, 16), 0.0).astype(BF16)
    iwt_ref[0] = (y * (IDX_HEADS * IDX_DIM) ** -0.5).T[IDX_DIM:IDX_DIM + IDX_HEADS, :]

    p = _mm(h, w_ref[:, C_CQ:C_QC])
    cq = p[:, 0:256]
    cqn = cq * lax.rsqrt(jnp.mean(cq * cq, axis=-1, keepdims=True) + EPS) * gcq_ref[...]
    qb = _mm(cqn.astype(BF16), wuq_ref[...])
    for s in range(B_HEADS):
        y = _rope(_seg_norm(qb[:, _slab(s)], mqb, grows_ref[2:3, :]), cb, sb, 16)
        qbt_ref[0, _slab(s), :] = (y * (LOG2E * (B_NOPE + B_ROPE) ** -0.5)).T.astype(BF16)
    ckv = p[:, 256:384]
    ckvn = ckv * lax.rsqrt(jnp.mean(ckv * ckv, axis=-1, keepdims=True) + EPS) * grows_ref[7:8, :]
    kvb = _mm(ckvn.astype(BF16), wukv_ref[...])
    krs = p[:, 384:512]
    kr = krs * lax.rsqrt(jnp.sum(krs * krs, axis=-1, keepdims=True) * (1.0 / B_ROPE) + EPS) * grows_ref[4:5, :]
    kr = _rope(kr, cb, sb, 16)
    for s in range(B_HEADS):
        kb_ref[:, _slab(s)] = (_seg_norm(kvb[:, _slab(s)], m64, grows_ref[3:4, :]) + kr).astype(BF16)
    for s in range(2):
        vbt_ref[0, _slab(s), :] = kvb[:, 512 + s * LANES:512 + (s + 1) * LANES].T.astype(BF16)

    p = _mm(h, w_ref[:, C_QC:C_G])
    for s in range(4):
        y = _rope(_seg_norm(p[:, _slab(s)], m64, grows_ref[5:6, :]), c64, s64, 32)
        qc_ref[:, _slab(s)] = (y * HEAD_DIM ** -0.5).astype(BF16)
    y = _rope(_seg_norm(p[:, _slab(4)], m64, grows_ref[6:7, :]), c64, s64, 32)
    kc_ref[...] = y.astype(BF16)
    vc_ref[...] = p[:, _slab(5)].astype(BF16)

    for c in range(3):
        lo = C_G + c * 1024
        gt_ref[:, c * 1024:(c + 1) * 1024] = _sigmoid(_mm(h, w_ref[:, lo:lo + 1024])).astype(gt_ref.dtype)


def _proj(xf, g, w_r, wuq, wukv, gcq, grows, mseg, tabs, tm=256):
    n, d = xf.shape
    row = lambda w: pl.BlockSpec((tm, w), lambda i: (i, 0))
    full = lambda a: pl.BlockSpec(a.shape, lambda i: (0,) * a.ndim)
    outs = [("t", 256, BF16), ("r", 128, BF16), ("t", HEAD_DIM, BF16), ("t", 256, BF16), ("r", 128, BF16),
            ("t", IDX_HEADS, F32), ("t", 512, BF16), ("r", 512, BF16), ("t", 256, BF16),
            ("r", 512, BF16), ("r", 128, BF16), ("r", 128, BF16), ("r", 3072, BF16)]
    shape = lambda k, w, dt: jax.ShapeDtypeStruct((n // tm, w, tm) if k == "t" else (n, w), dt)
    spec = lambda k, w: pl.BlockSpec((1, w, tm), lambda i: (i, 0, 0)) if k == "t" else row(w)
    return pl.pallas_call(
        _proj_kernel,
        out_shape=tuple(shape(*o) for o in outs),
        grid=(n // tm,),
        in_specs=[row(d), full(g), full(w_r), full(wuq), full(wukv), full(gcq), full(grows), full(mseg)]
        + [row(LANES)] * 4,
        out_specs=tuple(spec(k, w) for k, w, _ in outs),
        compiler_params=_params(("arbitrary",)), name="in_proj",
    )(xf, g, w_r, wuq, wukv, gcq, grows, mseg, *tabs)


def _flash_logits(ks, qts):
    return tuple(_mm(k, qt) for k, qt in zip(ks, qts))


def _flash_update(ss, vts, carries, bias=None):
    if bias is not None:
        ss = [s + bias for s in ss]
    m_new = [jnp.maximum(c[0], jnp.max(s, axis=0, keepdims=True)) for c, s in zip(carries, ss)]
    ps = [jnp.exp2(s - m) for s, m in zip(ss, m_new)]
    pvs = [_mm(vt, p.astype(BF16)) for vt, p in zip(vts, ps)]
    out = []
    for (m, l, acc), mn, p, pv in zip(carries, m_new, ps, pvs):
        alpha = jnp.exp2(m - mn)
        out.append((mn, alpha * l + jnp.sum(p, axis=0, keepdims=True), alpha * acc + pv))
    return tuple(out)


def _walk_chunks(n, step, carry, widths=(4, 2, 1)):
    start = 0
    for w in widths:
        trips = (n - start) // w
        carry = lax.fori_loop(0, trips, lambda j, cr, start=start, w=w: step(start + j * w, w, cr), carry)
        start = start + trips * w
    return carry


def _flash_init(dv, tq):
    return jnp.full((1, tq), NEG_BIG, F32), jnp.zeros((1, tq), F32), jnp.zeros((dv, tq), F32)


def _flash_out(results):
    return jnp.concatenate([acc / l for _, l, acc in results], axis=0).T.astype(BF16)


def _dsa_kernel(iqt_ref, iwt_ref, qat_ref, ik_ref, kv_ref, vat_ref, o_ref, keys_ref, hi_ref, lo_ref, gmax_ref, *,
                tq, nsel, seq):
    per, _, ck = iqt_ref.shape
    i = pl.program_id(1)
    nk = (i + 1) * per

    def lanes(ref):
        return jnp.concatenate([ref[j] for j in range(per)], axis=1)

    iqt = lanes(iqt_ref)
    iwt = lanes(iwt_ref)
    half = ck // (2 * per)
    qpos_h = lax.broadcasted_iota(I32, (half, tq), 1) + i * tq
    krow_h = lax.broadcasted_iota(I32, (half, tq), 0)
    gmax_ref[...] = jnp.full(gmax_ref.shape, -jnp.inf, F32)

    def to_key(score):
        bits = lax.bitcast_convert_type(score, I32)
        return bits ^ ((bits >> 31) & 0x7FFFFFFF)

    def score_body(c, carry):
        for r in range(ck // half):
            rows = pl.ds(pl.multiple_of(c * ck + r * half, half), half)
            ikc = ik_ref[rows, :][:, 0:IDX_DIM]
            acc = jnp.zeros((half, tq), F32)
            for h in range(IDX_HEADS):
                sc = _mm(ikc, iqt[h * IDX_DIM:(h + 1) * IDX_DIM, :])
                acc = acc + jnp.maximum(sc, 0.0) * iwt[h:h + 1, :]
            causal = krow_h + (c * ck + r * half) <= qpos_h
            key = jnp.where(causal, to_key(acc), INT_MIN)
            keys_ref[c, r * half:(r + 1) * half, :] = key
            hi_ref[c, r * half:(r + 1) * half, :] = (key >> 16).astype(jnp.int16)
            lo_ref[c, r * half:(r + 1) * half, :] = (((key ^ 0x8000) << 16) >> 16).astype(jnp.int16)
            gmax_ref[r * half:(r + 1) * half, :] = jnp.maximum(gmax_ref[r * half:(r + 1) * half, :],
                                                               jnp.where(causal, acc, -jnp.inf))
        return carry

    lax.fori_loop(0, nk, score_body, 0)

    def rep(row):
        return jnp.broadcast_to(row, (SUB, tq))

    def tile(v):
        return jnp.concatenate([v] * (ck // SUB), axis=0)

    def count(preds, *ops):
        nacc = 4 // len(preds)

        def body(c, accs):
            accs = [list(a) for a in accs]
            for g in range(ck // SUB):
                k = keys_ref[c, g * SUB:(g + 1) * SUB, :]
                for a, pred in zip(accs, preds):
                    a[g % nacc] = a[g % nacc] + jnp.where(pred(k, *ops), 1.0, 0.0)
            return tuple(tuple(a) for a in accs)
        accs = lax.fori_loop(0, nk, body, ((jnp.zeros((SUB, tq), F32),) * nacc,) * len(preds))
        return [rep(jnp.sum(sum(a[1:], a[0]), axis=0, keepdims=True)) for a in accs]

    gm = gmax_ref[...]
    smin = rep(jnp.min(gm, axis=0, keepdims=True))
    lo0 = jnp.where(smin == -jnp.inf, INT_MIN, to_key(smin))
    hi0 = to_key(rep(jnp.max(gm, axis=0, keepdims=True)))
    n_pos, n_nonneg = count([lambda k: k > 0, lambda k: k >= 0])
    at_zero = (n_pos < nsel) & (n_nonneg >= nsel)
    lo0 = jnp.where(at_zero, 0, jnp.where(n_pos >= nsel, jnp.maximum(lo0, 1), lo0))
    hi0 = jnp.where(at_zero, 0, jnp.where(n_nonneg < nsel, jnp.minimum(hi0, -1), hi0))
    unknown = float(2 * seq)
    low16 = -2 ** 15

    def pack16(v):
        return jnp.concatenate([v, v], axis=0).astype(jnp.int16)

    def count16(ref, t):
        t16 = pack16(t)
        nacc, rows = 4, 2 * SUB

        def body(c, accs):
            accs = list(accs)
            for g in range(ck // rows):
                hit = ref[c, g * rows:(g + 1) * rows, :] >= t16
                accs[g % nacc] = accs[g % nacc] + jnp.where(hit, jnp.int16(1), jnp.int16(0))
            return tuple(accs)
        accs = lax.fori_loop(0, nk, body, (jnp.zeros((rows, tq), jnp.int16),) * nacc)
        total = sum(a.astype(F32) for a in accs)
        return rep(jnp.sum(total, axis=0, keepdims=True))

    def bisect16(ref, lo, hi, above, want, inert):
        def cond(state):
            return (state[0] < 18) & (state[6] < 0.5)

        def body(state):
            it, lo, hi, n_lo, n_above, exact, _ = state
            all_done = jnp.min(jnp.where(lo == hi, 1.0, 0.0))
            mid = lo + ((hi - lo + 1) >> 1)
            cnt = count16(ref, mid)
            up = cnt >= want
            hit = (cnt == want) & (inert < 0.5)
            lo = jnp.where(up, mid, lo)
            n_lo = jnp.where(up, cnt, n_lo)
            n_above = jnp.where(up, n_above, cnt)
            hi = jnp.where(hit, mid, jnp.where(up, hi, mid - 1))
            exact = jnp.where(hit, 1.0, exact)
            return it + 1, lo, hi, n_lo, n_above, exact, all_done

        zero = jnp.zeros((SUB, tq), F32)
        out = lax.while_loop(cond, body, (jnp.int32(0), lo, hi, zero + unknown, above, zero, jnp.float32(0.0)))
        return out[1], out[3], out[4], out[5]

    zeros = jnp.zeros((SUB, tq), F32)
    top, _, n_above, exact1 = bisect16(hi_ref, lo0 >> 16, hi0 >> 16, jnp.where(n_nonneg < nsel, n_nonneg, 0.0),
                                       float(nsel), zeros)
    done1 = at_zero | (exact1 > 0.5)
    top16 = pack16(top)

    def mask_body(c, carry):
        for g in range(ck // (2 * SUB)):
            rows = slice(g * 2 * SUB, (g + 1) * 2 * SUB)
            lo_ref[c, rows, :] = jnp.where(hi_ref[c, rows, :] == top16, lo_ref[c, rows, :], jnp.int16(low16))
        return carry

    lax.fori_loop(0, nk, mask_body, 0)
    full = jnp.full((SUB, tq), low16, I32)
    bottom, n_bottom, _, _ = bisect16(lo_ref, full, jnp.where(done1, low16, -low16 - 1), zeros, nsel - n_above,
                                      jnp.where(done1, 1.0, 0.0))
    thr = jnp.where(at_zero, 0, jnp.where(done1, top << 16, (top << 16) + (bottom - low16)))
    n_thr = jnp.where(at_zero, n_nonneg, jnp.where(done1, float(nsel), n_above + n_bottom))

    tied = (n_thr != nsel) & (thr != INT_MIN)

    @pl.when(jnp.max(jnp.where(tied, 1.0, 0.0)) > 0.0)
    def _():
        n_above, = count([lambda k, t: k > t], thr)
        keep_t = tile(nsel - n_above)
        thr_t = tile(thr)
        tied_t = tile(tied)
        tri = jnp.where(lax.broadcasted_iota(I32, (ck, ck), 1) <= lax.broadcasted_iota(I32, (ck, ck), 0),
                        1.0, 0.0).astype(BF16)

        def strike(c, seen):
            key = keys_ref[c]
            tie = (key == thr_t) & tied_t
            rank = _mm(tri, jnp.where(tie, 1.0, 0.0).astype(BF16)) + tile(seen)
            keys_ref[c] = jnp.where(tie & (rank > keep_t), INT_MIN, key)
            return rep(rank[ck - 1:ck, :])
        lax.fori_loop(0, nk, strike, jnp.zeros((SUB, tq), F32))

    qat = lanes(qat_ref)
    sel_t = tile(jnp.where(thr == INT_MIN, INT_MIN + 1, thr))
    qts = [qat[h * HEAD_DIM:(h + 1) * HEAD_DIM, :] for h in range(A_HEADS)]

    def step(c, width, carry):
        k = kv_ref[pl.ds(pl.multiple_of(c * ck, ck), width * ck), :][:, 0:HEAD_DIM]
        vt = jnp.concatenate([vat_ref[c + j] for j in range(width)], axis=1)
        bias = jnp.concatenate([jnp.where(keys_ref[c + j] >= sel_t, 0.0, NEG_BIG) for j in range(width)], axis=0)
        ss = _flash_logits([k] * A_HEADS, qts)
        return _flash_update(ss, [vt] * A_HEADS, carry, bias)

    carry = (_flash_init(HEAD_DIM, tq),) * A_HEADS
    o_ref[...] = _flash_out(_walk_chunks(nk, step, carry, (4, 2, 1) if per == 1 else (2, 1)))


def _dsa(iqt, iwt, qat, ik, kva, vat, b, s, tq=512):
    n = b * s
    ck = iqt.shape[2]
    tq = min(tq, s)
    nq, nc, per = s // tq, s // ck, tq // ck
    nsel = min(TOPK_MAX, s // 4)
    assert nsel <= ck and tq % ck == 0
    qtile = lambda w: pl.BlockSpec((per, w, ck), lambda bi, i: (bi * nq + i, 0, 0))
    seq_rows = pl.BlockSpec((s, LANES), lambda bi, i: (bi, 0))
    return pl.pallas_call(
        functools.partial(_dsa_kernel, tq=tq, nsel=nsel, seq=s),
        out_shape=jax.ShapeDtypeStruct((n, A_HEADS * HEAD_DIM), BF16), grid=(b, nq),
        in_specs=[qtile(IDX_HEADS * IDX_DIM), qtile(IDX_HEADS), qtile(A_HEADS * HEAD_DIM), seq_rows, seq_rows,
                  pl.BlockSpec((nc, HEAD_DIM, ck), lambda bi, i: (bi, 0, 0))],
        out_specs=pl.BlockSpec((tq, A_HEADS * HEAD_DIM), lambda bi, i: (bi * nq + i, 0)),
        scratch_shapes=[pltpu.VMEM((nc, ck, tq), I32), pltpu.VMEM((nc, ck, tq), jnp.int16),
                        pltpu.VMEM((nc, ck, tq), jnp.int16), pltpu.VMEM((ck, tq), F32)],
        compiler_params=_params(("arbitrary", "arbitrary")), name="dsa_attention",
    )(iqt, iwt, qat, ik, kva, vat)


def _mla_kernel(qt_ref, k_ref, vt_ref, o_ref, *, tq):
    per, _, ck = qt_ref.shape
    i = pl.program_id(1)
    qt = jnp.concatenate([qt_ref[j] for j in range(per)], axis=1)
    kpos = lax.broadcasted_iota(I32, (per * ck, tq), 0)
    qpos = lax.broadcasted_iota(I32, (per * ck, tq), 1)
    diag_bias = jnp.where(kpos <= qpos, 0.0, NEG_BIG)

    heads = range(B_HEADS)
    qts = [qt[_slab(h), :] for h in heads]

    def step(c, width, carry, bias=None):
        rows = pl.ds(pl.multiple_of(c * ck, ck), width * ck)
        vt = jnp.concatenate([vt_ref[c + j] for j in range(width)], axis=1)
        ss = _flash_logits([k_ref[rows, _slab(h)] for h in heads], qts)
        return _flash_update(ss, [vt[h * B_V:(h + 1) * B_V, :] for h in heads], carry, bias)

    carry = (_flash_init(B_V, tq),) * B_HEADS
    carry = _walk_chunks(i * per, step, carry, (4, 2, 1) if per == 1 else (2, 1))
    o_ref[...] = _flash_out(step(i * per, per, carry, diag_bias))


def _mla(qbt, kb, vbt, b, s, tq=512):
    n = b * s
    ck = qbt.shape[2]
    tq = min(tq, s)
    nq, nc, per = s // tq, s // ck, tq // ck
    assert tq % ck == 0
    return pl.pallas_call(
        functools.partial(_mla_kernel, tq=tq),
        out_shape=jax.ShapeDtypeStruct((n, B_HEADS * B_V), BF16), grid=(b, nq),
        in_specs=[pl.BlockSpec((per, B_HEADS * LANES, ck), lambda bi, i: (bi * nq + i, 0, 0)),
                  pl.BlockSpec((s, B_HEADS * LANES), lambda bi, i: (bi, 0)),
                  pl.BlockSpec((nc, B_HEADS * B_V, ck), lambda bi, i: (bi, 0, 0))],
        out_specs=pl.BlockSpec((tq, B_HEADS * B_V), lambda bi, i: (bi * nq + i, 0)),
        compiler_params=_params(("arbitrary", "arbitrary")), name="mla_attention",
    )(qbt, kb, vbt)


def _swa_kernel(sink_ref, q_ref, kp_ref, kc_ref, vp_ref, vc_ref, o_ref, *, tq):
    i = pl.program_id(1)
    q = q_ref[...]
    keys = jnp.concatenate([kp_ref[...], kc_ref[...]], axis=0)
    vals = jnp.concatenate([vp_ref[...], vc_ref[...]], axis=0)
    nkeys = WINDOW + tq
    qpos = lax.broadcasted_iota(I32, (tq, nkeys), 0) + i * tq
    kpos = lax.broadcasted_iota(I32, (tq, nkeys), 1) + i * tq - WINDOW
    visible = (kpos > qpos - WINDOW) & (kpos <= qpos) & (kpos >= 0)
    bias = jnp.where(visible, 0.0, NEG_BIG)
    group = C_HEADS // C_KV_HEADS
    heads = range(C_HEADS)
    ks = [keys[:, j * HEAD_DIM:(j + 1) * HEAD_DIM] for j in range(C_KV_HEADS)]
    vs = [vals[:, j * HEAD_DIM:(j + 1) * HEAD_DIM] for j in range(C_KV_HEADS)]
    ss = [_mm_nt(q[:, h * HEAD_DIM:(h + 1) * HEAD_DIM], ks[h // group]) + bias for h in heads]
    ms = [jnp.maximum(jnp.max(s, axis=1, keepdims=True), sink_ref[h]) for h, s in zip(heads, ss)]
    ps = [jnp.exp(s - m) for s, m in zip(ss, ms)]
    ws = [p / (jnp.sum(p, axis=1, keepdims=True) + jnp.exp(sink_ref[h] - m)) for h, p, m in zip(heads, ps, ms)]
    outs = [_mm(w.astype(BF16), vs[h // group]) for h, w in zip(heads, ws)]
    o_ref[...] = jnp.concatenate(outs, axis=1).astype(BF16)


def _swa(sinks, qc, kc, vc, b, s, tq=256):
    n = b * s
    nq = s // tq
    per = tq // WINDOW
    qrow = lambda w: pl.BlockSpec((tq, w), lambda bi, i: (bi * nq + i, 0))
    prev = pl.BlockSpec((WINDOW, LANES), lambda bi, i: (jnp.maximum((bi * nq + i) * per - 1, 0), 0))
    return pl.pallas_call(
        functools.partial(_swa_kernel, tq=tq),
        out_shape=jax.ShapeDtypeStruct((n, C_HEADS * HEAD_DIM), BF16), grid=(b, nq),
        in_specs=[pl.BlockSpec(memory_space=pltpu.SMEM), qrow(512), prev, qrow(LANES), prev, qrow(LANES)],
        out_specs=qrow(512),
        compiler_params=_params(("arbitrary", "arbitrary")), name="swa_attention",
    )(sinks, qc, kc, kc, vc, vc)


def _merge_kernel(ya_ref, yb_ref, yc_ref, gt_ref, x_ref, wa_ref, wb_ref, wc_ref, wo_ref, g_ref,
                  xo_ref, h_ref):
    d = x_ref.shape[1]
    merged = (gt_ref[:, 0:d] * _mm(ya_ref[...], wa_ref[...])
              + gt_ref[:, d:2 * d] * _mm(yb_ref[...], wb_ref[...])
              + gt_ref[:, 2 * d:3 * d] * _mm(yc_ref[...], wc_ref[...]))
    xn = x_ref[...] + _mm(merged.astype(BF16), wo_ref[...])
    xo_ref[...] = xn
    h = xn * lax.rsqrt(jnp.mean(xn * xn, axis=-1, keepdims=True) + EPS) * g_ref[...]
    h_ref[...] = h.astype(h_ref.dtype)


def _merge(ya, yb, yc, gt, xf, wa, wb, wc, wo, g, h_dtype, tm=512):
    n, d = xf.shape
    row = lambda w: pl.BlockSpec((tm, w), lambda i: (i, 0))
    full = lambda a: pl.BlockSpec(a.shape, lambda i: (0,) * a.ndim)
    return pl.pallas_call(
        _merge_kernel,
        out_shape=(jax.ShapeDtypeStruct((n, d), F32), jax.ShapeDtypeStruct((n, d), h_dtype)),
        grid=(n // tm,),
        in_specs=[row(256), row(256), row(512), row(3 * d), row(d), full(wa), full(wb), full(wc), full(wo),
                  full(g)],
        out_specs=(row(d), row(d)),
        compiler_params=_params(("arbitrary",)), name="merge_out_proj",
    )(ya, yb, yc, gt, xf, wa, wb, wc, wo, g)


def _ffn_kernel(h_ref, x_ref, w1_hbm, w3_hbm, w2_hbm, o_ref, w1_ref, w3_ref, w2_ref, sem, *, chunk):
    @pl.when(pl.program_id(0) == 0)
    def _():
        copies = [pltpu.make_async_copy(src, dst, sem.at[j]) for j, (src, dst) in
                  enumerate(((w1_hbm, w1_ref), (w3_hbm, w3_ref), (w2_hbm, w2_ref)))]
        for cp in copies:
            cp.start()
        for cp in copies:
            cp.wait()

    h = h_ref[...]
    acc = x_ref[...]
    for j in range(w1_ref.shape[1] // chunk):
        cols = slice(j * chunk, (j + 1) * chunk)
        a = _mm(h, w1_ref[:, cols])
        mid = a * _sigmoid(a) * _mm(h, w3_ref[:, cols])
        acc = acc + _mm(mid.astype(BF16), w2_ref[cols, :])
    o_ref[...] = acc


def _ffn(h2, xf, w1, w3, w2, tm=512, chunk=1408):
    n, d = xf.shape
    row = lambda w: pl.BlockSpec((tm, w), lambda i: (i, 0))
    hbm = pl.BlockSpec(memory_space=pl.ANY)
    return pl.pallas_call(
        functools.partial(_ffn_kernel, chunk=chunk),
        out_shape=jax.ShapeDtypeStruct((n, d), F32), grid=(n // tm,),
        in_specs=[row(d), row(d), hbm, hbm, hbm], out_specs=row(d),
        scratch_shapes=[pltpu.VMEM(w1.shape, BF16), pltpu.VMEM(w3.shape, BF16), pltpu.VMEM(w2.shape, BF16),
                        pltpu.SemaphoreType.DMA((3,))],
        compiler_params=_params(("arbitrary",)), name="dense_swiglu",
    )(h2, xf, w1, w3, w2)


def _route_kernel(h_ref, wr_ref, tri_ref, route_ref, cnt_ref, carry_ref):
    @pl.when(pl.program_id(0) == 0)
    def _():
        carry_ref[...] = jnp.zeros_like(carry_ref)

    logits = jnp.dot(h_ref[...], wr_ref[...], precision=lax.Precision.HIGHEST, preferred_element_type=F32)
    lane = lax.broadcasted_iota(I32, logits.shape, 1).astype(F32)
    lg = jnp.where(lane < N_EXPERTS, logits, -jnp.inf)
    m1 = jnp.max(lg, axis=1, keepdims=True)
    e1 = jnp.min(jnp.where(lg == m1, lane, float(LANES)), axis=1, keepdims=True)
    lg2 = jnp.where(lane == e1, -jnp.inf, lg)
    m2 = jnp.max(lg2, axis=1, keepdims=True)
    e2 = jnp.min(jnp.where(lg2 == m2, lane, float(LANES)), axis=1, keepdims=True)
    ex = jnp.exp(m2 - m1)
    g1 = 1.0 / (1.0 + ex)
    g2 = ex / (1.0 + ex)
    onehot = jnp.where((lane == e1) | (lane == e2), 1.0, 0.0)
    before = _mm(tri_ref[...], onehot.astype(BF16)) + carry_ref[0:1, :]
    r1 = jnp.sum(jnp.where(lane == e1, before, 0.0), axis=1, keepdims=True)
    r2 = jnp.sum(jnp.where(lane == e2, before, 0.0), axis=1, keepdims=True)
    out = jnp.zeros_like(logits)
    for idx, val in enumerate((e1, e2, g1, g2, r1, r2)):
        out = jnp.where(lane == idx, val, out)
    route_ref[...] = out
    total = carry_ref[0:1, :] + jnp.sum(onehot, axis=0, keepdims=True)
    carry_ref[...] = jnp.broadcast_to(total, carry_ref.shape)
    cnt_ref[...] = jnp.broadcast_to(total, cnt_ref.shape)


def _route(h2, wr, tm=256):
    n, d = h2.shape
    tri = jnp.asarray(np.tril(np.ones((tm, tm), np.float32), -1), BF16)
    full = lambda a: pl.BlockSpec(a.shape, lambda i: (0,) * a.ndim)
    return pl.pallas_call(
        _route_kernel,
        out_shape=(jax.ShapeDtypeStruct((n, LANES), F32), jax.ShapeDtypeStruct((8, LANES), F32)),
        grid=(n // tm,),
        in_specs=[pl.BlockSpec((tm, d), lambda i: (i, 0)), full(wr), full(tri)],
        out_specs=(pl.BlockSpec((tm, LANES), lambda i: (i, 0)), pl.BlockSpec((8, LANES), lambda i: (0, 0))),
        scratch_shapes=[pltpu.VMEM((8, LANES), F32)],
        compiler_params=_params(("arbitrary",)), name="moe_route",
    )(h2, wr, tri)


def _scatter_kernel(dest_ref, h_ref, xin_hbm, xout_hbm, sem, *, tt):
    del xin_hbm

    def copy(r, k):
        d = dest_ref[0, 0, 2 * r + k]
        return pltpu.make_async_copy(h_ref.at[pl.ds(r, 1), :], xout_hbm.at[pl.ds(d, 1), :], sem)

    def start(r, carry):
        copy(r, 0).start()
        copy(r, 1).start()
        return carry

    def wait(r, carry):
        copy(r, 0).wait()
        copy(r, 1).wait()
        return carry

    lax.fori_loop(0, tt, start, 0)
    lax.fori_loop(0, tt, wait, 0)


def _scatter(dest3, h2, rows, tt):
    n, w = h2.shape
    hbm = pl.BlockSpec(memory_space=pl.ANY)
    return pl.pallas_call(
        functools.partial(_scatter_kernel, tt=tt),
        out_shape=jax.ShapeDtypeStruct((rows, w), h2.dtype), grid=(n // tt,),
        in_specs=[pl.BlockSpec((1, 1, 2 * tt), lambda i: (i, 0, 0), memory_space=pltpu.SMEM),
                  pl.BlockSpec((tt, w), lambda i: (i, 0)), hbm],
        out_specs=hbm, scratch_shapes=[pltpu.SemaphoreType.DMA(())],
        input_output_aliases={2: 0},
        compiler_params=_params(("arbitrary",)), name="moe_scatter",
    )(dest3, h2, jnp.zeros((rows, w), h2.dtype))


def _expert_kernel(be_ref, nv_ref, x_ref, w1_hbm, w3_hbm, w2_hbm, y_ref, w1_ref, w3_ref, w2_ref, sem, *, chunk):
    blk = pl.program_id(0)
    e = be_ref[blk]

    @pl.when((blk == 0) | (e != be_ref[jnp.maximum(blk - 1, 0)]))
    def _():
        copies = [pltpu.make_async_copy(src.at[e], dst, sem.at[j]) for j, (src, dst) in
                  enumerate(((w1_hbm, w1_ref), (w3_hbm, w3_ref), (w2_hbm, w2_ref)))]
        for cp in copies:
            cp.start()
        for cp in copies:
            cp.wait()

    @pl.when(blk < nv_ref[0])
    def _():
        x = x_ref[...].astype(BF16)
        acc = jnp.zeros(y_ref.shape, F32)
        for j in range(w1_ref.shape[1] // chunk):
            cols = slice(j * chunk, (j + 1) * chunk)
            a = _mm(x, w1_ref[:, cols])
            mid = a * _sigmoid(a) * _mm(x, w3_ref[:, cols])
            acc = acc + _mm(mid.astype(BF16), w2_ref[cols, :])
        y_ref[...] = acc

    @pl.when(blk >= nv_ref[0])
    def _():
        y_ref[...] = jnp.zeros(y_ref.shape, F32)


def _experts(block_e, nvalid, xbuf, w1, w3, w2, chunk=512):
    rows, d = xbuf.shape
    hbm = pl.BlockSpec(memory_space=pl.ANY)
    blk = pl.BlockSpec((MOE_BLOCK, d), lambda i, be, nv: (i, 0))
    return pl.pallas_call(
        functools.partial(_expert_kernel, chunk=chunk),
        out_shape=jax.ShapeDtypeStruct((rows, d), F32),
        grid_spec=pltpu.PrefetchScalarGridSpec(
            num_scalar_prefetch=2, grid=(rows // MOE_BLOCK,),
            in_specs=[blk, hbm, hbm, hbm], out_specs=blk,
            scratch_shapes=[pltpu.VMEM(w1.shape[1:], BF16), pltpu.VMEM(w3.shape[1:], BF16),
                            pltpu.VMEM(w2.shape[1:], BF16), pltpu.SemaphoreType.DMA((3,))]),
        compiler_params=_params(("arbitrary",)), name="moe_experts",
    )(block_e, nvalid, xbuf, w1, w3, w2)


def _combine_kernel(dest_ref, y_hbm, x_ref, route_ref, o_ref, buf_ref, sem, *, tt):
    def copy(r, k):
        d = dest_ref[0, 0, 2 * r + k]
        return pltpu.make_async_copy(y_hbm.at[pl.ds(d, 1), :], buf_ref.at[k, pl.ds(r, 1), :], sem)

    def start(r, carry):
        copy(r, 0).start()
        copy(r, 1).start()
        return carry

    def wait(r, carry):
        copy(r, 0).wait()
        copy(r, 1).wait()
        return carry

    lax.fori_loop(0, tt, start, 0)
    lax.fori_loop(0, tt, wait, 0)
    rt = route_ref[...]
    o_ref[...] = x_ref[...] + (buf_ref[0] * rt[:, 2:3] + buf_ref[1] * rt[:, 3:4])


def _combine(dest3, ybuf, xf, route, tt):
    n, d = xf.shape
    row = lambda w: pl.BlockSpec((tt, w), lambda i: (i, 0))
    return pl.pallas_call(
        functools.partial(_combine_kernel, tt=tt),
        out_shape=jax.ShapeDtypeStruct((n, d), F32), grid=(n // tt,),
        in_specs=[pl.BlockSpec((1, 1, 2 * tt), lambda i: (i, 0, 0), memory_space=pltpu.SMEM),
                  pl.BlockSpec(memory_space=pl.ANY), row(d), row(LANES)],
        out_specs=row(d),
        scratch_shapes=[pltpu.VMEM((2, tt, d), F32), pltpu.SemaphoreType.DMA(())],
        compiler_params=_params(("arbitrary",)), name="moe_combine",
    )(dest3, ybuf, xf, route)


def _moe(xf, h2, w_router, w1, w3, w2, tt=256):
    n, d = xf.shape
    wr = jnp.pad(w_router, ((0, 0), (0, LANES - N_EXPERTS)))
    route, cnt = _route(h2, wr)
    counts = cnt[0, :N_EXPERTS].astype(I32)
    padded = (counts + MOE_BLOCK - 1) // MOE_BLOCK * MOE_BLOCK
    pend = jnp.cumsum(padded)
    pstart = pend - padded
    dest = pstart[route[:, 0:2].astype(I32)] + route[:, 4:6].astype(I32)
    dest3 = dest.reshape(n // tt, 1, 2 * tt)
    n_blocks = n * 2 // MOE_BLOCK + N_EXPERTS
    rows = n_blocks * MOE_BLOCK
    block_e = jnp.minimum(jnp.searchsorted(pend, jnp.arange(n_blocks) * MOE_BLOCK, side='right'),
                          N_EXPERTS - 1).astype(I32)
    nvalid = (pend[-1:] // MOE_BLOCK).astype(I32)
    xbuf = _scatter(dest3, h2, rows, tt)
    ybuf = _experts(block_e, nvalid, xbuf, w1, w3, w2)
    return _combine(dest3, ybuf, xf, route, tt)


def _tile_row(v, width=LANES):
    v = v.astype(F32)
    return jnp.tile(v, width // v.shape[0])


def _layer_params(l, attn_norm_g, w_in, a_q_norm_g, a_k_norm_g, b_cq_norm_g, b_ckv_norm_g, b_w_uq, b_w_ukv,
                  b_qn_g, b_qr_g, b_kn_g, b_kr_g, c_q_norm_g, c_k_norm_g):
    d = w_in.shape[1]
    sizes = (A_HEADS * HEAD_DIM, HEAD_DIM, HEAD_DIM, IDX_HEADS * IDX_DIM, IDX_DIM, IDX_HEADS,
             b_w_uq.shape[1], b_w_ukv.shape[1], B_ROPE, C_HEADS * HEAD_DIM, C_KV_HEADS * HEAD_DIM,
             C_KV_HEADS * HEAD_DIM, 3 * d)
    qa, ka, va, iq, ik, iw, cq, ckv, kr, qc, kc, vc, gates = jnp.split(w_in[l], np.cumsum(sizes)[:-1].tolist(), axis=1)
    z = lambda k: jnp.zeros((d, k), F32)
    w_r = jnp.concatenate([qa, ka, va, iq, ik, iw, z(LANES - IDX_DIM - IDX_HEADS), cq, ckv,
                           z(B_NOPE), kr, z(LANES - B_NOPE - B_ROPE), qc, kc, vc, gates], axis=1).astype(BF16)
    uq = b_w_uq[l].reshape(-1, B_HEADS, B_NOPE + B_ROPE)
    uq = jnp.pad(uq, ((0, 0), (0, 0), (0, LANES - B_NOPE - B_ROPE))).reshape(-1, B_HEADS * LANES).astype(BF16)
    ukv = b_w_ukv[l].reshape(-1, B_HEADS, B_NOPE + B_V)
    uk = jnp.pad(ukv[:, :, :B_NOPE], ((0, 0), (0, 0), (0, LANES - B_NOPE))).reshape(-1, B_HEADS * LANES)
    uv = ukv[:, :, B_NOPE:].reshape(-1, B_HEADS * B_V)
    wukv = jnp.concatenate([uk, uv], axis=1).astype(BF16)
    zeros = lambda k: jnp.zeros((k,), F32)
    grows = jnp.stack([
        _tile_row(a_q_norm_g[l]),
        jnp.concatenate([a_k_norm_g[l], jnp.ones((HEAD_DIM,), F32)]),
        jnp.concatenate([b_qn_g[l], b_qr_g[l], zeros(LANES - B_NOPE - B_ROPE)]),
        jnp.concatenate([b_kn_g[l], zeros(LANES - B_NOPE)]),
        jnp.concatenate([zeros(B_NOPE), b_kr_g[l], zeros(LANES - B_NOPE - B_ROPE)]),
        _tile_row(c_q_norm_g[l]),
        _tile_row(c_k_norm_g[l]),
        b_ckv_norm_g[l],
    ]).astype(F32)
    return (attn_norm_g[l][None, :], w_r, uq, wukv, b_cq_norm_g[l][None, :], grows)


def _seg_matrices():
    lane = np.arange(LANES)
    g64 = lane // 64
    m64 = (g64[:, None] == g64[None, :]) / 64.0
    gqb = np.where(lane < B_NOPE, 0, np.where(lane < B_NOPE + B_ROPE, 1, 2))
    size = np.where(lane < B_NOPE, B_NOPE, B_ROPE)
    mqb = (gqb[:, None] == gqb[None, :]) / size[None, :]
    return jnp.asarray(np.stack([m64, mqb]), BF16)


def kernel(x, positions, attn_norm_g, w_in, a_q_norm_g, a_k_norm_g, b_cq_norm_g, b_ckv_norm_g, b_w_uq, b_w_ukv,
           b_qn_g, b_qr_g, b_kn_g, b_kr_g, c_q_norm_g, c_k_norm_g, c_sinks, w_a_out, w_b_out, w_c_out, w_o,
           ffn_norm_g, ffn_w1, ffn_w3, ffn_w2, router_w, moe_w1, moe_w3, moe_w2):
    b, s, d = x.shape
    n = b * s
    depth = w_in.shape[0]
    xf = x.reshape(n, d)
    tabs = _rope_tables(positions.reshape(n, 1).astype(F32))
    mseg = _seg_matrices()
    for l in range(depth):
        g, w_r, uq, wukv, gcq, grows = _layer_params(
            l, attn_norm_g, w_in, a_q_norm_g, a_k_norm_g, b_cq_norm_g, b_ckv_norm_g, b_w_uq, b_w_ukv,
            b_qn_g, b_qr_g, b_kn_g, b_kr_g, c_q_norm_g, c_k_norm_g)
        qat, kva, vat, iqt, ik, iwt, qbt, kb, vbt, qc, kc, vc, gt = _proj(xf, g, w_r, uq, wukv, gcq, grows, mseg,
                                                                         tabs)
        ya = _dsa(iqt, iwt, qat, ik, kva, vat, b, s)
        yb = _mla(qbt, kb, vbt, b, s)
        yc = _swa(c_sinks[l], qc, kc, vc, b, s)
        dense = l % 2 == 0
        xf, h2 = _merge(ya, yb, yc, gt, xf, w_a_out[l].astype(BF16), w_b_out[l].astype(BF16),
                        w_c_out[l].astype(BF16), w_o[l].astype(BF16), ffn_norm_g[l][None, :],
                        BF16 if dense else F32)
        if dense:
            xf = _ffn(h2, xf, ffn_w1[l // 2].astype(BF16), ffn_w3[l // 2].astype(BF16), ffn_w2[l // 2].astype(BF16))
        else:
            xf = _moe(xf, h2, router_w[l // 2], moe_w1[l // 2].astype(BF16),
                      moe_w3[l // 2].astype(BF16), moe_w2[l // 2].astype(BF16))
    return xf.reshape(b, s, d)
```
